```python
import math
import jax, jax.numpy as jnp
from jax import lax
import numpy as np

D_MODEL = 1024
BATCH = 8
SEQ = 8192
DEPTH = 4

CHUNK = 64
N_MIXERS = 3
N_SB_LAYERS = (DEPTH + 2) // 3
N_S5_LAYERS = (DEPTH + 1) // 3
N_CV_LAYERS = DEPTH // 3
SB_HEADS = 16
SB_HEAD_DIM = D_MODEL // SB_HEADS
Q_BLOCK = 128
S5_GROUP = 16
S5_GROUPS = D_MODEL // S5_GROUP
S5_STATE = 64
S5_DT_MIN = 1e-3
S5_DT_MAX = 1e-1
CONV_WIDTH = 31
D_FF = ((8 * D_MODEL + 2) // 3 + 255) // 256 * 256
EPS = 1e-6

kernel_name = "hybrid_stickbreak_s5_conformer_trunk"


def rms_norm(x, g):
    xf = x.astype(jnp.float32)
    y = xf * lax.rsqrt(jnp.mean(xf * xf, axis=-1, keepdims=True) + EPS)
    return (y * g.astype(jnp.float32)).astype(x.dtype)


def modulate(h, shift, scale):
    return h * (1 + scale[:, None, :]) + shift[:, None, :]


def stick_breaking_attention(u, w_qkv, w_o):
    bsz, seq, _ = u.shape
    q, k, v = jnp.split(u @ w_qkv, 3, axis=-1)
    to_heads = lambda t: t.reshape(bsz, seq, SB_HEADS, SB_HEAD_DIM).transpose(0, 2, 1, 3)
    q, k, v = to_heads(q), to_heads(k), to_heads(v)
    scale = SB_HEAD_DIM ** -0.5
    outs = []
    for blk in range(seq // Q_BLOCK):
        q0, q1 = blk * Q_BLOCK, (blk + 1) * Q_BLOCK
        qb, kb, vb = q[:, :, q0:q1], k[:, :, :q1], v[:, :, :q1]
        z = jnp.einsum('bhqd,bhkd->bhqk', qb, kb).astype(jnp.float32) * scale
        t_idx = q0 + jnp.arange(Q_BLOCK)[:, None]
        s_idx = jnp.arange(q1)[None, :]
        mask = s_idx < t_idx
        log_beta = jax.nn.log_sigmoid(z)
        log_keep = jnp.where(mask, jax.nn.log_sigmoid(-z), 0.0)
        log_later = lax.cumsum(log_keep, axis=3, reverse=True) - log_keep
        w = jnp.where(mask, jnp.exp(log_beta + log_later), 0.0)
        outs.append(jnp.einsum('bhqk,bhkd->bhqd', w.astype(vb.dtype), vb))
    o = jnp.concatenate(outs, axis=2).transpose(0, 2, 1, 3).reshape(bsz, seq, D_MODEL)
    return o @ w_o


def s5_layer(u, lam_re, lam_im, log_dt, b_re, b_im, c_re, c_im, d_skip, w_glu, b_glu):
    bsz, seq, _ = u.shape
    uf = u.astype(jnp.float32)
    ug = uf.reshape(bsz, seq, S5_GROUPS, S5_GROUP)
    dt = jnp.exp(log_dt.astype(jnp.float32))[:, None]
    lr, li = lam_re.astype(jnp.float32), lam_im.astype(jnp.float32)
    mag = jnp.exp(lr * dt)
    ar, ai = mag * jnp.cos(li * dt), mag * jnp.sin(li * dt)
    den = lr * lr + li * li
    er = ((ar - 1) * lr + ai * li) / den
    ei = (ai * lr - (ar - 1) * li) / den
    br, bi = b_re.astype(jnp.float32), b_im.astype(jnp.float32)
    bbr = er[..., None] * br - ei[..., None] * bi
    bbi = er[..., None] * bi + ei[..., None] * br
    bu_r = jnp.einsum('bsgc,gpc->bsgp', ug, bbr)
    bu_i = jnp.einsum('bsgc,gpc->bsgp', ug, bbi)
    a_r = jnp.broadcast_to(ar, (1, seq) + ar.shape)
    a_i = jnp.broadcast_to(ai, (1, seq) + ai.shape)

    def combine(e1, e2):
        a1r, a1i, b1r, b1i = e1
        a2r, a2i, b2r, b2i = e2
        return (a1r * a2r - a1i * a2i,
                a1r * a2i + a1i * a2r,
                a2r * b1r - a2i * b1i + b2r,
                a2r * b1i + a2i * b1r + b2i)

    _, _, xr, xi = lax.associative_scan(combine, (a_r, a_i, bu_r, bu_i), axis=1)
    y = (jnp.einsum('bsgp,gcp->bsgc', xr, c_re.astype(jnp.float32))
         - jnp.einsum('bsgp,gcp->bsgc', xi, c_im.astype(jnp.float32)))
    y = y.reshape(bsz, seq, D_MODEL) + d_skip.astype(jnp.float32) * uf
    y = jax.nn.gelu(y).astype(u.dtype)
    ga, gb = jnp.split(y @ w_glu + b_glu, 2, axis=-1)
    return ga * jax.nn.sigmoid(gb)


def conformer_conv(u, w_pw1, b_pw1, w_dw, b_dw, ln_g, ln_b, w_pw2, b_pw2):
    ga, gb = jnp.split(u @ w_pw1 + b_pw1, 2, axis=-1)
    h = ga * jax.nn.sigmoid(gb)
    hp = jnp.pad(h, ((0, 0), (CONV_WIDTH - 1, 0), (0, 0)))
    h = lax.conv_general_dilated(hp, w_dw[:, None, :], window_strides=(1,), padding='VALID',
                                 dimension_numbers=('NWC', 'WIO', 'NWC'),
                                 feature_group_count=D_MODEL) + b_dw
    hf = h.astype(jnp.float32)
    mu = jnp.mean(hf, axis=-1, keepdims=True)
    var = jnp.mean(jnp.square(hf - mu), axis=-1, keepdims=True)
    h = ((hf - mu) * lax.rsqrt(var + EPS) * ln_g.astype(jnp.float32)
         + ln_b.astype(jnp.float32)).astype(u.dtype)
    h = jax.nn.silu(h)
    return h @ w_pw2 + b_pw2


def swiglu(u, w_gate, w_up, w_down):
    return (jax.nn.silu(u @ w_gate) * (u @ w_up)) @ w_down


def _fwd_setup_inputs(seed: int = 0) -> dict:
    key = jax.random.key(seed)
    keys = list(jax.random.split(key, 40))
    nk = lambda: keys.pop()
    nrm = lambda shape, std: jax.random.normal(nk(), shape, jnp.float32) * std
    D, F, G, P, GC = D_MODEL, D_FF, S5_GROUPS, S5_STATE, S5_GROUP
    NA, NB, NC = N_SB_LAYERS, N_S5_LAYERS, N_CV_LAYERS
    n = jnp.arange(P, dtype=jnp.float32)
    return {
        "x": nrm((BATCH, SEQ, D), 1.0),
        "c": nrm((BATCH, D), 1.0),
        "norm_g": 1.0 + nrm((DEPTH, 4, D), 0.05),
        "w_mod": nrm((DEPTH, D, 6 * D), 0.5 * D ** -0.5),
        "b_mod": nrm((DEPTH, 6 * D), 0.01),
        "sb_w_qkv": nrm((NA, D, 3 * D), D ** -0.5),
        "sb_w_o": nrm((NA, D, D), D ** -0.5),
        "s5_lam_re": -0.5 + nrm((NB, G, P), 0.01),
        "s5_lam_im": jnp.pi * n + nrm((NB, G, P), 0.01),
        "s5_log_dt": jax.random.uniform(nk(), (NB, G), jnp.float32,
                                        minval=math.log(S5_DT_MIN), maxval=math.log(S5_DT_MAX)),
        "s5_b_re": nrm((NB, G, P, GC), (2 * GC) ** -0.5),
        "s5_b_im": nrm((NB, G, P, GC), (2 * GC) ** -0.5),
        "s5_c_re": nrm((NB, G, GC, P), (2 * P) ** -0.5),
        "s5_c_im": nrm((NB, G, GC, P), (2 * P) ** -0.5),
        "s5_d": nrm((NB, D), 1.0),
        "s5_w_glu": nrm((NB, D, 2 * D), D ** -0.5),
        "s5_b_glu": nrm((NB, 2 * D), 0.01),
        "cv_w_pw1": nrm((NC, D, 2 * D), D ** -0.5),
        "cv_b_pw1": nrm((NC, 2 * D), 0.01),
        "cv_w_dw": nrm((NC, CONV_WIDTH, D), CONV_WIDTH ** -0.5),
        "cv_b_dw": nrm((NC, D), 0.01),
        "cv_ln_g": 1.0 + nrm((NC, D), 0.05),
        "cv_ln_b": nrm((NC, D), 0.01),
        "cv_w_pw2": nrm((NC, D, D), D ** -0.5),
        "cv_b_pw2": nrm((NC, D), 0.01),
        "ffn_w_gate": nrm((DEPTH, D, F), D ** -0.5),
        "ffn_w_up": nrm((DEPTH, D, F), D ** -0.5),
        "ffn_w_down": nrm((DEPTH, F, D), F ** -0.5),
    }


def _fwd_reference(x, c, norm_g, w_mod, b_mod, sb_w_qkv, sb_w_o,
              s5_lam_re, s5_lam_im, s5_log_dt, s5_b_re, s5_b_im, s5_c_re, s5_c_im,
              s5_d, s5_w_glu, s5_b_glu,
              cv_w_pw1, cv_b_pw1, cv_w_dw, cv_b_dw, cv_ln_g, cv_ln_b, cv_w_pw2, cv_b_pw2,
              ffn_w_gate, ffn_w_up, ffn_w_down):
    mod_all = jnp.einsum('bd,lde->lbe', jax.nn.silu(c), w_mod) + b_mod[:, None, :]
    h = x
    for layer in range(DEPTH):
        sh_m, sc_m, g_m, sh_f, sc_f, g_f = jnp.split(mod_all[layer], 6, axis=-1)
        kind, j = layer % N_MIXERS, layer // N_MIXERS
        u = modulate(rms_norm(h, norm_g[layer, 0]), sh_m, sc_m)
        if kind == 0:
            m = stick_breaking_attention(u, sb_w_qkv[j], sb_w_o[j])
        elif kind == 1:
            m = s5_layer(u, s5_lam_re[j], s5_lam_im[j], s5_log_dt[j], s5_b_re[j], s5_b_im[j],
                         s5_c_re[j], s5_c_im[j], s5_d[j], s5_w_glu[j], s5_b_glu[j])
        else:
            m = conformer_conv(u, cv_w_pw1[j], cv_b_pw1[j], cv_w_dw[j], cv_b_dw[j],
                               cv_ln_g[j], cv_ln_b[j], cv_w_pw2[j], cv_b_pw2[j])
        h = h + g_m[:, None, :] * rms_norm(m, norm_g[layer, 1])
        u = modulate(rms_norm(h, norm_g[layer, 2]), sh_f, sc_f)
        f = swiglu(u, ffn_w_gate[layer], ffn_w_up[layer], ffn_w_down[layer])
        h = h + g_f[:, None, :] * rms_norm(f, norm_g[layer, 3])
    return h


import jax as _jax
import jax.numpy as _jnp

TWIN_FORMAT = 'train_step'
FWD_PARAMS = ['x', 'c', 'norm_g', 'w_mod', 'b_mod', 'sb_w_qkv', 'sb_w_o', 's5_lam_re', 's5_lam_im', 's5_log_dt', 's5_b_re', 's5_b_im', 's5_c_re', 's5_c_im', 's5_d', 's5_w_glu', 's5_b_glu', 'cv_w_pw1', 'cv_b_pw1', 'cv_w_dw', 'cv_b_dw', 'cv_ln_g', 'cv_ln_b', 'cv_w_pw2', 'cv_b_pw2', 'ffn_w_gate', 'ffn_w_up', 'ffn_w_down']
TWIN_WEIGHTS = ['norm_g', 'w_mod', 'b_mod', 'sb_w_qkv', 'sb_w_o', 's5_lam_re', 's5_lam_im', 's5_log_dt', 's5_b_re', 's5_b_im', 's5_c_re', 's5_c_im', 's5_d', 's5_w_glu', 's5_b_glu', 'cv_w_pw1', 'cv_b_pw1', 'cv_w_dw', 'cv_b_dw', 'cv_ln_g', 'cv_ln_b', 'cv_w_pw2', 'cv_b_pw2', 'ffn_w_gate', 'ffn_w_up', 'ffn_w_down']
TWIN_DIFF_INPUT = 'x'
TWIN_INPUTS = ['x', 'c', 'norm_g', 'w_mod', 'b_mod', 'sb_w_qkv', 'sb_w_o', 's5_lam_re', 's5_lam_im', 's5_log_dt', 's5_b_re', 's5_b_im', 's5_c_re', 's5_c_im', 's5_d', 's5_w_glu', 's5_b_glu', 'cv_w_pw1', 'cv_b_pw1', 'cv_w_dw', 'cv_b_dw', 'cv_ln_g', 'cv_ln_b', 'cv_w_pw2', 'cv_b_pw2', 'ffn_w_gate', 'ffn_w_up', 'ffn_w_down', 'loss_target', 'm_norm_g', 'm_w_mod', 'm_b_mod', 'm_sb_w_qkv', 'm_sb_w_o', 'm_s5_lam_re', 'm_s5_lam_im', 'm_s5_log_dt', 'm_s5_b_re', 'm_s5_b_im', 'm_s5_c_re', 'm_s5_c_im', 'm_s5_d', 'm_s5_w_glu', 'm_s5_b_glu', 'm_cv_w_pw1', 'm_cv_b_pw1', 'm_cv_w_dw', 'm_cv_b_dw', 'm_cv_ln_g', 'm_cv_ln_b', 'm_cv_w_pw2', 'm_cv_b_pw2', 'm_ffn_w_gate', 'm_ffn_w_up', 'm_ffn_w_down', 'v_norm_g', 'v_w_mod', 'v_b_mod', 'v_sb_w_qkv', 'v_sb_w_o', 'v_s5_lam_re', 'v_s5_lam_im', 'v_s5_log_dt', 'v_s5_b_re', 'v_s5_b_im', 'v_s5_c_re', 'v_s5_c_im', 'v_s5_d', 'v_s5_w_glu', 'v_s5_b_glu', 'v_cv_w_pw1', 'v_cv_b_pw1', 'v_cv_w_dw', 'v_cv_b_dw', 'v_cv_ln_g', 'v_cv_ln_b', 'v_cv_w_pw2', 'v_cv_b_pw2', 'v_ffn_w_gate', 'v_ffn_w_up', 'v_ffn_w_down']
TWIN_OUTPUTS = ['loss', 'grad_x', 'grad_norm_g', 'grad_w_mod', 'grad_b_mod', 'grad_sb_w_qkv', 'grad_sb_w_o', 'grad_s5_lam_re', 'grad_s5_lam_im', 'grad_s5_log_dt', 'grad_s5_b_re', 'grad_s5_b_im', 'grad_s5_c_re', 'grad_s5_c_im', 'grad_s5_d', 'grad_s5_w_glu', 'grad_s5_b_glu', 'grad_cv_w_pw1', 'grad_cv_b_pw1', 'grad_cv_w_dw', 'grad_cv_b_dw', 'grad_cv_ln_g', 'grad_cv_ln_b', 'grad_cv_w_pw2', 'grad_cv_b_pw2', 'grad_ffn_w_gate', 'grad_ffn_w_up', 'grad_ffn_w_down', 'delta_norm_g', 'delta_w_mod', 'delta_b_mod', 'delta_sb_w_qkv', 'delta_sb_w_o', 'delta_s5_lam_re', 'delta_s5_lam_im', 'delta_s5_log_dt', 'delta_s5_b_re', 'delta_s5_b_im', 'delta_s5_c_re', 'delta_s5_c_im', 'delta_s5_d', 'delta_s5_w_glu', 'delta_s5_b_glu', 'delta_cv_w_pw1', 'delta_cv_b_pw1', 'delta_cv_w_dw', 'delta_cv_b_dw', 'delta_cv_ln_g', 'delta_cv_ln_b', 'delta_cv_w_pw2', 'delta_cv_b_pw2', 'delta_ffn_w_gate', 'delta_ffn_w_up', 'delta_ffn_w_down', 'new_m_norm_g', 'new_m_w_mod', 'new_m_b_mod', 'new_m_sb_w_qkv', 'new_m_sb_w_o', 'new_m_s5_lam_re', 'new_m_s5_lam_im', 'new_m_s5_log_dt', 'new_m_s5_b_re', 'new_m_s5_b_im', 'new_m_s5_c_re', 'new_m_s5_c_im', 'new_m_s5_d', 'new_m_s5_w_glu', 'new_m_s5_b_glu', 'new_m_cv_w_pw1', 'new_m_cv_b_pw1', 'new_m_cv_w_dw', 'new_m_cv_b_dw', 'new_m_cv_ln_g', 'new_m_cv_ln_b', 'new_m_cv_w_pw2', 'new_m_cv_b_pw2', 'new_m_ffn_w_gate', 'new_m_ffn_w_up', 'new_m_ffn_w_down', 'new_v_norm_g', 'new_v_w_mod', 'new_v_b_mod', 'new_v_sb_w_qkv', 'new_v_sb_w_o', 'new_v_s5_lam_re', 'new_v_s5_lam_im', 'new_v_s5_log_dt', 'new_v_s5_b_re', 'new_v_s5_b_im', 'new_v_s5_c_re', 'new_v_s5_c_im', 'new_v_s5_d', 'new_v_s5_w_glu', 'new_v_s5_b_glu', 'new_v_cv_w_pw1', 'new_v_cv_b_pw1', 'new_v_cv_w_dw', 'new_v_cv_b_dw', 'new_v_cv_ln_g', 'new_v_cv_ln_b', 'new_v_cv_w_pw2', 'new_v_cv_b_pw2', 'new_v_ffn_w_gate', 'new_v_ffn_w_up', 'new_v_ffn_w_down']
TWIN_LEAF_KINDS = {'loss': 'loss', 'grad_x': 'grad_x', 'grad_norm_g': 'grad_w', 'grad_w_mod': 'grad_w', 'grad_b_mod': 'grad_w', 'grad_sb_w_qkv': 'grad_w', 'grad_sb_w_o': 'grad_w', 'grad_s5_lam_re': 'grad_w', 'grad_s5_lam_im': 'grad_w', 'grad_s5_log_dt': 'grad_w', 'grad_s5_b_re': 'grad_w', 'grad_s5_b_im': 'grad_w', 'grad_s5_c_re': 'grad_w', 'grad_s5_c_im': 'grad_w', 'grad_s5_d': 'grad_w', 'grad_s5_w_glu': 'grad_w', 'grad_s5_b_glu': 'grad_w', 'grad_cv_w_pw1': 'grad_w', 'grad_cv_b_pw1': 'grad_w', 'grad_cv_w_dw': 'grad_w', 'grad_cv_b_dw': 'grad_w', 'grad_cv_ln_g': 'grad_w', 'grad_cv_ln_b': 'grad_w', 'grad_cv_w_pw2': 'grad_w', 'grad_cv_b_pw2': 'grad_w', 'grad_ffn_w_gate': 'grad_w', 'grad_ffn_w_up': 'grad_w', 'grad_ffn_w_down': 'grad_w', 'delta_norm_g': 'delta_w', 'delta_w_mod': 'delta_w', 'delta_b_mod': 'delta_w', 'delta_sb_w_qkv': 'delta_w', 'delta_sb_w_o': 'delta_w', 'delta_s5_lam_re': 'delta_w', 'delta_s5_lam_im': 'delta_w', 'delta_s5_log_dt': 'delta_w', 'delta_s5_b_re': 'delta_w', 'delta_s5_b_im': 'delta_w', 'delta_s5_c_re': 'delta_w', 'delta_s5_c_im': 'delta_w', 'delta_s5_d': 'delta_w', 'delta_s5_w_glu': 'delta_w', 'delta_s5_b_glu': 'delta_w', 'delta_cv_w_pw1': 'delta_w', 'delta_cv_b_pw1': 'delta_w', 'delta_cv_w_dw': 'delta_w', 'delta_cv_b_dw': 'delta_w', 'delta_cv_ln_g': 'delta_w', 'delta_cv_ln_b': 'delta_w', 'delta_cv_w_pw2': 'delta_w', 'delta_cv_b_pw2': 'delta_w', 'delta_ffn_w_gate': 'delta_w', 'delta_ffn_w_up': 'delta_w', 'delta_ffn_w_down': 'delta_w', 'new_m_norm_g': 'new_m', 'new_m_w_mod': 'new_m', 'new_m_b_mod': 'new_m', 'new_m_sb_w_qkv': 'new_m', 'new_m_sb_w_o': 'new_m', 'new_m_s5_lam_re': 'new_m', 'new_m_s5_lam_im': 'new_m', 'new_m_s5_log_dt': 'new_m', 'new_m_s5_b_re': 'new_m', 'new_m_s5_b_im': 'new_m', 'new_m_s5_c_re': 'new_m', 'new_m_s5_c_im': 'new_m', 'new_m_s5_d': 'new_m', 'new_m_s5_w_glu': 'new_m', 'new_m_s5_b_glu': 'new_m', 'new_m_cv_w_pw1': 'new_m', 'new_m_cv_b_pw1': 'new_m', 'new_m_cv_w_dw': 'new_m', 'new_m_cv_b_dw': 'new_m', 'new_m_cv_ln_g': 'new_m', 'new_m_cv_ln_b': 'new_m', 'new_m_cv_w_pw2': 'new_m', 'new_m_cv_b_pw2': 'new_m', 'new_m_ffn_w_gate': 'new_m', 'new_m_ffn_w_up': 'new_m', 'new_m_ffn_w_down': 'new_m', 'new_v_norm_g': 'new_v', 'new_v_w_mod': 'new_v', 'new_v_b_mod': 'new_v', 'new_v_sb_w_qkv': 'new_v', 'new_v_sb_w_o': 'new_v', 'new_v_s5_lam_re': 'new_v', 'new_v_s5_lam_im': 'new_v', 'new_v_s5_log_dt': 'new_v', 'new_v_s5_b_re': 'new_v', 'new_v_s5_b_im': 'new_v', 'new_v_s5_c_re': 'new_v', 'new_v_s5_c_im': 'new_v', 'new_v_s5_d': 'new_v', 'new_v_s5_w_glu': 'new_v', 'new_v_s5_b_glu': 'new_v', 'new_v_cv_w_pw1': 'new_v', 'new_v_cv_b_pw1': 'new_v', 'new_v_cv_w_dw': 'new_v', 'new_v_cv_b_dw': 'new_v', 'new_v_cv_ln_g': 'new_v', 'new_v_cv_ln_b': 'new_v', 'new_v_cv_w_pw2': 'new_v', 'new_v_cv_b_pw2': 'new_v', 'new_v_ffn_w_gate': 'new_v', 'new_v_ffn_w_up': 'new_v', 'new_v_ffn_w_down': 'new_v'}


def _forward(args):
    return _fwd_reference(*[args[k] for k in FWD_PARAMS])


def _output_shape():
    def fwd():
        inp = _fwd_setup_inputs(0)
        return _fwd_reference(*[inp[k] for k in FWD_PARAMS])
    out = _jax.eval_shape(fwd)
    return out.shape, out.dtype

N_MICROBATCH = 1
ADAM_LR = 0.001
ADAM_B1 = 0.9
ADAM_B2 = 0.999
ADAM_EPS = 1e-08
ADAM_WD = 0.01
ADAM_STEP = 10
PER_EXAMPLE_BATCH_AXIS = {'x': 0, 'c': 0, 'loss_target': 0}
SHARED_INPUTS = []
_WEIGHT_DTYPES = {'norm_g': _jnp.float32, 'w_mod': _jnp.float32, 'b_mod': _jnp.float32, 'sb_w_qkv': _jnp.float32, 'sb_w_o': _jnp.float32, 's5_lam_re': _jnp.float32, 's5_lam_im': _jnp.float32, 's5_log_dt': _jnp.float32, 's5_b_re': _jnp.float32, 's5_b_im': _jnp.float32, 's5_c_re': _jnp.float32, 's5_c_im': _jnp.float32, 's5_d': _jnp.float32, 's5_w_glu': _jnp.float32, 's5_b_glu': _jnp.float32, 'cv_w_pw1': _jnp.float32, 'cv_b_pw1': _jnp.float32, 'cv_w_dw': _jnp.float32, 'cv_b_dw': _jnp.float32, 'cv_ln_g': _jnp.float32, 'cv_ln_b': _jnp.float32, 'cv_w_pw2': _jnp.float32, 'cv_b_pw2': _jnp.float32, 'ffn_w_gate': _jnp.float32, 'ffn_w_up': _jnp.float32, 'ffn_w_down': _jnp.float32}
MOMENT_SCALE = {'norm_g': 4.386265e+00, 'w_mod': 2.478963e+00, 'b_mod': 5.385411e+00, 'sb_w_qkv': 4.317993e-01, 'sb_w_o': 7.695449e-01, 's5_lam_re': 3.904714e-02, 's5_lam_im': 5.176348e-02, 's5_log_dt': 5.365901e+00, 's5_b_re': 3.865850e-02, 's5_b_im': 4.041648e-02, 's5_c_re': 7.308590e-02, 's5_c_im': 7.844321e-02, 's5_d': 1.280057e+00, 's5_w_glu': 8.927293e-01, 's5_b_glu': 2.357088e+00, 'cv_w_pw1': 3.158723e-01, 'cv_b_pw1': 1.041095e+00, 'cv_w_dw': 4.774110e-01, 'cv_b_dw': 2.539801e+00, 'cv_ln_g': 1.162309e+00, 'cv_ln_b': 1.634150e+00, 'cv_w_pw2': 7.915688e-01, 'cv_b_pw2': 3.288348e+00, 'ffn_w_gate': 1.151845e-01, 'ffn_w_up': 1.507414e-01, 'ffn_w_down': 2.531786e-01}


def _to_microbatches(a, axis):
    t = _jnp.moveaxis(a, axis, 0)
    t = t.reshape((N_MICROBATCH, t.shape[0] // N_MICROBATCH) + t.shape[1:])
    return _jnp.moveaxis(t, 1, axis + 1)


def setup_inputs(seed: int = 0) -> dict:
    inp = _fwd_setup_inputs(seed)
    key = _jax.random.fold_in(_jax.random.key(seed), 7919)
    shape, _ = _output_shape()
    out = dict(inp)
    out["loss_target"] = _jax.random.normal(_jax.random.fold_in(key, 0), shape, _jnp.float32)
    for i, name in enumerate(TWIN_WEIGHTS):
        w = inp[name].astype(_jnp.float32)
        if MOMENT_SCALE is None:
            s = _jnp.sqrt(_jnp.mean(_jnp.square(w)) + 1e-30)
        else:
            s = MOMENT_SCALE[name]
        km, kv = _jax.random.split(_jax.random.fold_in(key, i + 1))
        out[name] = w
        out["m_" + name] = s * _jax.random.normal(km, w.shape, _jnp.float32)
        out["v_" + name] = (s * s) * _jax.random.uniform(kv, w.shape, _jnp.float32, 0.5, 1.5)
    if N_MICROBATCH > 1:
        for name, axis in PER_EXAMPLE_BATCH_AXIS.items():
            out[name] = _to_microbatches(out[name], axis)
    return {'x': out['x'], 'c': out['c'], 'norm_g': out['norm_g'], 'w_mod': out['w_mod'], 'b_mod': out['b_mod'], 'sb_w_qkv': out['sb_w_qkv'], 'sb_w_o': out['sb_w_o'], 's5_lam_re': out['s5_lam_re'], 's5_lam_im': out['s5_lam_im'], 's5_log_dt': out['s5_log_dt'], 's5_b_re': out['s5_b_re'], 's5_b_im': out['s5_b_im'], 's5_c_re': out['s5_c_re'], 's5_c_im': out['s5_c_im'], 's5_d': out['s5_d'], 's5_w_glu': out['s5_w_glu'], 's5_b_glu': out['s5_b_glu'], 'cv_w_pw1': out['cv_w_pw1'], 'cv_b_pw1': out['cv_b_pw1'], 'cv_w_dw': out['cv_w_dw'], 'cv_b_dw': out['cv_b_dw'], 'cv_ln_g': out['cv_ln_g'], 'cv_ln_b': out['cv_ln_b'], 'cv_w_pw2': out['cv_w_pw2'], 'cv_b_pw2': out['cv_b_pw2'], 'ffn_w_gate': out['ffn_w_gate'], 'ffn_w_up': out['ffn_w_up'], 'ffn_w_down': out['ffn_w_down'], 'loss_target': out['loss_target'], 'm_norm_g': out['m_norm_g'], 'm_w_mod': out['m_w_mod'], 'm_b_mod': out['m_b_mod'], 'm_sb_w_qkv': out['m_sb_w_qkv'], 'm_sb_w_o': out['m_sb_w_o'], 'm_s5_lam_re': out['m_s5_lam_re'], 'm_s5_lam_im': out['m_s5_lam_im'], 'm_s5_log_dt': out['m_s5_log_dt'], 'm_s5_b_re': out['m_s5_b_re'], 'm_s5_b_im': out['m_s5_b_im'], 'm_s5_c_re': out['m_s5_c_re'], 'm_s5_c_im': out['m_s5_c_im'], 'm_s5_d': out['m_s5_d'], 'm_s5_w_glu': out['m_s5_w_glu'], 'm_s5_b_glu': out['m_s5_b_glu'], 'm_cv_w_pw1': out['m_cv_w_pw1'], 'm_cv_b_pw1': out['m_cv_b_pw1'], 'm_cv_w_dw': out['m_cv_w_dw'], 'm_cv_b_dw': out['m_cv_b_dw'], 'm_cv_ln_g': out['m_cv_ln_g'], 'm_cv_ln_b': out['m_cv_ln_b'], 'm_cv_w_pw2': out['m_cv_w_pw2'], 'm_cv_b_pw2': out['m_cv_b_pw2'], 'm_ffn_w_gate': out['m_ffn_w_gate'], 'm_ffn_w_up': out['m_ffn_w_up'], 'm_ffn_w_down': out['m_ffn_w_down'], 'v_norm_g': out['v_norm_g'], 'v_w_mod': out['v_w_mod'], 'v_b_mod': out['v_b_mod'], 'v_sb_w_qkv': out['v_sb_w_qkv'], 'v_sb_w_o': out['v_sb_w_o'], 'v_s5_lam_re': out['v_s5_lam_re'], 'v_s5_lam_im': out['v_s5_lam_im'], 'v_s5_log_dt': out['v_s5_log_dt'], 'v_s5_b_re': out['v_s5_b_re'], 'v_s5_b_im': out['v_s5_b_im'], 'v_s5_c_re': out['v_s5_c_re'], 'v_s5_c_im': out['v_s5_c_im'], 'v_s5_d': out['v_s5_d'], 'v_s5_w_glu': out['v_s5_w_glu'], 'v_s5_b_glu': out['v_s5_b_glu'], 'v_cv_w_pw1': out['v_cv_w_pw1'], 'v_cv_b_pw1': out['v_cv_b_pw1'], 'v_cv_w_dw': out['v_cv_w_dw'], 'v_cv_b_dw': out['v_cv_b_dw'], 'v_cv_ln_g': out['v_cv_ln_g'], 'v_cv_ln_b': out['v_cv_ln_b'], 'v_cv_w_pw2': out['v_cv_w_pw2'], 'v_cv_b_pw2': out['v_cv_b_pw2'], 'v_ffn_w_gate': out['v_ffn_w_gate'], 'v_ffn_w_up': out['v_ffn_w_up'], 'v_ffn_w_down': out['v_ffn_w_down']}


def _loss(weights, diff, rest, loss_target):
    with _jax.named_scope("forward"):
        args = {**rest, TWIN_DIFF_INPUT: diff, **{k: w.astype(_WEIGHT_DTYPES[k]) for k, w in weights.items()}}
        y = _forward(args)
    with _jax.named_scope("loss_head"):
        err = _jnp.square(y.astype(_jnp.float32) - loss_target)
        return 0.5 * _jnp.sum(_jnp.mean(err, axis=-1)) if err.ndim else 0.5 * err


def _adamw(w, g, m, v):
    m = ADAM_B1 * m + (1.0 - ADAM_B1) * g
    v = ADAM_B2 * v + (1.0 - ADAM_B2) * _jnp.square(g)
    m_hat = m / (1.0 - ADAM_B1 ** ADAM_STEP)
    v_hat = v / (1.0 - ADAM_B2 ** ADAM_STEP)
    delta = -ADAM_LR * (m_hat / (_jnp.sqrt(v_hat) + ADAM_EPS) + ADAM_WD * w)
    return delta, m, v


def reference(x, c, norm_g, w_mod, b_mod, sb_w_qkv, sb_w_o, s5_lam_re, s5_lam_im, s5_log_dt, s5_b_re, s5_b_im, s5_c_re, s5_c_im, s5_d, s5_w_glu, s5_b_glu, cv_w_pw1, cv_b_pw1, cv_w_dw, cv_b_dw, cv_ln_g, cv_ln_b, cv_w_pw2, cv_b_pw2, ffn_w_gate, ffn_w_up, ffn_w_down, loss_target, m_norm_g, m_w_mod, m_b_mod, m_sb_w_qkv, m_sb_w_o, m_s5_lam_re, m_s5_lam_im, m_s5_log_dt, m_s5_b_re, m_s5_b_im, m_s5_c_re, m_s5_c_im, m_s5_d, m_s5_w_glu, m_s5_b_glu, m_cv_w_pw1, m_cv_b_pw1, m_cv_w_dw, m_cv_b_dw, m_cv_ln_g, m_cv_ln_b, m_cv_w_pw2, m_cv_b_pw2, m_ffn_w_gate, m_ffn_w_up, m_ffn_w_down, v_norm_g, v_w_mod, v_b_mod, v_sb_w_qkv, v_sb_w_o, v_s5_lam_re, v_s5_lam_im, v_s5_log_dt, v_s5_b_re, v_s5_b_im, v_s5_c_re, v_s5_c_im, v_s5_d, v_s5_w_glu, v_s5_b_glu, v_cv_w_pw1, v_cv_b_pw1, v_cv_w_dw, v_cv_b_dw, v_cv_ln_g, v_cv_ln_b, v_cv_w_pw2, v_cv_b_pw2, v_ffn_w_gate, v_ffn_w_up, v_ffn_w_down):
    given = dict(x=x, c=c, norm_g=norm_g, w_mod=w_mod, b_mod=b_mod, sb_w_qkv=sb_w_qkv, sb_w_o=sb_w_o, s5_lam_re=s5_lam_re, s5_lam_im=s5_lam_im, s5_log_dt=s5_log_dt, s5_b_re=s5_b_re, s5_b_im=s5_b_im, s5_c_re=s5_c_re, s5_c_im=s5_c_im, s5_d=s5_d, s5_w_glu=s5_w_glu, s5_b_glu=s5_b_glu, cv_w_pw1=cv_w_pw1, cv_b_pw1=cv_b_pw1, cv_w_dw=cv_w_dw, cv_b_dw=cv_b_dw, cv_ln_g=cv_ln_g, cv_ln_b=cv_ln_b, cv_w_pw2=cv_w_pw2, cv_b_pw2=cv_b_pw2, ffn_w_gate=ffn_w_gate, ffn_w_up=ffn_w_up, ffn_w_down=ffn_w_down, loss_target=loss_target, m_norm_g=m_norm_g, m_w_mod=m_w_mod, m_b_mod=m_b_mod, m_sb_w_qkv=m_sb_w_qkv, m_sb_w_o=m_sb_w_o, m_s5_lam_re=m_s5_lam_re, m_s5_lam_im=m_s5_lam_im, m_s5_log_dt=m_s5_log_dt, m_s5_b_re=m_s5_b_re, m_s5_b_im=m_s5_b_im, m_s5_c_re=m_s5_c_re, m_s5_c_im=m_s5_c_im, m_s5_d=m_s5_d, m_s5_w_glu=m_s5_w_glu, m_s5_b_glu=m_s5_b_glu, m_cv_w_pw1=m_cv_w_pw1, m_cv_b_pw1=m_cv_b_pw1, m_cv_w_dw=m_cv_w_dw, m_cv_b_dw=m_cv_b_dw, m_cv_ln_g=m_cv_ln_g, m_cv_ln_b=m_cv_ln_b, m_cv_w_pw2=m_cv_w_pw2, m_cv_b_pw2=m_cv_b_pw2, m_ffn_w_gate=m_ffn_w_gate, m_ffn_w_up=m_ffn_w_up, m_ffn_w_down=m_ffn_w_down, v_norm_g=v_norm_g, v_w_mod=v_w_mod, v_b_mod=v_b_mod, v_sb_w_qkv=v_sb_w_qkv, v_sb_w_o=v_sb_w_o, v_s5_lam_re=v_s5_lam_re, v_s5_lam_im=v_s5_lam_im, v_s5_log_dt=v_s5_log_dt, v_s5_b_re=v_s5_b_re, v_s5_b_im=v_s5_b_im, v_s5_c_re=v_s5_c_re, v_s5_c_im=v_s5_c_im, v_s5_d=v_s5_d, v_s5_w_glu=v_s5_w_glu, v_s5_b_glu=v_s5_b_glu, v_cv_w_pw1=v_cv_w_pw1, v_cv_b_pw1=v_cv_b_pw1, v_cv_w_dw=v_cv_w_dw, v_cv_b_dw=v_cv_b_dw, v_cv_ln_g=v_cv_ln_g, v_cv_ln_b=v_cv_ln_b, v_cv_w_pw2=v_cv_w_pw2, v_cv_b_pw2=v_cv_b_pw2, v_ffn_w_gate=v_ffn_w_gate, v_ffn_w_up=v_ffn_w_up, v_ffn_w_down=v_ffn_w_down)
    weights = {n: given[n] for n in TWIN_WEIGHTS}
    shared = {n: given[n] for n in SHARED_INPUTS}
    per_example = {n: given[n] for n in ['x', 'c']}
    grad_fn = _jax.value_and_grad(_loss, argnums=(0, 1))

    def one_microbatch(ex, loss_target):
        ex = dict(ex)
        diff = ex.pop(TWIN_DIFF_INPUT)
        return grad_fn(weights, diff, {**shared, **ex}, loss_target)

    if N_MICROBATCH == 1:
        loss, (grad_w, grad_x) = one_microbatch(per_example, given["loss_target"])
    else:
        def body(carry, xs):
            loss_sum, grad_sum = carry
            l_k, (gw_k, gx_k) = one_microbatch(xs[0], xs[1])
            with _jax.named_scope("update"):
                return (loss_sum + l_k, _jax.tree.map(_jnp.add, grad_sum, gw_k)), gx_k

        init = (_jnp.zeros((), _jnp.float32), _jax.tree.map(_jnp.zeros_like, weights))
        (loss, grad_w), grad_x = _jax.lax.scan(body, init, (per_example, given["loss_target"]))
    with _jax.named_scope("update"):
        delta_w, new_m, new_v = {}, {}, {}
        for n in TWIN_WEIGHTS:
            delta_w[n], new_m[n], new_v[n] = _adamw(weights[n], grad_w[n], given["m_" + n], given["v_" + n])
    return (loss, grad_x, *[grad_w[n] for n in TWIN_WEIGHTS], *[delta_w[n] for n in TWIN_WEIGHTS],
            *[new_m[n] for n in TWIN_WEIGHTS], *[new_v[n] for n in TWIN_WEIGHTS])
```

```python
import functools
import math

import jax
import jax.numpy as jnp
from jax import lax
from jax.experimental import pallas as pl
from jax.experimental.pallas import tpu as pltpu

F32, BF16 = jnp.float32, jnp.bfloat16
N_DEV = 8
AXES = ("x", "y", "c")
EPS = 1e-6
HEAD_DIM = 64
LANES = 128
SUBLANES = 8
VMEM_LIMIT = 56 * 1024 * 1024
S5_GROUP = 16
ADAM_LR, ADAM_B1, ADAM_B2, ADAM_EPS, ADAM_WD, ADAM_STEP = 0.001, 0.9, 0.999, 1e-08, 0.01, 10
NEG_CUTOFF = -104.0


def _cp(sem):
    return pltpu.CompilerParams(dimension_semantics=sem, vmem_limit_bytes=VMEM_LIMIT)


def _tile(n, cap, mult=LANES):
    best = None
    for t in range(mult, min(n, cap) + 1, mult):
        if n % t == 0:
            best = t
    return best if best is not None else n


def _sigmoid(x):
    return 1.0 / (1.0 + jnp.exp(-x))


_DIMS = {"nn": (((1,), (0,)), ((), ())), "nt": (((1,), (1,)), ((), ())), "tn": (((0,), (0,)), ((), ()))}


def mm(pairs, mode, out_dtype=F32, bias=None, name="mm", caps=(1024, 512, 1024)):
    a0, b0 = pairs[0]
    if mode == "nn":
        (m, k), n = a0.shape, b0.shape[1]
    elif mode == "nt":
        (m, k), n = a0.shape, b0.shape[0]
    else:
        (k, m), n = a0.shape, b0.shape[1]
    tm, tn, tk = _tile(m, caps[0]), _tile(n, caps[1]), _tile(k, caps[2])
    nk = k // tk
    npairs = len(pairs)
    dims = _DIMS[mode]

    def body(*refs):
        ins = refs[:2 * npairs]
        bias_ref = refs[2 * npairs] if bias is not None else None
        o_ref, acc = refs[-2], refs[-1]
        kk = pl.program_id(2)

        @pl.when(kk == 0)
        def _():
            acc[...] = jnp.zeros_like(acc)

        part = None
        for p in range(npairs):
            d = lax.dot_general(ins[2 * p][...].astype(BF16), ins[2 * p + 1][...].astype(BF16), dims,
                                preferred_element_type=F32)
            part = d if part is None else part + d
        acc[...] += part

        @pl.when(kk == nk - 1)
        def _():
            r = acc[...]
            if bias_ref is not None:
                r = r + bias_ref[...]
            o_ref[...] = r.astype(o_ref.dtype)

    if mode == "nn":
        sa, sb = pl.BlockSpec((tm, tk), lambda i, j, kk: (i, kk)), pl.BlockSpec((tk, tn), lambda i, j, kk: (kk, j))
    elif mode == "nt":
        sa, sb = pl.BlockSpec((tm, tk), lambda i, j, kk: (i, kk)), pl.BlockSpec((tn, tk), lambda i, j, kk: (j, kk))
    else:
        sa, sb = pl.BlockSpec((tk, tm), lambda i, j, kk: (kk, i)), pl.BlockSpec((tk, tn), lambda i, j, kk: (kk, j))
    in_specs, args = [], []
    for a, b in pairs:
        in_specs += [sa, sb]
        args += [a, b]
    if bias is not None:
        in_specs.append(pl.BlockSpec((1, tn), lambda i, j, kk: (0, j)))
        args.append(bias.reshape(1, n).astype(F32))
    return pl.pallas_call(
        body, name=name, grid=(m // tm, n // tn, nk), in_specs=in_specs,
        out_specs=pl.BlockSpec((tm, tn), lambda i, j, kk: (i, j)),
        out_shape=jax.ShapeDtypeStruct((m, n), out_dtype),
        scratch_shapes=[pltpu.VMEM((tm, tn), F32)],
        compiler_params=_cp(("parallel", "parallel", "arbitrary")),
    )(*args)


def _act_fwd(kind, p1, p2):
    if kind == "swiglu":
        return p1 * _sigmoid(p1) * p2
    return p1 * _sigmoid(p2)


def _act_bwd(kind, d, p1, p2):
    if kind == "swiglu":
        s = _sigmoid(p1)
        return d * p2 * s * (1.0 + p1 * (1.0 - s)), d * (p1 * s)
    s = _sigmoid(p2)
    return d * s, d * p1 * s * (1.0 - s)


def mm_dual(a, w, bias, kind, pre_dtype, act_dtype, name):
    m, k = a.shape
    n = w.shape[1] // 2
    tm, tn = _tile(m, 512), _tile(n, 512)
    nb = n // tn

    def body(*refs):
        a_ref, w1_ref, w2_ref = refs[:3]
        p1_ref, p2_ref, act_ref = refs[-3:]
        av = a_ref[...].astype(BF16)
        p1 = jnp.dot(av, w1_ref[...].astype(BF16), preferred_element_type=F32)
        p2 = jnp.dot(av, w2_ref[...].astype(BF16), preferred_element_type=F32)
        if bias is not None:
            p1 = p1 + refs[3][...]
            p2 = p2 + refs[4][...]
        p1_ref[...] = p1.astype(p1_ref.dtype)
        p2_ref[...] = p2.astype(p2_ref.dtype)
        act_ref[...] = _act_fwd(kind, p1, p2).astype(act_ref.dtype)

    in_specs = [pl.BlockSpec((tm, k), lambda i, j: (i, 0)), pl.BlockSpec((k, tn), lambda i, j: (0, j)),
                pl.BlockSpec((k, tn), lambda i, j: (0, j + nb))]
    args = [a, w, w]
    if bias is not None:
        b2 = bias.reshape(1, 2 * n).astype(F32)
        in_specs += [pl.BlockSpec((1, tn), lambda i, j: (0, j)), pl.BlockSpec((1, tn), lambda i, j: (0, j + nb))]
        args += [b2, b2]
    ospec = pl.BlockSpec((tm, tn), lambda i, j: (i, j))
    return pl.pallas_call(
        body, name=name, grid=(m // tm, nb), in_specs=in_specs, out_specs=[ospec, ospec, ospec],
        out_shape=[jax.ShapeDtypeStruct((m, n), pre_dtype), jax.ShapeDtypeStruct((m, n), pre_dtype),
                   jax.ShapeDtypeStruct((m, n), act_dtype)],
        compiler_params=_cp(("parallel", "parallel")),
    )(*args)


def dual_bwd(dact, p1, p2, kind, name):
    m, n = dact.shape
    tm, tn = _tile(m, 512), _tile(n, 512)

    def body(d_ref, p1_ref, p2_ref, d1_ref, d2_ref, s_ref):
        d1, d2 = _act_bwd(kind, d_ref[...].astype(F32), p1_ref[...].astype(F32), p2_ref[...].astype(F32))
        d1_ref[...] = d1.astype(BF16)
        d2_ref[...] = d2.astype(BF16)

        @pl.when(pl.program_id(1) == 0)
        def _():
            s_ref[...] = jnp.zeros_like(s_ref)

        s_ref[0:1, :] += jnp.sum(d1, axis=0, keepdims=True)
        s_ref[1:2, :] += jnp.sum(d2, axis=0, keepdims=True)

    spec = pl.BlockSpec((tm, tn), lambda j, i: (i, j))
    return pl.pallas_call(
        body, name=name, grid=(n // tn, m // tm), in_specs=[spec, spec, spec],
        out_specs=[spec, spec, pl.BlockSpec((SUBLANES, tn), lambda j, i: (0, j))],
        out_shape=[jax.ShapeDtypeStruct((m, n), BF16), jax.ShapeDtypeStruct((m, n), BF16),
                   jax.ShapeDtypeStruct((SUBLANES, n), F32)],
        compiler_params=_cp(("parallel", "arbitrary")),
    )(dact, p1, p2)


def _rms(x):
    r = lax.rsqrt(jnp.mean(x * x, axis=-1, keepdims=True) + EPS)
    return x * r, r


def _vec8(*rows):
    d = rows[0].shape[-1]
    out = jnp.zeros((SUBLANES, d), F32)
    for i, r in enumerate(rows):
        out = out.at[i].set(r.reshape(d).astype(F32))
    return out


def norm_mod_fwd(h, g, scale, shift, out_dtypes, name):
    s, d = h.shape
    ts = _tile(s, 512, SUBLANES)
    vec = _vec8(g, 1.0 + scale, shift)

    def body(h_ref, v_ref, *outs):
        hh, _ = _rms(h_ref[...])
        u = hh * v_ref[0:1, :] * v_ref[1:2, :] + v_ref[2:3, :]
        for o in outs:
            o[...] = u.astype(o.dtype)

    spec = pl.BlockSpec((ts, d), lambda i: (i, 0))
    return pl.pallas_call(
        body, name=name, grid=(s // ts,), in_specs=[spec, pl.BlockSpec((SUBLANES, d), lambda i: (0, 0))],
        out_specs=[spec] * len(out_dtypes), out_shape=[jax.ShapeDtypeStruct((s, d), t) for t in out_dtypes],
        compiler_params=_cp(("parallel",)),
    )(h, vec)


def norm_mod_bwd(dus, h, dh_in, g, scale, name):
    s, d = h.shape
    ts = _tile(s, 256, SUBLANES)
    vec = _vec8(g, 1.0 + scale)
    nd = len(dus)

    def body(*refs):
        du = refs[0][...].astype(F32)
        for r in refs[1:nd]:
            du = du + r[...].astype(F32)
        h_ref, dhi_ref, v_ref, dh_ref, s_ref = refs[nd:]
        hh, r = _rms(h_ref[...])
        gg, sc1 = v_ref[0:1, :], v_ref[1:2, :]
        dn = du * sc1
        dhh = dn * gg
        dh = r * (dhh - hh * jnp.mean(dhh * hh, axis=-1, keepdims=True))
        dh_ref[...] = dhi_ref[...] + dh

        @pl.when(pl.program_id(0) == 0)
        def _():
            s_ref[...] = jnp.zeros_like(s_ref)

        s_ref[0:1, :] += jnp.sum(du, axis=0, keepdims=True)
        s_ref[1:2, :] += jnp.sum(du * (hh * gg), axis=0, keepdims=True)
        s_ref[2:3, :] += jnp.sum(dn * hh, axis=0, keepdims=True)

    spec = pl.BlockSpec((ts, d), lambda i: (i, 0))
    vspec = pl.BlockSpec((SUBLANES, d), lambda i: (0, 0))
    return pl.pallas_call(
        body, name=name, grid=(s // ts,), in_specs=[spec] * (nd + 2) + [vspec], out_specs=[spec, vspec],
        out_shape=[jax.ShapeDtypeStruct((s, d), F32), jax.ShapeDtypeStruct((SUBLANES, d), F32)],
        compiler_params=_cp(("arbitrary",)),
    )(*dus, h, dh_in, vec)


def resid_fwd(h, m, g, gate, name):
    s, d = h.shape
    ts = _tile(s, 512, SUBLANES)
    vec = _vec8(g, gate)

    def body(h_ref, m_ref, v_ref, o_ref):
        mh, _ = _rms(m_ref[...])
        o_ref[...] = h_ref[...] + v_ref[1:2, :] * (mh * v_ref[0:1, :])

    spec = pl.BlockSpec((ts, d), lambda i: (i, 0))
    return pl.pallas_call(
        body, name=name, grid=(s // ts,), in_specs=[spec, spec, pl.BlockSpec((SUBLANES, d), lambda i: (0, 0))],
        out_specs=spec, out_shape=jax.ShapeDtypeStruct((s, d), F32), compiler_params=_cp(("parallel",)),
    )(h, m, vec)


def resid_bwd(dh2, m, g, gate, out_dtype, name):
    s, d = m.shape
    ts = _tile(s, 256, SUBLANES)
    vec = _vec8(g, gate)

    def body(d_ref, m_ref, v_ref, dm_ref, s_ref):
        dh = d_ref[...]
        mh, r = _rms(m_ref[...])
        gg, gt = v_ref[0:1, :], v_ref[1:2, :]
        dmh = dh * (gt * gg)
        dm = r * (dmh - mh * jnp.mean(dmh * mh, axis=-1, keepdims=True))
        dm_ref[...] = dm.astype(dm_ref.dtype)

        @pl.when(pl.program_id(0) == 0)
        def _():
            s_ref[...] = jnp.zeros_like(s_ref)

        s_ref[0:1, :] += jnp.sum(dh * (mh * gg), axis=0, keepdims=True)
        s_ref[1:2, :] += jnp.sum(dh * gt * mh, axis=0, keepdims=True)
        s_ref[2:3, :] += jnp.sum(dm, axis=0, keepdims=True)

    spec = pl.BlockSpec((ts, d), lambda i: (i, 0))
    vspec = pl.BlockSpec((SUBLANES, d), lambda i: (0, 0))
    return pl.pallas_call(
        body, name=name, grid=(s // ts,), in_specs=[spec, spec, vspec], out_specs=[spec, vspec],
        out_shape=[jax.ShapeDtypeStruct((s, d), out_dtype), jax.ShapeDtypeStruct((SUBLANES, d), F32)],
        compiler_params=_cp(("arbitrary",)),
    )(dh2, m, vec)


def loss_and_grad(h, target, name):
    s, d = h.shape
    ts = _tile(s, 512, SUBLANES)

    def body(h_ref, t_ref, l_ref, dy_ref):
        e = h_ref[...] - t_ref[...]
        dy_ref[...] = e * (1.0 / d)

        @pl.when(pl.program_id(0) == 0)
        def _():
            l_ref[...] = jnp.zeros_like(l_ref)

        l_ref[...] += (0.5 / d) * jnp.sum(e * e)

    spec = pl.BlockSpec((ts, d), lambda i: (i, 0))
    lspec = pl.BlockSpec((SUBLANES, LANES), lambda i: (0, 0))
    return pl.pallas_call(
        body, name=name, grid=(s // ts,), in_specs=[spec, spec], out_specs=[lspec, spec],
        out_shape=[jax.ShapeDtypeStruct((SUBLANES, LANES), F32), jax.ShapeDtypeStruct((s, d), F32)],
        compiler_params=_cp(("arbitrary",)),
    )(h, target)


TK = 128
TQ_FWD = 256
TQ_BWD = 128


def _cum_matrices():
    a = jnp.arange(TK)
    ones = jnp.ones((TK, TK), F32)
    suffix = (a[:, None] > a[None, :]).astype(F32)
    prefix = (a[:, None] < a[None, :]).astype(F32)
    mk = lambda u: jnp.tile(jnp.concatenate([u, ones], axis=1), (2, 1)).astype(BF16)
    return mk(suffix), mk(prefix)


def _split_dot(x, cum):
    hi = x.astype(BF16)
    lo = (x - hi.astype(F32)).astype(BF16)
    return jnp.dot(jnp.concatenate([hi, lo], axis=1), cum, preferred_element_type=F32)


def _sb_scores(qh, kj, j, row, col0, scale):
    z = lax.dot_general(qh, kj, _DIMS["nt"], preferred_element_type=F32) * scale
    lb = jnp.minimum(z, 0.0) - jnp.log(1.0 + jnp.exp(-jnp.abs(z)))
    mask = (col0 + j * TK) < row
    lk = jnp.where(mask, lb - z, 0.0)
    return lb, lk, mask


def attn_fwd(qkv, name):
    s, d3 = qkv.shape
    d = d3 // 3
    npair = d // LANES
    tq = min(TQ_FWD, s)
    scale = HEAD_DIM ** -0.5
    cum_s, _ = _cum_matrices()

    def body(q_ref, k_ref, v_ref, c_ref, o_ref):
        i = pl.program_id(1)
        q = q_ref[...]
        cum = c_ref[...]
        lane = lax.broadcasted_iota(jnp.int32, (tq, LANES), 1)
        row = lax.broadcasted_iota(jnp.int32, (tq, TK), 0) + i * tq
        col0 = lax.broadcasted_iota(jnp.int32, (tq, TK), 1)
        j0 = (i * tq + tq) // TK - 1
        out = jnp.zeros((tq, LANES), F32)
        for hh in range(LANES // HEAD_DIM):
            in_head = (lane >= hh * HEAD_DIM) & (lane < (hh + 1) * HEAD_DIM)
            qh = jnp.where(in_head, q, jnp.zeros_like(q))

            def cond(c):
                j, r, _ = c
                return jnp.logical_and(j >= 0, jnp.max(r) > NEG_CUTOFF)

            def step(c):
                j, r, acc = c
                off = pl.multiple_of(j * TK, TK)
                kj = k_ref[pl.ds(off, TK), :]
                vj = v_ref[pl.ds(off, TK), :]
                lb, lk, mask = _sb_scores(qh, kj, j, row, col0, scale)
                t = _split_dot(lk, cum)
                w = jnp.where(mask, jnp.exp(lb + t[:, :TK] + r), 0.0)
                acc = acc + jnp.dot(w.astype(BF16), vj, preferred_element_type=F32)
                return j - 1, r + t[:, TK:], acc

            _, _, acc = lax.while_loop(cond, step, (j0, jnp.zeros((tq, TK), F32), jnp.zeros((tq, LANES), F32)))
            out = jnp.where(in_head, acc, out)
        o_ref[...] = out.astype(o_ref.dtype)

    return pl.pallas_call(
        body, name=name, grid=(npair, s // tq),
        in_specs=[pl.BlockSpec((tq, LANES), lambda p, i: (i, p)),
                  pl.BlockSpec((s, LANES), lambda p, i: (0, npair + p)),
                  pl.BlockSpec((s, LANES), lambda p, i: (0, 2 * npair + p)),
                  pl.BlockSpec((2 * TK, 2 * TK), lambda p, i: (0, 0))],
        out_specs=pl.BlockSpec((tq, LANES), lambda p, i: (i, p)),
        out_shape=jax.ShapeDtypeStruct((s, d), BF16),
        compiler_params=_cp(("parallel", "arbitrary")),
    )(qkv, qkv, qkv, cum_s)


def attn_bwd(qkv, do, name):
    s, d3 = qkv.shape
    d = d3 // 3
    npair = d // LANES
    tq = min(TQ_BWD, s)
    nq = s // tq
    scale = HEAD_DIM ** -0.5
    cum_s, cum_p = _cum_matrices()

    def body(q_ref, k_ref, v_ref, do_ref, cs_ref, cp_ref, dq_ref, dk_ref, dv_ref, dk_acc, dv_acc, e_scr, b_scr):
        i = pl.program_id(1)

        @pl.when(i == 0)
        def _():
            dk_acc[...] = jnp.zeros_like(dk_acc)
            dv_acc[...] = jnp.zeros_like(dv_acc)

        q = q_ref[...]
        do = do_ref[...]
        cum_suf = cs_ref[...]
        cum_pre = cp_ref[...]
        lane = lax.broadcasted_iota(jnp.int32, (tq, LANES), 1)
        lane_k = lax.broadcasted_iota(jnp.int32, (TK, LANES), 1)
        row = lax.broadcasted_iota(jnp.int32, (tq, TK), 0) + i * tq
        col0 = lax.broadcasted_iota(jnp.int32, (tq, TK), 1)
        j0 = (i * tq + tq) // TK - 1
        dq = jnp.zeros((tq, LANES), F32)
        for hh in range(LANES // HEAD_DIM):
            in_head = (lane >= hh * HEAD_DIM) & (lane < (hh + 1) * HEAD_DIM)
            qh = jnp.where(in_head, q, jnp.zeros_like(q))
            doh = jnp.where(in_head, do, jnp.zeros_like(do))

            def cond(c):
                j, r = c
                return jnp.logical_and(j >= 0, jnp.max(r) > NEG_CUTOFF)

            def sweep_left(c):
                j, r = c
                off = pl.multiple_of(j * TK, TK)
                kj = k_ref[pl.ds(off, TK), :]
                vj = v_ref[pl.ds(off, TK), :]
                lb, lk, mask = _sb_scores(qh, kj, j, row, col0, scale)
                t = _split_dot(lk, cum_suf)
                a = jnp.where(mask, jnp.exp(lb + t[:, :TK] + r), 0.0)
                da = lax.dot_general(doh, vj, _DIMS["nt"], preferred_element_type=F32)
                e_scr[j] = da * a
                b_scr[j] = jnp.exp(lb)
                dv_acc[pl.ds(off, TK), :] += lax.dot_general(a.astype(BF16), doh, _DIMS["tn"],
                                                             preferred_element_type=F32)
                return j - 1, r + t[:, TK:]

            jend, _ = lax.while_loop(cond, sweep_left, (j0, jnp.zeros((tq, TK), F32)))

            def sweep_right(j, c):
                pe, dq = c
                off = pl.multiple_of(j * TK, TK)
                kj = k_ref[pl.ds(off, TK), :]
                e = e_scr[j]
                beta = b_scr[j]
                t = _split_dot(e, cum_pre)
                mask = (col0 + j * TK) < row
                dz = jnp.where(mask, e * (1.0 - beta) - beta * (t[:, :TK] + pe), 0.0) * scale
                dzb = dz.astype(BF16)
                kjh = jnp.where((lane_k >= hh * HEAD_DIM) & (lane_k < (hh + 1) * HEAD_DIM), kj, jnp.zeros_like(kj))
                dq = dq + jnp.dot(dzb, kjh, preferred_element_type=F32)
                dk_acc[pl.ds(off, TK), :] += lax.dot_general(dzb, qh, _DIMS["tn"], preferred_element_type=F32)
                return pe + t[:, TK:], dq

            _, dq = lax.fori_loop(jend + 1, j0 + 1, sweep_right, (jnp.zeros((tq, TK), F32), dq))
        dq_ref[...] = dq.astype(dq_ref.dtype)

        @pl.when(i == nq - 1)
        def _():
            dk_ref[...] = dk_acc[...].astype(dk_ref.dtype)
            dv_ref[...] = dv_acc[...].astype(dv_ref.dtype)

    qspec = pl.BlockSpec((tq, LANES), lambda p, i: (i, p))
    full = lambda base: pl.BlockSpec((s, LANES), lambda p, i: (0, base + p))
    cspec = pl.BlockSpec((2 * TK, 2 * TK), lambda p, i: (0, 0))
    sd = jax.ShapeDtypeStruct((s, d), BF16)
    return pl.pallas_call(
        body, name=name, grid=(npair, nq),
        in_specs=[qspec, full(npair), full(2 * npair), qspec, cspec, cspec],
        out_specs=[qspec, full(0), full(0)], out_shape=[sd, sd, sd],
        scratch_shapes=[pltpu.VMEM((s, LANES), F32), pltpu.VMEM((s, LANES), F32),
                        pltpu.VMEM((s // TK, tq, TK), F32), pltpu.VMEM((s // TK, tq, TK), F32)],
        compiler_params=_cp(("parallel", "arbitrary")),
    )(qkv, qkv, qkv, do, cum_s, cum_p)


HALO = 32
CONV_ROWS = 128


def _ln_swish(hc, g, b):
    mu = jnp.mean(hc, axis=-1, keepdims=True)
    xc = hc - mu
    rstd = lax.rsqrt(jnp.mean(xc * xc, axis=-1, keepdims=True) + EPS)
    xh = xc * rstd
    hn = xh * g + b
    return xh, rstd, hn


def conv_mid_fwd(x, w_dw, b_dw, ln_g, ln_b, name):
    s, d = x.shape
    width = w_dw.shape[0]
    ts = _tile(s, 256, CONV_ROWS)
    base = HALO - (width - 1)
    wpad = jnp.zeros((HALO, d), F32).at[:width].set(w_dw.astype(F32))
    vec = _vec8(b_dw, ln_g, ln_b)

    def body(x_ref, w_ref, v_ref, hc_ref, hs_ref, win):
        i = pl.program_id(0)

        @pl.when(i == 0)
        def _():
            win[0:HALO, :] = jnp.zeros((HALO, d), F32)

        @pl.when(i > 0)
        def _():
            win[0:HALO, :] = win[ts:ts + HALO, :]

        win[HALO:HALO + ts, :] = x_ref[...]
        for rc in range(ts // CONV_ROWS):
            for lc in range(d // LANES):
                cols = slice(lc * LANES, (lc + 1) * LANES)
                acc = jnp.broadcast_to(v_ref[0:1, cols], (CONV_ROWS, LANES))
                for k in range(width):
                    acc = acc + w_ref[k:k + 1, cols] * win[pl.ds(rc * CONV_ROWS + base + k, CONV_ROWS), cols]
                hc_ref[rc * CONV_ROWS:(rc + 1) * CONV_ROWS, cols] = acc
        _, _, hn = _ln_swish(hc_ref[...], v_ref[1:2, :], v_ref[2:3, :])
        hs_ref[...] = (hn * _sigmoid(hn)).astype(hs_ref.dtype)

    spec = pl.BlockSpec((ts, d), lambda i: (i, 0))
    return pl.pallas_call(
        body, name=name, grid=(s // ts,),
        in_specs=[spec, pl.BlockSpec((HALO, d), lambda i: (0, 0)), pl.BlockSpec((SUBLANES, d), lambda i: (0, 0))],
        out_specs=[spec, spec], out_shape=[jax.ShapeDtypeStruct((s, d), F32), jax.ShapeDtypeStruct((s, d), BF16)],
        scratch_shapes=[pltpu.VMEM((ts + HALO, d), F32)],
        compiler_params=_cp(("arbitrary",)),
    )(x, wpad, vec)


def conv_mid_bwd(dhs, hc, x, w_dw, ln_g, ln_b, name):
    s, d = x.shape
    width = w_dw.shape[0]
    ts = _tile(s, 256, CONV_ROWS)
    nt = s // ts
    base = HALO - (width - 1)
    wpad = jnp.zeros((HALO, d), F32).at[:width].set(w_dw.astype(F32))
    vec = _vec8(ln_g, ln_b)

    def body(dhs_ref, hc_ref, x_ref, xh_ref, w_ref, v_ref, dx_ref, dw_ref, s_ref, dwin, xwin):
        i = pl.program_id(0)

        @pl.when(i == 0)
        def _():
            dwin[ts:ts + HALO, :] = jnp.zeros((HALO, d), F32)
            dw_ref[...] = jnp.zeros_like(dw_ref)
            s_ref[...] = jnp.zeros_like(s_ref)

        @pl.when(i > 0)
        def _():
            dwin[ts:ts + HALO, :] = dwin[0:HALO, :]

        @pl.when(i == nt - 1)
        def _():
            xwin[0:HALO, :] = jnp.zeros((HALO, d), F32)

        @pl.when(i < nt - 1)
        def _():
            xwin[0:HALO, :] = xh_ref[...]

        xwin[HALO:HALO + ts, :] = x_ref[...]
        g = v_ref[0:1, :]
        xh, rstd, hn = _ln_swish(hc_ref[...], g, v_ref[1:2, :])
        sig = _sigmoid(hn)
        dhn = dhs_ref[...].astype(F32) * (sig * (1.0 + hn * (1.0 - sig)))
        dxh = dhn * g
        dhc = rstd * (dxh - jnp.mean(dxh, axis=-1, keepdims=True) - xh * jnp.mean(dxh * xh, axis=-1, keepdims=True))
        dwin[0:ts, :] = dhc
        s_ref[0:1, :] += jnp.sum(dhc, axis=0, keepdims=True)
        s_ref[1:2, :] += jnp.sum(dhn * xh, axis=0, keepdims=True)
        s_ref[2:3, :] += jnp.sum(dhn, axis=0, keepdims=True)
        for rc in range(ts // CONV_ROWS):
            for lc in range(d // LANES):
                cols = slice(lc * LANES, (lc + 1) * LANES)
                r0 = rc * CONV_ROWS
                dch = dwin[r0:r0 + CONV_ROWS, cols]
                acc = jnp.zeros((CONV_ROWS, LANES), F32)
                for k in range(width):
                    acc = acc + w_ref[k:k + 1, cols] * dwin[pl.ds(r0 + (width - 1) - k, CONV_ROWS), cols]
                    dw_ref[k:k + 1, cols] += jnp.sum(dch * xwin[pl.ds(r0 + base + k, CONV_ROWS), cols], axis=0,
                                                     keepdims=True)
                dx_ref[r0:r0 + CONV_ROWS, cols] = acc

    rev = pl.BlockSpec((ts, d), lambda i: (nt - 1 - i, 0))
    halo = pl.BlockSpec((HALO, d), lambda i: (jnp.maximum((nt - 1 - i) * (ts // HALO) - 1, 0), 0))
    vspec = pl.BlockSpec((SUBLANES, d), lambda i: (0, 0))
    wspec = pl.BlockSpec((HALO, d), lambda i: (0, 0))
    return pl.pallas_call(
        body, name=name, grid=(nt,), in_specs=[rev, rev, rev, halo, wspec, vspec],
        out_specs=[rev, wspec, vspec],
        out_shape=[jax.ShapeDtypeStruct((s, d), F32), jax.ShapeDtypeStruct((HALO, d), F32),
                   jax.ShapeDtypeStruct((SUBLANES, d), F32)],
        scratch_shapes=[pltpu.VMEM((ts + HALO, d), F32), pltpu.VMEM((ts + HALO, d), F32)],
        compiler_params=_cp(("arbitrary",)),
    )(dhs, hc, x, x, wpad, vec)


S5_KB = 256
SCAN_LANES = 256
GELU_C = math.sqrt(2.0 / math.pi)
GELU_A = 0.044715


def _gelu(x):
    return 0.5 * x * (1.0 + jnp.tanh(GELU_C * (x + GELU_A * x * x * x)))


def _gelu_grad(x):
    th = jnp.tanh(GELU_C * (x + GELU_A * x * x * x))
    return 0.5 * (1.0 + th) + 0.5 * x * (1.0 - th * th) * GELU_C * (1.0 + 3.0 * GELU_A * x * x)


def _s5_discretize(lam_re, lam_im, log_dt, b_re, b_im):
    dt = jnp.exp(log_dt)[:, None]
    mag = jnp.exp(lam_re * dt)
    ar, ai = mag * jnp.cos(lam_im * dt), mag * jnp.sin(lam_im * dt)
    den = lam_re * lam_re + lam_im * lam_im
    er = ((ar - 1) * lam_re + ai * lam_im) / den
    ei = (ai * lam_re - (ar - 1) * lam_im) / den
    bbr = er[..., None] * b_re - ei[..., None] * b_im
    bbi = er[..., None] * b_im + ei[..., None] * b_re
    return ar, ai, bbr, bbi


def _blockdiag(w, nkb):
    g, r, c = w.shape
    gpb = g // nkb
    eye = jnp.eye(gpb, dtype=w.dtype)
    return jnp.einsum("kgrc,gh->kgrhc", w.reshape(nkb, gpb, r, c), eye).reshape(nkb, gpb * r, gpb * c)


def _blockdiag_extract(m, g):
    nkb = m.shape[0]
    gpb = g // nkb
    r, c = m.shape[1] // gpb, m.shape[2] // gpb
    eye = jnp.eye(gpb, dtype=m.dtype)
    return jnp.einsum("kgrhc,gh->kgrc", m.reshape(nkb, gpb, r, gpb, c), eye).reshape(g, r, c)


def _scan_powers(ar, ai, reverse):
    ar = ar.reshape(-1)
    ai = (-ai if reverse else ai).reshape(-1)
    cmul = lambda x, y: (x[0] * y[0] - x[1] * y[1], x[0] * y[1] + x[1] * y[0])
    a1 = (ar, ai)
    a2 = cmul(a1, a1)
    a4 = cmul(a2, a2)
    r = jnp.arange(SUBLANES)[:, None]
    rows = []
    for sft, p in ((1, a1), (2, a2), (4, a4)):
        keep = (r + sft <= SUBLANES - 1) if reverse else (r >= sft)
        rows += [jnp.where(keep, p[0][None, :], 0.0), jnp.where(keep, p[1][None, :], 0.0)]
    pows = [a1]
    for _ in range(SUBLANES - 1):
        pows.append(cmul(pows[-1], a1))
    if reverse:
        pows = pows[::-1]
    rows += [jnp.stack([p[0] for p in pows]), jnp.stack([p[1] for p in pows])]
    return jnp.concatenate(rows, axis=0).astype(F32)


def _scan_tile(sr, si, pw_ref, car, nrg, reverse):
    nsb = sr.shape[1]
    ch = min(SCAN_LANES, nsb)
    nch = nsb // ch
    row = 0 if reverse else SUBLANES - 1

    def step(t, carry):
        rg = (nrg - 1 - t) if reverse else t
        off = pl.multiple_of(rg * SUBLANES, SUBLANES)
        out = []
        for c in range(nch):
            cols = slice(c * ch, (c + 1) * ch)
            cr, ci = carry[2 * c], carry[2 * c + 1]
            br = sr[pl.ds(off, SUBLANES), cols]
            bi = si[pl.ds(off, SUBLANES), cols]
            for idx, sft in enumerate((1, 2, 4)):
                sh = SUBLANES - sft if reverse else sft
                tr = pltpu.roll(br, sh, axis=0)
                ti = pltpu.roll(bi, sh, axis=0)
                mr = pw_ref[16 * idx:16 * idx + 8, cols]
                mi = pw_ref[16 * idx + 8:16 * idx + 16, cols]
                br, bi = br + mr * tr - mi * ti, bi + mr * ti + mi * tr
            apr, api = pw_ref[48:56, cols], pw_ref[56:64, cols]
            xr = br + apr * cr - api * ci
            xi = bi + apr * ci + api * cr
            sr[pl.ds(off, SUBLANES), cols] = xr
            si[pl.ds(off, SUBLANES), cols] = xi
            out += [jnp.broadcast_to(xr[row:row + 1, :], xr.shape), jnp.broadcast_to(xi[row:row + 1, :], xi.shape)]
        return tuple(out)

    init = []
    for c in range(nch):
        cols = slice(c * ch, (c + 1) * ch)
        init += [car[0:SUBLANES, cols], car[SUBLANES:2 * SUBLANES, cols]]
    fin = lax.fori_loop(0, nrg, step, tuple(init))
    for c in range(nch):
        cols = slice(c * ch, (c + 1) * ch)
        car[0:SUBLANES, cols] = fin[2 * c]
        car[SUBLANES:2 * SUBLANES, cols] = fin[2 * c + 1]


def s5_fwd(u, wb_r, wb_i, wc_r, wc_i, pw, d_skip, name):
    s, d = u.shape
    nkb, kb, nsb = wb_r.shape
    ts = _tile(s, 256, SUBLANES)
    dvec = _vec8(d_skip)

    def body(u_ref, wbr, wbi, wcr, wci, pw_ref, dv_ref, xr_ref, xi_ref, yy_ref, g_ref, sr, si, car):
        @pl.when(pl.program_id(1) == 0)
        def _():
            car[...] = jnp.zeros_like(car)

        uu = u_ref[...]
        ub = uu.astype(BF16)
        sr[...] = jnp.dot(ub, wbr[...], preferred_element_type=F32)
        si[...] = jnp.dot(ub, wbi[...], preferred_element_type=F32)
        _scan_tile(sr, si, pw_ref, car, ts // SUBLANES, False)
        xr, xi = sr[...], si[...]
        xr_ref[...] = xr
        xi_ref[...] = xi
        y = (jnp.dot(xr.astype(BF16), wcr[...], preferred_element_type=F32)
             + jnp.dot(xi.astype(BF16), wci[...], preferred_element_type=F32) + dv_ref[0:1, :] * uu)
        yy_ref[...] = y
        g_ref[...] = _gelu(y).astype(g_ref.dtype)

    cspec = pl.BlockSpec((ts, kb), lambda k, i: (i, k))
    sspec = pl.BlockSpec((ts, nsb), lambda k, i: (i, k))
    wbspec = pl.BlockSpec((None, kb, nsb), lambda k, i: (k, 0, 0))
    wcspec = pl.BlockSpec((None, nsb, kb), lambda k, i: (k, 0, 0))
    ns = nkb * nsb
    return pl.pallas_call(
        body, name=name, grid=(nkb, s // ts),
        in_specs=[cspec, wbspec, wbspec, wcspec, wcspec, pl.BlockSpec((64, nsb), lambda k, i: (0, k)),
                  pl.BlockSpec((SUBLANES, kb), lambda k, i: (0, k))],
        out_specs=[sspec, sspec, cspec, cspec],
        out_shape=[jax.ShapeDtypeStruct((s, ns), F32), jax.ShapeDtypeStruct((s, ns), F32),
                   jax.ShapeDtypeStruct((s, d), F32), jax.ShapeDtypeStruct((s, d), BF16)],
        scratch_shapes=[pltpu.VMEM((ts, nsb), F32), pltpu.VMEM((ts, nsb), F32), pltpu.VMEM((2 * SUBLANES, nsb), F32)],
        compiler_params=_cp(("parallel", "arbitrary")),
    )(u, wb_r, wb_i, wc_r, wc_i, pw, dvec)


def s5_bwd(dg, yy, u, xr, xi, wb_r, wb_i, wc_r, wc_i, pwb, d_skip, name):
    s, d = u.shape
    nkb, kb, nsb = wb_r.shape
    ns = nkb * nsb
    ts = _tile(s, 256, SUBLANES)
    nt = s // ts
    dvec = _vec8(d_skip)

    def body(dg_ref, yy_ref, u_ref, xr_ref, xi_ref, xrp_ref, xip_ref, wbr, wbi, wcr, wci, pw_ref, dv_ref,
             du_ref, dwbr, dwbi, dwcr, dwci, da_ref, dd_ref, sr, si, car):
        i = pl.program_id(1)

        @pl.when(i == 0)
        def _():
            car[...] = jnp.zeros_like(car)
            for r in (dwbr, dwbi, dwcr, dwci, da_ref, dd_ref):
                r[...] = jnp.zeros_like(r)

        uu = u_ref[...]
        dyy = dg_ref[...] * _gelu_grad(yy_ref[...])
        dd_ref[0:1, :] += jnp.sum(dyy * uu, axis=0, keepdims=True)
        dyb = dyy.astype(BF16)
        sr[...] = lax.dot_general(dyb, wcr[...], _DIMS["nt"], preferred_element_type=F32)
        si[...] = lax.dot_general(dyb, wci[...], _DIMS["nt"], preferred_element_type=F32)
        _scan_tile(sr, si, pw_ref, car, ts // SUBLANES, True)
        gr, gi = sr[...], si[...]
        grb, gib = gr.astype(BF16), gi.astype(BF16)
        xrt, xit = xr_ref[...], xi_ref[...]
        dwcr[...] += lax.dot_general(xrt.astype(BF16), dyb, _DIMS["tn"], preferred_element_type=F32)
        dwci[...] += lax.dot_general(xit.astype(BF16), dyb, _DIMS["tn"], preferred_element_type=F32)
        ub = uu.astype(BF16)
        dwbr[...] += lax.dot_general(ub, grb, _DIMS["tn"], preferred_element_type=F32)
        dwbi[...] += lax.dot_general(ub, gib, _DIMS["tn"], preferred_element_type=F32)
        du_ref[...] = (lax.dot_general(grb, wbr[...], _DIMS["nt"], preferred_element_type=F32)
                       + lax.dot_general(gib, wbi[...], _DIMS["nt"], preferred_element_type=F32)
                       + dyy * dv_ref[0:1, :])
        has_prev = (i < nt - 1).astype(F32)
        rowid = lax.broadcasted_iota(jnp.int32, (ts, nsb), 0)
        pr = jnp.broadcast_to(xrp_ref[SUBLANES - 1:SUBLANES, :] * has_prev, (ts, nsb))
        pi = jnp.broadcast_to(xip_ref[SUBLANES - 1:SUBLANES, :] * has_prev, (ts, nsb))
        xpr = jnp.where(rowid == 0, pr, pltpu.roll(xrt, 1, axis=0))
        xpi = jnp.where(rowid == 0, pi, pltpu.roll(xit, 1, axis=0))
        da_ref[0:1, :] += jnp.sum(gr * xpr + gi * xpi, axis=0, keepdims=True)
        da_ref[1:2, :] += jnp.sum(gi * xpr - gr * xpi, axis=0, keepdims=True)

    cspec = pl.BlockSpec((ts, kb), lambda k, i: (nt - 1 - i, k))
    sspec = pl.BlockSpec((ts, nsb), lambda k, i: (nt - 1 - i, k))
    pspec = pl.BlockSpec((SUBLANES, nsb), lambda k, i: (jnp.maximum((nt - 1 - i) * (ts // SUBLANES) - 1, 0), k))
    wbspec = pl.BlockSpec((None, kb, nsb), lambda k, i: (k, 0, 0))
    wcspec = pl.BlockSpec((None, nsb, kb), lambda k, i: (k, 0, 0))
    v8s = pl.BlockSpec((SUBLANES, nsb), lambda k, i: (0, k))
    v8c = pl.BlockSpec((SUBLANES, kb), lambda k, i: (0, k))
    return pl.pallas_call(
        body, name=name, grid=(nkb, nt),
        in_specs=[cspec, cspec, cspec, sspec, sspec, pspec, pspec, wbspec, wbspec, wcspec, wcspec,
                  pl.BlockSpec((64, nsb), lambda k, i: (0, k)), v8c],
        out_specs=[cspec, wbspec, wbspec, wcspec, wcspec, v8s, v8c],
        out_shape=[jax.ShapeDtypeStruct((s, d), F32),
                   jax.ShapeDtypeStruct((nkb, kb, nsb), F32), jax.ShapeDtypeStruct((nkb, kb, nsb), F32),
                   jax.ShapeDtypeStruct((nkb, nsb, kb), F32), jax.ShapeDtypeStruct((nkb, nsb, kb), F32),
                   jax.ShapeDtypeStruct((SUBLANES, ns), F32), jax.ShapeDtypeStruct((SUBLANES, d), F32)],
        scratch_shapes=[pltpu.VMEM((ts, nsb), F32), pltpu.VMEM((ts, nsb), F32), pltpu.VMEM((2 * SUBLANES, nsb), F32)],
        compiler_params=_cp(("parallel", "arbitrary")),
    )(dg, yy, u, xr, xi, xr, xi, wb_r, wb_i, wc_r, wc_i, pwb, dvec)


def s5_operands(lam_re, lam_im, log_dt, b_re, b_im, c_re, c_im, d):
    ar, ai, bbr, bbi = _s5_discretize(lam_re, lam_im, log_dt, b_re, b_im)
    nkb = max(d // S5_KB, 1)
    wb_r = _blockdiag(bbr.transpose(0, 2, 1), nkb).astype(BF16)
    wb_i = _blockdiag(bbi.transpose(0, 2, 1), nkb).astype(BF16)
    wc_r = _blockdiag(c_re.transpose(0, 2, 1), nkb).astype(BF16)
    wc_i = _blockdiag(-c_im.transpose(0, 2, 1), nkb).astype(BF16)
    return wb_r, wb_i, wc_r, wc_i, _scan_powers(ar, ai, False), _scan_powers(ar, ai, True)


MESH = pl.DeviceIdType.MESH
HBM_SPEC = pl.BlockSpec(memory_space=pltpu.HBM)


def _me():
    return 4 * lax.axis_index("x") + 2 * lax.axis_index("y") + lax.axis_index("c")


def all_gather(shard, name):
    m, n = shard.shape

    def body(x_ref, out_ref, send_sems, recv_sems, local_sem):
        x, y, c = lax.axis_index("x"), lax.axis_index("y"), lax.axis_index("c")
        me, sibling = (x, y, c), (x, y, 1 - c)
        chips = [(1 - x, y), (x, 1 - y), (1 - x, 1 - y)]

        def slot(px, py, pc):
            return out_ref.at[4 * px + 2 * py + pc]

        def copy(k, block, to, src=None):
            return pltpu.make_async_remote_copy(
                src_ref=slot(*block) if src is None else src, dst_ref=slot(*block),
                send_sem=send_sems.at[k], recv_sem=recv_sems.at[k], device_id=to, device_id_type=MESH)

        mine = pltpu.make_async_copy(x_ref, slot(*me), local_sem)
        mine.start()
        first = [copy(0, me, sibling, src=x_ref)]
        first += [copy(1 + j, me, (*chip, c), src=x_ref) for j, chip in enumerate(chips)]
        for cp in first:
            cp.start()
        passed = [copy(4 + j, (*chip, c), sibling) for j, chip in enumerate(chips)]
        for j, chip in enumerate(chips):
            copy(1 + j, (*chip, c), me).wait_recv()
            passed[j].start()
        copy(0, sibling, me).wait_recv()
        for j, chip in enumerate(chips):
            copy(4 + j, (*chip, 1 - c), me).wait_recv()
        for cp in first + passed:
            cp.wait_send()
        mine.wait()

    return pl.pallas_call(
        body, name=name, out_shape=jax.ShapeDtypeStruct((N_DEV, m, n), shard.dtype),
        in_specs=[HBM_SPEC], out_specs=HBM_SPEC,
        scratch_shapes=[pltpu.SemaphoreType.DMA((7,)), pltpu.SemaphoreType.DMA((7,)), pltpu.SemaphoreType.DMA],
    )(shard)


def exchange(slabs, name):
    _, m, n = slabs.shape

    def body(x_ref, out_ref, send_sems, recv_sems, local_sem):
        x, y, c = lax.axis_index("x"), lax.axis_index("y"), lax.axis_index("c")
        me = 4 * x + 2 * y + c
        mine = pltpu.make_async_copy(x_ref.at[me], out_ref.at[me], local_sem)
        mine.start()
        copies = []
        for k in range(1, N_DEV):
            px = 1 - x if k & 4 else x
            py = 1 - y if k & 2 else y
            pc = 1 - c if k & 1 else c
            peer = 4 * px + 2 * py + pc
            cp = pltpu.make_async_remote_copy(
                src_ref=x_ref.at[peer], dst_ref=out_ref.at[me], send_sem=send_sems.at[k - 1],
                recv_sem=recv_sems.at[k - 1], device_id=(px, py, pc), device_id_type=MESH)
            cp.start()
            copies.append(cp)
        for cp in copies:
            cp.wait_recv()
        for cp in copies:
            cp.wait_send()
        mine.wait()

    return pl.pallas_call(
        body, name=name, out_shape=jax.ShapeDtypeStruct((N_DEV, m, n), slabs.dtype),
        in_specs=[HBM_SPEC], out_specs=HBM_SPEC,
        scratch_shapes=[pltpu.SemaphoreType.DMA((7,)), pltpu.SemaphoreType.DMA((7,)), pltpu.SemaphoreType.DMA],
    )(slabs)


def sum_slots(parts, name):
    _, m, n = parts.shape
    tm = _tile(m, 256, SUBLANES)

    def body(p_ref, o_ref):
        acc = p_ref[0].astype(F32)
        for q in range(1, N_DEV):
            acc = acc + p_ref[q].astype(F32)
        o_ref[...] = acc

    return pl.pallas_call(
        body, name=name, grid=(m // tm,), in_specs=[pl.BlockSpec((N_DEV, tm, n), lambda i: (0, i, 0))],
        out_specs=pl.BlockSpec((tm, n), lambda i: (i, 0)), out_shape=jax.ShapeDtypeStruct((m, n), F32),
        compiler_params=_cp(("parallel",)),
    )(parts)


PACK_COLS = 1024
PACK_ROWS = 16


def _pack_flat(pieces, dtype):
    lead = pieces[0].shape[:-1]
    flat = jnp.concatenate([p.astype(dtype) for p in pieces], axis=-1)
    unit = PACK_COLS * PACK_ROWS
    total = -(-flat.shape[-1] // unit) * unit
    flat = jnp.pad(flat, [(0, 0)] * len(lead) + [(0, total - flat.shape[-1])])
    return flat.reshape(*lead, total // PACK_COLS, PACK_COLS)


def _unpack_flat(packed, sizes):
    lead = packed.shape[:-2]
    flat = packed.reshape(*lead, -1)
    out, off = [], 0
    for n in sizes:
        out.append(flat[..., off:off + n])
        off += n
    return out


def mod_fwd(c_all, w_mod, b_cols, name):
    nl, d, n = w_mod.shape

    def body(c_ref, w_ref, b_ref, o_ref):
        cv = c_ref[...]
        sc = (cv * _sigmoid(cv)).astype(BF16)
        o_ref[...] = jnp.dot(sc, w_ref[...].astype(BF16), preferred_element_type=F32) + b_ref[...]

    return pl.pallas_call(
        body, name=name, grid=(nl,),
        in_specs=[pl.BlockSpec((N_DEV, d), lambda l: (0, 0)), pl.BlockSpec((None, d, n), lambda l: (l, 0, 0)),
                  pl.BlockSpec((None, 1, n), lambda l: (l, 0, 0))],
        out_specs=pl.BlockSpec((None, N_DEV, n), lambda l: (l, 0, 0)),
        out_shape=jax.ShapeDtypeStruct((nl, N_DEV, n), F32), compiler_params=_cp(("parallel",)),
    )(c_all, w_mod, b_cols.reshape(nl, 1, n))


def mod_bwd(c_all, dmod_cols, name):
    nl, _, n = dmod_cols.shape
    d = c_all.shape[1]

    def body(c_ref, g_ref, o_ref):
        cv = c_ref[...]
        sc = (cv * _sigmoid(cv)).astype(BF16)
        o_ref[...] = lax.dot_general(sc, g_ref[...].astype(BF16), _DIMS["tn"], preferred_element_type=F32)

    return pl.pallas_call(
        body, name=name, grid=(nl,),
        in_specs=[pl.BlockSpec((N_DEV, d), lambda l: (0, 0)), pl.BlockSpec((None, N_DEV, n), lambda l: (l, 0, 0))],
        out_specs=pl.BlockSpec((None, d, n), lambda l: (l, 0, 0)),
        out_shape=jax.ShapeDtypeStruct((nl, d, n), F32), compiler_params=_cp(("parallel",)),
    )(c_all, dmod_cols)


def adamw(w, g, m, v, name):
    r, c = w.shape
    tr = _tile(r, 512, SUBLANES)
    c1 = 1.0 - ADAM_B1 ** ADAM_STEP
    c2 = 1.0 - ADAM_B2 ** ADAM_STEP

    def body(w_ref, g_ref, m_ref, v_ref, d_ref, nm_ref, nv_ref):
        gg = g_ref[...]
        nm = ADAM_B1 * m_ref[...] + (1.0 - ADAM_B1) * gg
        nv = ADAM_B2 * v_ref[...] + (1.0 - ADAM_B2) * (gg * gg)
        nm_ref[...] = nm
        nv_ref[...] = nv
        d_ref[...] = -ADAM_LR * ((nm / c1) / (jnp.sqrt(nv / c2) + ADAM_EPS) + ADAM_WD * w_ref[...])

    spec = pl.BlockSpec((tr, c), lambda i: (i, 0))
    sd = jax.ShapeDtypeStruct((r, c), F32)
    return pl.pallas_call(
        body, name=name, grid=(r // tr,), in_specs=[spec] * 4, out_specs=[spec] * 3, out_shape=[sd, sd, sd],
        compiler_params=_cp(("parallel",)),
    )(w, g, m, v)


WEIGHTS = ["norm_g", "w_mod", "b_mod", "sb_w_qkv", "sb_w_o", "s5_lam_re", "s5_lam_im", "s5_log_dt", "s5_b_re",
           "s5_b_im", "s5_c_re", "s5_c_im", "s5_d", "s5_w_glu", "s5_b_glu", "cv_w_pw1", "cv_b_pw1", "cv_w_dw",
           "cv_b_dw", "cv_ln_g", "cv_ln_b", "cv_w_pw2", "cv_b_pw2", "ffn_w_gate", "ffn_w_up", "ffn_w_down"]
BIG = ["w_mod", "sb_w_qkv", "sb_w_o", "s5_w_glu", "cv_w_pw1", "cv_w_pw2", "ffn_w_gate", "ffn_w_up", "ffn_w_down"]
SMALL_SHARDED = ["norm_g", "cv_b_pw1", "cv_w_dw", "cv_b_dw", "cv_ln_g", "cv_ln_b", "cv_b_pw2"]
SMALL = [n for n in WEIGHTS if n not in BIG]


def _unshard_last(part, local_shape):
    a = jnp.moveaxis(part.reshape((N_DEV,) + tuple(local_shape)), 0, -2)
    return a.reshape(tuple(local_shape[:-1]) + (N_DEV * local_shape[-1],))


def kernel(x, c, norm_g, w_mod, b_mod, sb_w_qkv, sb_w_o, s5_lam_re, s5_lam_im, s5_log_dt, s5_b_re, s5_b_im, s5_c_re, s5_c_im, s5_d, s5_w_glu, s5_b_glu, cv_w_pw1, cv_b_pw1, cv_w_dw, cv_b_dw, cv_ln_g, cv_ln_b, cv_w_pw2, cv_b_pw2, ffn_w_gate, ffn_w_up, ffn_w_down, loss_target, m_norm_g, m_w_mod, m_b_mod, m_sb_w_qkv, m_sb_w_o, m_s5_lam_re, m_s5_lam_im, m_s5_log_dt, m_s5_b_re, m_s5_b_im, m_s5_c_re, m_s5_c_im, m_s5_d, m_s5_w_glu, m_s5_b_glu, m_cv_w_pw1, m_cv_b_pw1, m_cv_w_dw, m_cv_b_dw, m_cv_ln_g, m_cv_ln_b, m_cv_w_pw2, m_cv_b_pw2, m_ffn_w_gate, m_ffn_w_up, m_ffn_w_down, v_norm_g, v_w_mod, v_b_mod, v_sb_w_qkv, v_sb_w_o, v_s5_lam_re, v_s5_lam_im, v_s5_log_dt, v_s5_b_re, v_s5_b_im, v_s5_c_re, v_s5_c_im, v_s5_d, v_s5_w_glu, v_s5_b_glu, v_cv_w_pw1, v_cv_b_pw1, v_cv_w_dw, v_cv_b_dw, v_cv_ln_g, v_cv_ln_b, v_cv_w_pw2, v_cv_b_pw2, v_ffn_w_gate, v_ffn_w_up, v_ffn_w_down):
    p = dict(locals())
    me = _me()
    s, d = x.shape[1], x.shape[2]
    depth = norm_g.shape[0]
    h = x.reshape(s, d)
    target = loss_target.reshape(s, d)

    pieces = [p[n].reshape(-1) for n in SMALL_SHARDED] + [c.reshape(-1)]
    parts = _unpack_flat(all_gather(_pack_flat(pieces, F32), "ag_small"), [q.shape[0] for q in pieces])
    full = {n: _unshard_last(part, p[n].shape) for n, part in zip(SMALL_SHARDED, parts)}
    c_all = parts[-1]

    nmod = w_mod.shape[2]
    b_cols = lax.dynamic_slice_in_dim(b_mod, me * nmod, nmod, axis=1)
    mod_cols = mod_fwd(c_all, w_mod, b_cols, "mod_fwd")
    g_mod = all_gather(mod_cols.reshape(depth * N_DEV, nmod), "ag_mod").reshape(N_DEV, depth, N_DEV, nmod)
    mod = jnp.moveaxis(lax.dynamic_index_in_dim(g_mod, me, axis=2, keepdims=False), 0, 1).reshape(depth, N_DEV * nmod)
    ng = full["norm_g"]

    def layer_pieces(l):
        kind, j = l % 3, l // 3
        if kind == 0:
            ps = [("qkv", sb_w_qkv[j], "col"), ("o", sb_w_o[j], "row")]
        elif kind == 1:
            ps = [("glu", s5_w_glu[j], "col")]
        else:
            ps = [("pw1", cv_w_pw1[j], "col"), ("pw2", cv_w_pw2[j], "row")]
        return ps + [("gate", ffn_w_gate[l], "col"), ("up", ffn_w_up[l], "col"), ("down", ffn_w_down[l], "row")]

    def gather_layer(l):
        ps = layer_pieces(l)
        got = all_gather(_pack_flat([a.reshape(-1) for _, a, _ in ps], BF16), f"ag_w{l}")
        out = {}
        for (key, a, how), part in zip(ps, _unpack_flat(got, [a.size for _, a, _ in ps])):
            r, cc = a.shape
            blk = part.reshape(N_DEV, r, cc)
            out[key] = blk.transpose(1, 0, 2).reshape(r, N_DEV * cc) if how == "col" else blk.reshape(N_DEV * r, cc)
        return out

    def scatter_layer(l, grads):
        ps = layer_pieces(l)
        slabs = []
        for key, a, how in ps:
            r, cc = a.shape
            g = grads[key]
            slabs.append(g.reshape(r, N_DEV, cc).transpose(1, 0, 2).reshape(N_DEV, r * cc) if how == "col"
                         else g.reshape(N_DEV, r * cc))
        tot = sum_slots(exchange(_pack_flat(slabs, BF16), f"rs_x{l}"), f"rs_sum{l}")
        return {key: part.reshape(a.shape)
                for (key, a, _), part in zip(ps, _unpack_flat(tot, [a.size for _, a, _ in ps]))}

    saved = []
    for l in range(depth):
        kind, j = l % 3, l // 3
        w = gather_layer(l)
        sh_m, sc_m, g_m, sh_f, sc_f, g_f = jnp.split(mod[l], 6)
        st = {"w": w, "h": h}
        if kind == 0:
            (u,) = norm_mod_fwd(h, ng[l, 0], sc_m, sh_m, [BF16], f"nm_a{l}")
            qkv = mm([(u, w["qkv"])], "nn", BF16, name=f"qkv{l}")
            o = attn_fwd(qkv, f"attn_fwd{l}")
            m = mm([(o, w["o"])], "nn", F32, name=f"attn_o{l}")
            st.update(u=u, qkv=qkv, o=o)
        elif kind == 1:
            (u,) = norm_mod_fwd(h, ng[l, 0], sc_m, sh_m, [F32], f"nm_a{l}")
            ops = s5_operands(s5_lam_re[j], s5_lam_im[j], s5_log_dt[j], s5_b_re[j], s5_b_im[j], s5_c_re[j],
                              s5_c_im[j], d)
            xr, xi, yy, gl = s5_fwd(u, *ops[:5], s5_d[j], f"s5_fwd{l}")
            p1, p2, m = mm_dual(gl, w["glu"], s5_b_glu[j], "glu", F32, F32, f"s5_glu{l}")
            st.update(u=u, ops=ops, xr=xr, xi=xi, yy=yy, gl=gl, p1=p1, p2=p2)
        else:
            (u,) = norm_mod_fwd(h, ng[l, 0], sc_m, sh_m, [BF16], f"nm_a{l}")
            p1, p2, hg = mm_dual(u, w["pw1"], full["cv_b_pw1"][j], "glu", F32, F32, f"cv_pw1{l}")
            hc, hs = conv_mid_fwd(hg, full["cv_w_dw"][j], full["cv_b_dw"][j], full["cv_ln_g"][j],
                                  full["cv_ln_b"][j], f"cv_mid{l}")
            m = mm([(hs, w["pw2"])], "nn", F32, bias=full["cv_b_pw2"][j], name=f"cv_pw2{l}")
            st.update(u=u, p1=p1, p2=p2, hg=hg, hc=hc, hs=hs)
        h2 = resid_fwd(h, m, ng[l, 1], g_m, f"res_a{l}")
        (u2,) = norm_mod_fwd(h2, ng[l, 2], sc_f, sh_f, [BF16], f"nm_f{l}")
        w_gu = jnp.concatenate([w["gate"], w["up"]], axis=1)
        f1, f2, z = mm_dual(u2, w_gu, None, "swiglu", BF16, BF16, f"ffn_up{l}")
        f = mm([(z, w["down"])], "nn", F32, name=f"ffn_down{l}")
        h = resid_fwd(h2, f, ng[l, 3], g_f, f"res_f{l}")
        st.update(m=m, h2=h2, u2=u2, f1=f1, f2=f2, z=z, f=f)
        saved.append(st)

    loss_arr, dh = loss_and_grad(h, target, "loss")
    loss = lax.psum(loss_arr[0, 0], AXES)

    nl_sb, nl_s5, nl_cv = sb_w_qkv.shape[0], s5_w_glu.shape[0], cv_w_pw1.shape[0]
    gbig = {n: [None] * p[n].shape[0] for n in BIG if n != "w_mod"}
    dng = [None] * depth
    dmod = [None] * depth
    gs5 = {n: [None] * nl_s5 for n in SMALL if n.startswith("s5_")}
    gcv = {n: [None] * nl_cv for n in SMALL if n.startswith("cv_")}
    for l in reversed(range(depth)):
        kind, j = l % 3, l // 3
        st = saved[l]
        w = st["w"]
        sh_m, sc_m, g_m, sh_f, sc_f, g_f = jnp.split(mod[l], 6)
        gw = {}
        df, s_rf = resid_bwd(dh, st["f"], ng[l, 3], g_f, BF16, f"res_f_bwd{l}")
        dz = mm([(df, w["down"])], "nt", F32, name=f"ffn_dz{l}")
        gw["down"] = mm([(st["z"], df)], "tn", BF16, name=f"ffn_dwd{l}")
        d1, d2, _ = dual_bwd(dz, st["f1"], st["f2"], "swiglu", f"ffn_act_bwd{l}")
        du2 = mm([(d1, w["gate"]), (d2, w["up"])], "nt", F32, name=f"ffn_du{l}")
        gw["gate"] = mm([(st["u2"], d1)], "tn", BF16, name=f"ffn_dwg{l}")
        gw["up"] = mm([(st["u2"], d2)], "tn", BF16, name=f"ffn_dwu{l}")
        dh2, s_nf = norm_mod_bwd([du2], st["h2"], dh, ng[l, 2], sc_f, f"nm_f_bwd{l}")
        dm, s_rm = resid_bwd(dh2, st["m"], ng[l, 1], g_m, F32 if kind == 1 else BF16, f"res_a_bwd{l}")
        if kind == 0:
            do = mm([(dm, w["o"])], "nt", BF16, name=f"attn_do{l}")
            gw["o"] = mm([(st["o"], dm)], "tn", BF16, name=f"attn_dwo{l}")
            dq, dk, dv = attn_bwd(st["qkv"], do, f"attn_bwd{l}")
            wq = w["qkv"]
            dus = [mm([(dq, wq[:, :d]), (dk, wq[:, d:2 * d]), (dv, wq[:, 2 * d:])], "nt", F32, name=f"qkv_du{l}")]
            gw["qkv"] = jnp.concatenate([mm([(st["u"], t)], "tn", BF16, name=f"qkv_dw{l}_{i}")
                                         for i, t in enumerate((dq, dk, dv))], axis=1)
        elif kind == 1:
            d1, d2, cs = dual_bwd(dm, st["p1"], st["p2"], "glu", f"s5_glu_bwd{l}")
            gs5["s5_b_glu"][j] = jnp.concatenate([cs[0], cs[1]])
            wg = w["glu"]
            dgl = mm([(d1, wg[:, :d]), (d2, wg[:, d:])], "nt", F32, name=f"s5_dgl{l}")
            gw["glu"] = jnp.concatenate([mm([(st["gl"], t)], "tn", BF16, name=f"s5_dwglu{l}_{i}")
                                         for i, t in enumerate((d1, d2))], axis=1)
            ops = st["ops"]
            du, dwbr, dwbi, dwcr, dwci, da, dd = s5_bwd(dgl, st["yy"], st["u"], st["xr"], st["xi"], *ops[:4],
                                                        ops[5], s5_d[j], f"s5_bwd{l}")
            dus = [du]
            ngrp = s5_lam_re.shape[1]
            ext = lambda t: _blockdiag_extract(t, ngrp).transpose(0, 2, 1)
            _, disc_vjp = jax.vjp(_s5_discretize, s5_lam_re[j], s5_lam_im[j], s5_log_dt[j], s5_b_re[j], s5_b_im[j])
            shp = s5_lam_re[j].shape
            dlr, dli, dldt, dbr, dbi = disc_vjp((da[0].reshape(shp), da[1].reshape(shp), ext(dwbr), ext(dwbi)))
            for n, t in (("s5_lam_re", dlr), ("s5_lam_im", dli), ("s5_log_dt", dldt), ("s5_b_re", dbr),
                         ("s5_b_im", dbi), ("s5_c_re", ext(dwcr)), ("s5_c_im", -ext(dwci)), ("s5_d", dd[0])):
                gs5[n][j] = t
        else:
            dhs = mm([(dm, w["pw2"])], "nt", BF16, name=f"cv_dhs{l}")
            gw["pw2"] = mm([(st["hs"], dm)], "tn", BF16, name=f"cv_dwpw2{l}")
            dhg, dwdw, s_cv = conv_mid_bwd(dhs, st["hc"], st["hg"], full["cv_w_dw"][j], full["cv_ln_g"][j],
                                           full["cv_ln_b"][j], f"cv_mid_bwd{l}")
            d1, d2, cs = dual_bwd(dhg, st["p1"], st["p2"], "glu", f"cv_glu_bwd{l}")
            wp = w["pw1"]
            dus = [mm([(d1, wp[:, :d]), (d2, wp[:, d:])], "nt", F32, name=f"cv_du{l}")]
            gw["pw1"] = jnp.concatenate([mm([(st["u"], t)], "tn", BF16, name=f"cv_dwpw1{l}_{i}")
                                         for i, t in enumerate((d1, d2))], axis=1)
            for n, t in (("cv_b_pw1", jnp.concatenate([cs[0], cs[1]])), ("cv_w_dw", dwdw[:cv_w_dw.shape[1]]),
                         ("cv_b_dw", s_cv[0]), ("cv_ln_g", s_cv[1]), ("cv_ln_b", s_cv[2]), ("cv_b_pw2", s_rm[2])):
                gcv[n][j] = t
        dh, s_nm = norm_mod_bwd(dus, st["h"], dh2, ng[l, 0], sc_m, f"nm_a_bwd{l}")
        dng[l] = jnp.stack([s_nm[2], s_rm[1], s_nf[2], s_rf[1]])
        dmod[l] = jnp.concatenate([s_nm[0], s_nm[1], s_rm[0], s_nf[0], s_nf[1], s_rf[0]])
        got = scatter_layer(l, gw)
        for key, name in (("qkv", "sb_w_qkv"), ("o", "sb_w_o"), ("glu", "s5_w_glu"), ("pw1", "cv_w_pw1"),
                          ("pw2", "cv_w_pw2")):
            if key in got:
                gbig[name][j] = got[key]
        for key, name in (("gate", "ffn_w_gate"), ("up", "ffn_w_up"), ("down", "ffn_w_down")):
            gbig[name][l] = got[key]

    local = {"norm_g": jnp.stack(dng), "b_mod": jnp.stack(dmod)}
    local.update({n: jnp.stack(t) for n, t in gs5.items()})
    local.update({n: jnp.stack(t) for n, t in gcv.items()})
    pieces = [local[n].reshape(-1) for n in SMALL]
    sizes = [q.shape[0] for q in pieces]
    gathered = all_gather(_pack_flat(pieces, F32), "ag_grads")
    sums = _unpack_flat(sum_slots(gathered, "sum_grads"), sizes)
    grads = {}
    for n, t in zip(SMALL, sums):
        t = t.reshape(local[n].shape)
        if n in SMALL_SHARDED:
            nsh = p[n].shape[-1]
            t = lax.dynamic_slice_in_dim(t, me * nsh, nsh, axis=t.ndim - 1)
        grads[n] = t.reshape(p[n].shape)
    dmod_all = _unpack_flat(gathered, sizes)[SMALL.index("b_mod")].reshape(N_DEV, depth, N_DEV * nmod)
    dmod_cols = jnp.moveaxis(lax.dynamic_slice_in_dim(dmod_all, me * nmod, nmod, axis=2), 0, 1)
    grads["w_mod"] = mod_bwd(c_all, dmod_cols, "mod_bwd")
    for n in gbig:
        grads[n] = jnp.stack(gbig[n])

    delta, new_m, new_v = {}, {}, {}
    for n in BIG:
        shp = p[n].shape
        two = lambda t: t.reshape(-1, shp[-1])
        delta[n], new_m[n], new_v[n] = (t.reshape(shp) for t in
                                        adamw(two(p[n]), two(grads[n]), two(p["m_" + n]), two(p["v_" + n]), f"adamw_{n}"))
    sizes = [p[n].size for n in SMALL]
    packs = [_pack_flat([t[n].reshape(-1) for n in SMALL], F32)
             for t in (p, grads, {n: p["m_" + n] for n in SMALL}, {n: p["v_" + n] for n in SMALL})]
    for res, out in zip(adamw(*packs, "adamw_small"), (delta, new_m, new_v)):
        for n, t in zip(SMALL, _unpack_flat(res, sizes)):
            out[n] = t.reshape(p[n].shape)

    return (loss, dh.reshape(x.shape), *[grads[n] for n in WEIGHTS], *[delta[n] for n in WEIGHTS],
            *[new_m[n] for n in WEIGHTS], *[new_v[n] for n in WEIGHTS])
```

```python
import functools
import math

import jax
import jax.numpy as jnp
from jax import lax
from jax.experimental import pallas as pl
from jax.experimental.pallas import tpu as pltpu

F32, BF16 = jnp.float32, jnp.bfloat16
N_DEV = 8
AXES = ("x", "y", "c")
EPS = 1e-6
HEAD_DIM = 64
LANES = 128
SUBLANES = 8
VMEM_LIMIT = 56 * 1024 * 1024
S5_GROUP = 16
ADAM_LR, ADAM_B1, ADAM_B2, ADAM_EPS, ADAM_WD, ADAM_STEP = 0.001, 0.9, 0.999, 1e-08, 0.01, 10
NEG_CUTOFF = -104.0


def _cp(sem):
    return pltpu.CompilerParams(dimension_semantics=sem, vmem_limit_bytes=VMEM_LIMIT)


def _tile(n, cap, mult=LANES):
    best = None
    for t in range(mult, min(n, cap) + 1, mult):
        if n % t == 0:
            best = t
    return best if best is not None else n


def _sigmoid(x):
    return 1.0 / (1.0 + jnp.exp(-x))


_DIMS = {"nn": (((1,), (0,)), ((), ())), "nt": (((1,), (1,)), ((), ())), "tn": (((0,), (0,)), ((), ()))}


CAPS_FULL_K = (1024, 1536, 4096)
CAPS_PAIRS = (512, 1024, 1408)
CAPS_TN = (4096, 4096, 512)


def mm(pairs, mode, out_dtype=F32, bias=None, name="mm", caps=None):
    a0, b0 = pairs[0]
    if mode == "nn":
        (m, k), n = a0.shape, b0.shape[1]
    elif mode == "nt":
        (m, k), n = a0.shape, b0.shape[0]
    else:
        (k, m), n = a0.shape, b0.shape[1]
    if caps is None:
        caps = CAPS_TN if mode == "tn" else (CAPS_FULL_K if len(pairs) == 1 or k <= 1024 else CAPS_PAIRS)
    tm, tn, tk = _tile(m, caps[0]), _tile(n, caps[1]), _tile(k, caps[2])
    nk = k // tk
    npairs = len(pairs)
    dims = _DIMS[mode]

    def body(*refs):
        ins = refs[:2 * npairs]
        bias_ref = refs[2 * npairs] if bias is not None else None
        o_ref, acc = refs[-2], refs[-1]
        kk = pl.program_id(2)
        part = None
        for p in range(npairs):
            d = lax.dot_general(ins[2 * p][...].astype(BF16), ins[2 * p + 1][...].astype(BF16), dims,
                                preferred_element_type=F32)
            part = d if part is None else part + d

        def finish(r):
            if bias_ref is not None:
                r = r + bias_ref[...]
            o_ref[...] = r.astype(o_ref.dtype)

        if nk == 1:
            finish(part)
        else:
            @pl.when(kk == 0)
            def _():
                acc[...] = part

            @pl.when(kk > 0)
            def _():
                acc[...] += part

            @pl.when(kk == nk - 1)
            def _():
                finish(acc[...])

    if mode == "nn":
        sa, sb = pl.BlockSpec((tm, tk), lambda i, j, kk: (i, kk)), pl.BlockSpec((tk, tn), lambda i, j, kk: (kk, j))
    elif mode == "nt":
        sa, sb = pl.BlockSpec((tm, tk), lambda i, j, kk: (i, kk)), pl.BlockSpec((tn, tk), lambda i, j, kk: (j, kk))
    else:
        sa, sb = pl.BlockSpec((tk, tm), lambda i, j, kk: (kk, i)), pl.BlockSpec((tk, tn), lambda i, j, kk: (kk, j))
    in_specs, args = [], []
    for a, b in pairs:
        in_specs += [sa, sb]
        args += [a, b]
    if bias is not None:
        in_specs.append(pl.BlockSpec((1, tn), lambda i, j, kk: (0, j)))
        args.append(bias.reshape(1, n).astype(F32))
    return pl.pallas_call(
        body, name=name, grid=(m // tm, n // tn, nk), in_specs=in_specs,
        out_specs=pl.BlockSpec((tm, tn), lambda i, j, kk: (i, j)),
        out_shape=jax.ShapeDtypeStruct((m, n), out_dtype),
        scratch_shapes=[pltpu.VMEM((tm, tn) if nk > 1 else (SUBLANES, LANES), F32)],
        compiler_params=_cp(("parallel", "parallel", "arbitrary")),
    )(*args)


def _act_fwd(kind, p1, p2):
    if kind == "swiglu":
        return p1 * _sigmoid(p1) * p2
    return p1 * _sigmoid(p2)


def _act_bwd(kind, d, p1, p2):
    if kind == "swiglu":
        s = _sigmoid(p1)
        return d * p2 * s * (1.0 + p1 * (1.0 - s)), d * (p1 * s)
    s = _sigmoid(p2)
    return d * s, d * p1 * s * (1.0 - s)


def mm_dual(a, w, bias, kind, pre_dtype, act_dtype, name):
    m, k = a.shape
    n = w.shape[1] // 2
    tm, tn = _tile(m, 512), _tile(n, 1536)
    nb = n // tn

    def body(*refs):
        a_ref, w1_ref, w2_ref = refs[:3]
        p1_ref, p2_ref, act_ref = refs[-3:]
        av = a_ref[...].astype(BF16)
        p1 = jnp.dot(av, w1_ref[...].astype(BF16), preferred_element_type=F32)
        p2 = jnp.dot(av, w2_ref[...].astype(BF16), preferred_element_type=F32)
        if bias is not None:
            p1 = p1 + refs[3][...]
            p2 = p2 + refs[4][...]
        p1_ref[...] = p1.astype(p1_ref.dtype)
        p2_ref[...] = p2.astype(p2_ref.dtype)
        act_ref[...] = _act_fwd(kind, p1, p2).astype(act_ref.dtype)

    in_specs = [pl.BlockSpec((tm, k), lambda i, j: (i, 0)), pl.BlockSpec((k, tn), lambda i, j: (0, j)),
                pl.BlockSpec((k, tn), lambda i, j: (0, j + nb))]
    args = [a, w, w]
    if bias is not None:
        b2 = bias.reshape(1, 2 * n).astype(F32)
        in_specs += [pl.BlockSpec((1, tn), lambda i, j: (0, j)), pl.BlockSpec((1, tn), lambda i, j: (0, j + nb))]
        args += [b2, b2]
    ospec = pl.BlockSpec((tm, tn), lambda i, j: (i, j))
    return pl.pallas_call(
        body, name=name, grid=(m // tm, nb), in_specs=in_specs, out_specs=[ospec, ospec, ospec],
        out_shape=[jax.ShapeDtypeStruct((m, n), pre_dtype), jax.ShapeDtypeStruct((m, n), pre_dtype),
                   jax.ShapeDtypeStruct((m, n), act_dtype)],
        compiler_params=_cp(("parallel", "parallel")),
    )(*args)


def mm_act_bwd(dy, w, p1, p2, kind, name):
    m, k = dy.shape
    n = w.shape[0]
    tm, tn = _tile(m, 512), _tile(n, 1536)

    def body(dy_ref, w_ref, p1_ref, p2_ref, d1_ref, d2_ref):
        dact = lax.dot_general(dy_ref[...].astype(BF16), w_ref[...].astype(BF16), _DIMS["nt"],
                               preferred_element_type=F32)
        d1, d2 = _act_bwd(kind, dact, p1_ref[...].astype(F32), p2_ref[...].astype(F32))
        d1_ref[...] = d1.astype(BF16)
        d2_ref[...] = d2.astype(BF16)

    spec = pl.BlockSpec((tm, tn), lambda i, j: (i, j))
    sd = jax.ShapeDtypeStruct((m, n), BF16)
    return pl.pallas_call(
        body, name=name, grid=(m // tm, n // tn),
        in_specs=[pl.BlockSpec((tm, k), lambda i, j: (i, 0)), pl.BlockSpec((tn, k), lambda i, j: (j, 0)), spec, spec],
        out_specs=[spec, spec], out_shape=[sd, sd], compiler_params=_cp(("parallel", "parallel")),
    )(dy, w, p1, p2)


def dual_bwd(dact, p1, p2, kind, name):
    m, n = dact.shape
    tm, tn = _tile(m, 512), _tile(n, 512)

    def body(d_ref, p1_ref, p2_ref, d1_ref, d2_ref, s_ref):
        d1, d2 = _act_bwd(kind, d_ref[...].astype(F32), p1_ref[...].astype(F32), p2_ref[...].astype(F32))
        d1_ref[...] = d1.astype(BF16)
        d2_ref[...] = d2.astype(BF16)

        @pl.when(pl.program_id(1) == 0)
        def _():
            s_ref[...] = jnp.zeros_like(s_ref)

        s_ref[0:1, :] += jnp.sum(d1, axis=0, keepdims=True)
        s_ref[1:2, :] += jnp.sum(d2, axis=0, keepdims=True)

    spec = pl.BlockSpec((tm, tn), lambda j, i: (i, j))
    return pl.pallas_call(
        body, name=name, grid=(n // tn, m // tm), in_specs=[spec, spec, spec],
        out_specs=[spec, spec, pl.BlockSpec((SUBLANES, tn), lambda j, i: (0, j))],
        out_shape=[jax.ShapeDtypeStruct((m, n), BF16), jax.ShapeDtypeStruct((m, n), BF16),
                   jax.ShapeDtypeStruct((SUBLANES, n), F32)],
        compiler_params=_cp(("parallel", "arbitrary")),
    )(dact, p1, p2)


def _rms(x):
    r = lax.rsqrt(jnp.mean(x * x, axis=-1, keepdims=True) + EPS)
    return x * r, r


def _vec8(*rows):
    d = rows[0].shape[-1]
    out = jnp.zeros((SUBLANES, d), F32)
    for i, r in enumerate(rows):
        out = out.at[i].set(r.reshape(d).astype(F32))
    return out


def norm_mod_fwd(h, g, scale, shift, out_dtypes, name):
    s, d = h.shape
    ts = _tile(s, 512, SUBLANES)
    vec = _vec8(g, 1.0 + scale, shift)

    def body(h_ref, v_ref, *outs):
        hh, _ = _rms(h_ref[...])
        u = hh * v_ref[0:1, :] * v_ref[1:2, :] + v_ref[2:3, :]
        for o in outs:
            o[...] = u.astype(o.dtype)

    spec = pl.BlockSpec((ts, d), lambda i: (i, 0))
    return pl.pallas_call(
        body, name=name, grid=(s // ts,), in_specs=[spec, pl.BlockSpec((SUBLANES, d), lambda i: (0, 0))],
        out_specs=[spec] * len(out_dtypes), out_shape=[jax.ShapeDtypeStruct((s, d), t) for t in out_dtypes],
        compiler_params=_cp(("parallel",)),
    )(h, vec)


def norm_mod_bwd(dus, h, dh_in, g, scale, name):
    s, d = h.shape
    ts = _tile(s, 256, SUBLANES)
    vec = _vec8(g, 1.0 + scale)
    nd = len(dus)

    def body(*refs):
        du = refs[0][...].astype(F32)
        for r in refs[1:nd]:
            du = du + r[...].astype(F32)
        h_ref, dhi_ref, v_ref, dh_ref, s_ref = refs[nd:]
        hh, r = _rms(h_ref[...])
        gg, sc1 = v_ref[0:1, :], v_ref[1:2, :]
        dn = du * sc1
        dhh = dn * gg
        dh = r * (dhh - hh * jnp.mean(dhh * hh, axis=-1, keepdims=True))
        dh_ref[...] = dhi_ref[...] + dh

        @pl.when(pl.program_id(0) == 0)
        def _():
            s_ref[...] = jnp.zeros_like(s_ref)

        s_ref[0:1, :] += jnp.sum(du, axis=0, keepdims=True)
        s_ref[1:2, :] += jnp.sum(du * (hh * gg), axis=0, keepdims=True)
        s_ref[2:3, :] += jnp.sum(dn * hh, axis=0, keepdims=True)

    spec = pl.BlockSpec((ts, d), lambda i: (i, 0))
    vspec = pl.BlockSpec((SUBLANES, d), lambda i: (0, 0))
    return pl.pallas_call(
        body, name=name, grid=(s // ts,), in_specs=[spec] * (nd + 2) + [vspec], out_specs=[spec, vspec],
        out_shape=[jax.ShapeDtypeStruct((s, d), F32), jax.ShapeDtypeStruct((SUBLANES, d), F32)],
        compiler_params=_cp(("arbitrary",)),
    )(*dus, h, dh_in, vec)


def resid_fwd(h, m, g, gate, name):
    s, d = h.shape
    ts = _tile(s, 512, SUBLANES)
    vec = _vec8(g, gate)

    def body(h_ref, m_ref, v_ref, o_ref):
        mh, _ = _rms(m_ref[...])
        o_ref[...] = h_ref[...] + v_ref[1:2, :] * (mh * v_ref[0:1, :])

    spec = pl.BlockSpec((ts, d), lambda i: (i, 0))
    return pl.pallas_call(
        body, name=name, grid=(s // ts,), in_specs=[spec, spec, pl.BlockSpec((SUBLANES, d), lambda i: (0, 0))],
        out_specs=spec, out_shape=jax.ShapeDtypeStruct((s, d), F32), compiler_params=_cp(("parallel",)),
    )(h, m, vec)


def resid_bwd(dh2, m, g, gate, out_dtype, name):
    s, d = m.shape
    ts = _tile(s, 256, SUBLANES)
    vec = _vec8(g, gate)

    def body(d_ref, m_ref, v_ref, dm_ref, s_ref):
        dh = d_ref[...]
        mh, r = _rms(m_ref[...])
        gg, gt = v_ref[0:1, :], v_ref[1:2, :]
        dmh = dh * (gt * gg)
        dm = r * (dmh - mh * jnp.mean(dmh * mh, axis=-1, keepdims=True))
        dm_ref[...] = dm.astype(dm_ref.dtype)

        @pl.when(pl.program_id(0) == 0)
        def _():
            s_ref[...] = jnp.zeros_like(s_ref)

        s_ref[0:1, :] += jnp.sum(dh * (mh * gg), axis=0, keepdims=True)
        s_ref[1:2, :] += jnp.sum(dh * gt * mh, axis=0, keepdims=True)
        s_ref[2:3, :] += jnp.sum(dm, axis=0, keepdims=True)

    spec = pl.BlockSpec((ts, d), lambda i: (i, 0))
    vspec = pl.BlockSpec((SUBLANES, d), lambda i: (0, 0))
    return pl.pallas_call(
        body, name=name, grid=(s // ts,), in_specs=[spec, spec, vspec], out_specs=[spec, vspec],
        out_shape=[jax.ShapeDtypeStruct((s, d), out_dtype), jax.ShapeDtypeStruct((SUBLANES, d), F32)],
        compiler_params=_cp(("arbitrary",)),
    )(dh2, m, vec)


def loss_and_grad(h, target, name):
    s, d = h.shape
    ts = _tile(s, 512, SUBLANES)

    def body(h_ref, t_ref, l_ref, dy_ref):
        e = h_ref[...] - t_ref[...]
        dy_ref[...] = e * (1.0 / d)

        @pl.when(pl.program_id(0) == 0)
        def _():
            l_ref[...] = jnp.zeros_like(l_ref)

        l_ref[...] += (0.5 / d) * jnp.sum(e * e)

    spec = pl.BlockSpec((ts, d), lambda i: (i, 0))
    lspec = pl.BlockSpec((SUBLANES, LANES), lambda i: (0, 0))
    return pl.pallas_call(
        body, name=name, grid=(s // ts,), in_specs=[spec, spec], out_specs=[lspec, spec],
        out_shape=[jax.ShapeDtypeStruct((SUBLANES, LANES), F32), jax.ShapeDtypeStruct((s, d), F32)],
        compiler_params=_cp(("arbitrary",)),
    )(h, target)


TK = 128
N_PAIR_HEADS = LANES // HEAD_DIM
TQ_FWD, SLAB_FWD = 256, 4
TQ_BWD, SLAB_BWD = 128, 3


def _cum_matrices():
    a = jnp.arange(TK)
    ones = jnp.ones((TK, TK), F32)
    suffix = (a[:, None] > a[None, :]).astype(F32)
    prefix = (a[:, None] < a[None, :]).astype(F32)
    mk = lambda u: jnp.tile(jnp.concatenate([u, ones], axis=1), (2, 1)).astype(BF16)
    return mk(suffix), mk(prefix)


def _split_dot(x, cum):
    hi = x.astype(BF16)
    lo = (x - hi.astype(F32)).astype(BF16)
    return jnp.dot(jnp.concatenate([hi, lo], axis=1), cum, preferred_element_type=F32)


def _sb_slab(qs, k_slab, cum, nb, r, mask):
    m = qs.shape[0]
    z = lax.dot_general(qs, k_slab, _DIMS["nt"], preferred_element_type=F32)
    lb = jnp.minimum(z, 0.0) - jnp.log(1.0 + jnp.exp(-jnp.abs(z)))
    lk = lb - z
    if mask is not None:
        lk = jnp.where(mask, lk, 0.0)
    t = _split_dot(jnp.concatenate([lk[:, b * TK:(b + 1) * TK] for b in range(nb)], axis=0), cum)
    cs = [None] * nb
    for b in reversed(range(nb)):
        tb = t[b * m:(b + 1) * m]
        cs[b] = tb[:, :TK] + r
        r = r + tb[:, TK:]
    a = jnp.exp(lb + jnp.concatenate(cs, axis=1))
    if mask is not None:
        a = jnp.where(mask, a, 0.0)
    return lb, a, r


def _stack_heads(x, heads):
    return jnp.concatenate([jnp.where(hm, x, jnp.zeros_like(x)) for hm in heads], axis=0)


def _unstack_heads(xs, heads, tq):
    return jnp.where(heads[0], xs[0:tq], xs[tq:2 * tq])


def _head_masks(rows):
    lane = lax.broadcasted_iota(jnp.int32, (rows, LANES), 1)
    return [(lane >= hh * HEAD_DIM) & (lane < (hh + 1) * HEAD_DIM) for hh in range(N_PAIR_HEADS)]


def _slab_geometry(i, tq, nb):
    m = N_PAIR_HEADS * tq
    sb = jnp.maximum((i + 1) * (tq // TK) - nb, 0)
    rowq = lax.broadcasted_iota(jnp.int32, (m, nb * TK), 0) & (tq - 1)
    col = lax.broadcasted_iota(jnp.int32, (m, nb * TK), 1)
    mask = (col - rowq) < (i * tq - sb * TK)
    return sb, mask


def attn_fwd(qkv, name):
    s, d3 = qkv.shape
    d = d3 // 3
    npair = d // LANES
    tq = min(TQ_FWD, s)
    nb = SLAB_FWD
    m = N_PAIR_HEADS * tq
    scale = HEAD_DIM ** -0.5
    cum_s, _ = _cum_matrices()

    def body(q_ref, k_ref, v_ref, c_ref, o_ref):
        i = pl.program_id(1)
        cum = c_ref[...]
        heads = _head_masks(tq)
        qs = _stack_heads((q_ref[...].astype(F32) * scale).astype(BF16), heads)
        sb, mask = _slab_geometry(i, tq, nb)
        off = pl.multiple_of(sb * TK, TK)
        _, w, r = _sb_slab(qs, k_ref[pl.ds(off, nb * TK), :], cum, nb, jnp.zeros((m, TK), F32), mask)
        acc = jnp.dot(w.astype(BF16), v_ref[pl.ds(off, nb * TK), :], preferred_element_type=F32)

        def cond(c):
            return jnp.logical_and(c[0] >= 0, jnp.max(c[1]) > NEG_CUTOFF)

        def step(c):
            j, r, acc = c
            off = pl.multiple_of(j * TK, TK)
            _, w, r = _sb_slab(qs, k_ref[pl.ds(off, TK), :], cum, 1, r, None)
            return j - 1, r, acc + jnp.dot(w.astype(BF16), v_ref[pl.ds(off, TK), :], preferred_element_type=F32)

        _, _, acc = lax.while_loop(cond, step, (sb - 1, r, acc))
        o_ref[...] = _unstack_heads(acc, heads, tq).astype(o_ref.dtype)

    return pl.pallas_call(
        body, name=name, grid=(npair, s // tq),
        in_specs=[pl.BlockSpec((tq, LANES), lambda p, i: (i, p)),
                  pl.BlockSpec((s, LANES), lambda p, i: (0, npair + p)),
                  pl.BlockSpec((s, LANES), lambda p, i: (0, 2 * npair + p)),
                  pl.BlockSpec((2 * TK, 2 * TK), lambda p, i: (0, 0))],
        out_specs=pl.BlockSpec((tq, LANES), lambda p, i: (i, p)),
        out_shape=jax.ShapeDtypeStruct((s, d), BF16),
        compiler_params=_cp(("parallel", "arbitrary")),
    )(qkv, qkv, qkv, cum_s)


def attn_bwd(qkv, do, name):
    s, d3 = qkv.shape
    d = d3 // 3
    npair = d // LANES
    tq = min(TQ_BWD, s)
    nb = SLAB_BWD
    nq = s // tq
    m = N_PAIR_HEADS * tq
    scale = HEAD_DIM ** -0.5
    cum_s, cum_p = _cum_matrices()

    def body(q_ref, k_ref, v_ref, do_ref, cs_ref, cp_ref, dq_ref, dk_ref, dv_ref, dk_acc, dv_acc, e_scr, b_scr):
        i = pl.program_id(1)

        @pl.when(i == 0)
        def _():
            dk_acc[...] = jnp.zeros_like(dk_acc)
            dv_acc[...] = jnp.zeros_like(dv_acc)

        cum_suf = cs_ref[...]
        cum_pre = cp_ref[...]
        heads = _head_masks(tq)
        qs = _stack_heads((q_ref[...].astype(F32) * scale).astype(BF16), heads)
        dos = _stack_heads(do_ref[...], heads)

        def left(off, n, r, mask):
            rows = pl.ds(off, n * TK)
            lb, a, r = _sb_slab(qs, k_ref[rows, :], cum_suf, n, r, mask)
            da = lax.dot_general(dos, v_ref[rows, :], _DIMS["nt"], preferred_element_type=F32)
            dv_acc[rows, :] += lax.dot_general(a.astype(BF16), dos, _DIMS["tn"], preferred_element_type=F32)
            return da * a, jnp.exp(lb), r

        def right(off, n, e, beta, pe, mask):
            rows = pl.ds(off, n * TK)
            t = _split_dot(jnp.concatenate([e[:, b * TK:(b + 1) * TK] for b in range(n)], axis=0), cum_pre)
            ps = [None] * n
            for b in range(n):
                tb = t[b * m:(b + 1) * m]
                ps[b] = tb[:, :TK] + pe
                pe = pe + tb[:, TK:]
            dz = e * (1.0 - beta) - beta * jnp.concatenate(ps, axis=1)
            if mask is not None:
                dz = jnp.where(mask, dz, 0.0)
            dzb = dz.astype(BF16)
            dk_acc[rows, :] += lax.dot_general(dzb, qs, _DIMS["tn"], preferred_element_type=F32)
            return pe, jnp.dot(dzb, k_ref[rows, :], preferred_element_type=F32)

        sb, mask = _slab_geometry(i, tq, nb)
        off0 = pl.multiple_of(sb * TK, TK)
        e0, beta0, r = left(off0, nb, jnp.zeros((m, TK), F32), mask)

        def cond(c):
            return jnp.logical_and(c[0] >= 0, jnp.max(c[1]) > NEG_CUTOFF)

        def tail_left(c):
            j, r = c
            e_scr[j], b_scr[j], r = left(pl.multiple_of(j * TK, TK), 1, r, None)
            return j - 1, r

        jend, _ = lax.while_loop(cond, tail_left, (sb - 1, r))

        def tail_right(j, c):
            pe, dq = c
            pe, dqj = right(pl.multiple_of(j * TK, TK), 1, e_scr[j], b_scr[j], pe, None)
            return pe, dq + dqj

        pe, dq = lax.fori_loop(jend + 1, sb, tail_right, (jnp.zeros((m, TK), F32), jnp.zeros((m, LANES), F32)))
        _, dq0 = right(off0, nb, e0, beta0, pe, mask)
        dq_ref[...] = (_unstack_heads(dq + dq0, heads, tq) * scale).astype(dq_ref.dtype)

        @pl.when(i == nq - 1)
        def _():
            dk_ref[...] = dk_acc[...].astype(dk_ref.dtype)
            dv_ref[...] = dv_acc[...].astype(dv_ref.dtype)

    qspec = pl.BlockSpec((tq, LANES), lambda p, i: (i, p))
    full = lambda base: pl.BlockSpec((s, LANES), lambda p, i: (0, base + p))
    cspec = pl.BlockSpec((2 * TK, 2 * TK), lambda p, i: (0, 0))
    sd = jax.ShapeDtypeStruct((s, d), BF16)
    return pl.pallas_call(
        body, name=name, grid=(npair, nq),
        in_specs=[qspec, full(npair), full(2 * npair), qspec, cspec, cspec],
        out_specs=[qspec, full(0), full(0)], out_shape=[sd, sd, sd],
        scratch_shapes=[pltpu.VMEM((s, LANES), F32), pltpu.VMEM((s, LANES), F32),
                        pltpu.VMEM((s // TK, m, TK), F32), pltpu.VMEM((s // TK, m, TK), F32)],
        compiler_params=_cp(("parallel", "arbitrary")),
    )(qkv, qkv, qkv, do, cum_s, cum_p)


HALO = 32
CONV_ROWS = 128


def _ln_swish(hc, g, b):
    mu = jnp.mean(hc, axis=-1, keepdims=True)
    xc = hc - mu
    rstd = lax.rsqrt(jnp.mean(xc * xc, axis=-1, keepdims=True) + EPS)
    xh = xc * rstd
    hn = xh * g + b
    return xh, rstd, hn


def conv_mid_fwd(x, w_dw, b_dw, ln_g, ln_b, name):
    s, d = x.shape
    width = w_dw.shape[0]
    ts = _tile(s, 256, CONV_ROWS)
    base = HALO - (width - 1)
    wpad = jnp.zeros((HALO, d), F32).at[:width].set(w_dw.astype(F32))
    vec = _vec8(b_dw, ln_g, ln_b)

    def body(x_ref, w_ref, v_ref, hc_ref, hs_ref, win):
        i = pl.program_id(0)

        @pl.when(i == 0)
        def _():
            win[0:HALO, :] = jnp.zeros((HALO, d), F32)

        @pl.when(i > 0)
        def _():
            win[0:HALO, :] = win[ts:ts + HALO, :]

        win[HALO:HALO + ts, :] = x_ref[...]
        for rc in range(ts // CONV_ROWS):
            for lc in range(d // LANES):
                cols = slice(lc * LANES, (lc + 1) * LANES)
                acc = jnp.broadcast_to(v_ref[0:1, cols], (CONV_ROWS, LANES))
                for k in range(width):
                    acc = acc + w_ref[k:k + 1, cols] * win[pl.ds(rc * CONV_ROWS + base + k, CONV_ROWS), cols]
                hc_ref[rc * CONV_ROWS:(rc + 1) * CONV_ROWS, cols] = acc
        _, _, hn = _ln_swish(hc_ref[...], v_ref[1:2, :], v_ref[2:3, :])
        hs_ref[...] = (hn * _sigmoid(hn)).astype(hs_ref.dtype)

    spec = pl.BlockSpec((ts, d), lambda i: (i, 0))
    return pl.pallas_call(
        body, name=name, grid=(s // ts,),
        in_specs=[spec, pl.BlockSpec((HALO, d), lambda i: (0, 0)), pl.BlockSpec((SUBLANES, d), lambda i: (0, 0))],
        out_specs=[spec, spec], out_shape=[jax.ShapeDtypeStruct((s, d), F32), jax.ShapeDtypeStruct((s, d), BF16)],
        scratch_shapes=[pltpu.VMEM((ts + HALO, d), F32)],
        compiler_params=_cp(("arbitrary",)),
    )(x, wpad, vec)


def conv_mid_bwd(dhs, hc, x, w_dw, ln_g, ln_b, name):
    s, d = x.shape
    width = w_dw.shape[0]
    ts = _tile(s, 256, CONV_ROWS)
    nt = s // ts
    base = HALO - (width - 1)
    wpad = jnp.zeros((HALO, d), F32).at[:width].set(w_dw.astype(F32))
    vec = _vec8(ln_g, ln_b)

    def body(dhs_ref, hc_ref, x_ref, xh_ref, w_ref, v_ref, dx_ref, dw_ref, s_ref, dwin, xwin):
        i = pl.program_id(0)

        @pl.when(i == 0)
        def _():
            dwin[ts:ts + HALO, :] = jnp.zeros((HALO, d), F32)
            dw_ref[...] = jnp.zeros_like(dw_ref)
            s_ref[...] = jnp.zeros_like(s_ref)

        @pl.when(i > 0)
        def _():
            dwin[ts:ts + HALO, :] = dwin[0:HALO, :]

        @pl.when(i == nt - 1)
        def _():
            xwin[0:HALO, :] = jnp.zeros((HALO, d), F32)

        @pl.when(i < nt - 1)
        def _():
            xwin[0:HALO, :] = xh_ref[...]

        xwin[HALO:HALO + ts, :] = x_ref[...]
        g = v_ref[0:1, :]
        xh, rstd, hn = _ln_swish(hc_ref[...], g, v_ref[1:2, :])
        sig = _sigmoid(hn)
        dhn = dhs_ref[...].astype(F32) * (sig * (1.0 + hn * (1.0 - sig)))
        dxh = dhn * g
        dhc = rstd * (dxh - jnp.mean(dxh, axis=-1, keepdims=True) - xh * jnp.mean(dxh * xh, axis=-1, keepdims=True))
        dwin[0:ts, :] = dhc
        s_ref[0:1, :] += jnp.sum(dhc, axis=0, keepdims=True)
        s_ref[1:2, :] += jnp.sum(dhn * xh, axis=0, keepdims=True)
        s_ref[2:3, :] += jnp.sum(dhn, axis=0, keepdims=True)
        for rc in range(ts // CONV_ROWS):
            for lc in range(d // LANES):
                cols = slice(lc * LANES, (lc + 1) * LANES)
                r0 = rc * CONV_ROWS
                dch = dwin[r0:r0 + CONV_ROWS, cols]
                acc = jnp.zeros((CONV_ROWS, LANES), F32)
                for k in range(width):
                    acc = acc + w_ref[k:k + 1, cols] * dwin[pl.ds(r0 + (width - 1) - k, CONV_ROWS), cols]
                    dw_ref[k:k + 1, cols] += jnp.sum(dch * xwin[pl.ds(r0 + base + k, CONV_ROWS), cols], axis=0,
                                                     keepdims=True)
                dx_ref[r0:r0 + CONV_ROWS, cols] = acc

    rev = pl.BlockSpec((ts, d), lambda i: (nt - 1 - i, 0))
    halo = pl.BlockSpec((HALO, d), lambda i: (jnp.maximum((nt - 1 - i) * (ts // HALO) - 1, 0), 0))
    vspec = pl.BlockSpec((SUBLANES, d), lambda i: (0, 0))
    wspec = pl.BlockSpec((HALO, d), lambda i: (0, 0))
    return pl.pallas_call(
        body, name=name, grid=(nt,), in_specs=[rev, rev, rev, halo, wspec, vspec],
        out_specs=[rev, wspec, vspec],
        out_shape=[jax.ShapeDtypeStruct((s, d), F32), jax.ShapeDtypeStruct((HALO, d), F32),
                   jax.ShapeDtypeStruct((SUBLANES, d), F32)],
        scratch_shapes=[pltpu.VMEM((ts + HALO, d), F32), pltpu.VMEM((ts + HALO, d), F32)],
        compiler_params=_cp(("arbitrary",)),
    )(dhs, hc, x, x, wpad, vec)


S5_KB = 256
SCAN_LANES = 256
GELU_C = math.sqrt(2.0 / math.pi)
GELU_A = 0.044715


def _gelu(x):
    return 0.5 * x * (1.0 + jnp.tanh(GELU_C * (x + GELU_A * x * x * x)))


def _gelu_grad(x):
    th = jnp.tanh(GELU_C * (x + GELU_A * x * x * x))
    return 0.5 * (1.0 + th) + 0.5 * x * (1.0 - th * th) * GELU_C * (1.0 + 3.0 * GELU_A * x * x)


def _s5_discretize(lam_re, lam_im, log_dt, b_re, b_im):
    dt = jnp.exp(log_dt)[:, None]
    mag = jnp.exp(lam_re * dt)
    ar, ai = mag * jnp.cos(lam_im * dt), mag * jnp.sin(lam_im * dt)
    den = lam_re * lam_re + lam_im * lam_im
    er = ((ar - 1) * lam_re + ai * lam_im) / den
    ei = (ai * lam_re - (ar - 1) * lam_im) / den
    bbr = er[..., None] * b_re - ei[..., None] * b_im
    bbi = er[..., None] * b_im + ei[..., None] * b_re
    return ar, ai, bbr, bbi


def _blockdiag(w, nkb):
    g, r, c = w.shape
    gpb = g // nkb
    eye = jnp.eye(gpb, dtype=w.dtype)
    return jnp.einsum("kgrc,gh->kgrhc", w.reshape(nkb, gpb, r, c), eye).reshape(nkb, gpb * r, gpb * c)


def _blockdiag_extract(m, g):
    nkb = m.shape[0]
    gpb = g // nkb
    r, c = m.shape[1] // gpb, m.shape[2] // gpb
    eye = jnp.eye(gpb, dtype=m.dtype)
    return jnp.einsum("kgrhc,gh->kgrc", m.reshape(nkb, gpb, r, gpb, c), eye).reshape(g, r, c)


def _scan_powers(ar, ai, reverse):
    ar = ar.reshape(-1)
    ai = (-ai if reverse else ai).reshape(-1)
    cmul = lambda x, y: (x[0] * y[0] - x[1] * y[1], x[0] * y[1] + x[1] * y[0])
    a1 = (ar, ai)
    a2 = cmul(a1, a1)
    a4 = cmul(a2, a2)
    r = jnp.arange(SUBLANES)[:, None]
    rows = []
    for sft, p in ((1, a1), (2, a2), (4, a4)):
        keep = (r + sft <= SUBLANES - 1) if reverse else (r >= sft)
        rows += [jnp.where(keep, p[0][None, :], 0.0), jnp.where(keep, p[1][None, :], 0.0)]
    pows = [a1]
    for _ in range(SUBLANES - 1):
        pows.append(cmul(pows[-1], a1))
    if reverse:
        pows = pows[::-1]
    rows += [jnp.stack([p[0] for p in pows]), jnp.stack([p[1] for p in pows])]
    return jnp.concatenate(rows, axis=0).astype(F32)


def _scan_tile(sr, si, pw_ref, car, nrg, reverse):
    nsb = sr.shape[1]
    ch = min(SCAN_LANES, nsb)
    nch = nsb // ch
    row = 0 if reverse else SUBLANES - 1

    def step(t, carry):
        rg = (nrg - 1 - t) if reverse else t
        off = pl.multiple_of(rg * SUBLANES, SUBLANES)
        out = []
        for c in range(nch):
            cols = slice(c * ch, (c + 1) * ch)
            cr, ci = carry[2 * c], carry[2 * c + 1]
            br = sr[pl.ds(off, SUBLANES), cols]
            bi = si[pl.ds(off, SUBLANES), cols]
            for idx, sft in enumerate((1, 2, 4)):
                sh = SUBLANES - sft if reverse else sft
                tr = pltpu.roll(br, sh, axis=0)
                ti = pltpu.roll(bi, sh, axis=0)
                mr = pw_ref[16 * idx:16 * idx + 8, cols]
                mi = pw_ref[16 * idx + 8:16 * idx + 16, cols]
                br, bi = br + mr * tr - mi * ti, bi + mr * ti + mi * tr
            apr, api = pw_ref[48:56, cols], pw_ref[56:64, cols]
            xr = br + apr * cr - api * ci
            xi = bi + apr * ci + api * cr
            sr[pl.ds(off, SUBLANES), cols] = xr
            si[pl.ds(off, SUBLANES), cols] = xi
            out += [jnp.broadcast_to(xr[row:row + 1, :], xr.shape), jnp.broadcast_to(xi[row:row + 1, :], xi.shape)]
        return tuple(out)

    init = []
    for c in range(nch):
        cols = slice(c * ch, (c + 1) * ch)
        init += [car[0:SUBLANES, cols], car[SUBLANES:2 * SUBLANES, cols]]
    fin = lax.fori_loop(0, nrg, step, tuple(init))
    for c in range(nch):
        cols = slice(c * ch, (c + 1) * ch)
        car[0:SUBLANES, cols] = fin[2 * c]
        car[SUBLANES:2 * SUBLANES, cols] = fin[2 * c + 1]


def s5_fwd(u, wb_r, wb_i, wc_r, wc_i, pw, d_skip, name):
    s, d = u.shape
    nkb, kb, nsb = wb_r.shape
    ts = _tile(s, 256, SUBLANES)
    dvec = _vec8(d_skip)

    def body(u_ref, wbr, wbi, wcr, wci, pw_ref, dv_ref, xr_ref, xi_ref, yy_ref, g_ref, sr, si, car):
        @pl.when(pl.program_id(1) == 0)
        def _():
            car[...] = jnp.zeros_like(car)

        uu = u_ref[...]
        ub = uu.astype(BF16)
        sr[...] = jnp.dot(ub, wbr[...], preferred_element_type=F32)
        si[...] = jnp.dot(ub, wbi[...], preferred_element_type=F32)
        _scan_tile(sr, si, pw_ref, car, ts // SUBLANES, False)
        xr, xi = sr[...], si[...]
        xr_ref[...] = xr
        xi_ref[...] = xi
        y = (jnp.dot(xr.astype(BF16), wcr[...], preferred_element_type=F32)
             + jnp.dot(xi.astype(BF16), wci[...], preferred_element_type=F32) + dv_ref[0:1, :] * uu)
        yy_ref[...] = y
        g_ref[...] = _gelu(y).astype(g_ref.dtype)

    cspec = pl.BlockSpec((ts, kb), lambda k, i: (i, k))
    sspec = pl.BlockSpec((ts, nsb), lambda k, i: (i, k))
    wbspec = pl.BlockSpec((None, kb, nsb), lambda k, i: (k, 0, 0))
    wcspec = pl.BlockSpec((None, nsb, kb), lambda k, i: (k, 0, 0))
    ns = nkb * nsb
    return pl.pallas_call(
        body, name=name, grid=(nkb, s // ts),
        in_specs=[cspec, wbspec, wbspec, wcspec, wcspec, pl.BlockSpec((64, nsb), lambda k, i: (0, k)),
                  pl.BlockSpec((SUBLANES, kb), lambda k, i: (0, k))],
        out_specs=[sspec, sspec, cspec, cspec],
        out_shape=[jax.ShapeDtypeStruct((s, ns), F32), jax.ShapeDtypeStruct((s, ns), F32),
                   jax.ShapeDtypeStruct((s, d), F32), jax.ShapeDtypeStruct((s, d), BF16)],
        scratch_shapes=[pltpu.VMEM((ts, nsb), F32), pltpu.VMEM((ts, nsb), F32), pltpu.VMEM((2 * SUBLANES, nsb), F32)],
        compiler_params=_cp(("parallel", "arbitrary")),
    )(u, wb_r, wb_i, wc_r, wc_i, pw, dvec)


def s5_bwd(dg, yy, u, xr, xi, wb_r, wb_i, wc_r, wc_i, pwb, d_skip, name):
    s, d = u.shape
    nkb, kb, nsb = wb_r.shape
    ns = nkb * nsb
    ts = _tile(s, 256, SUBLANES)
    nt = s // ts
    dvec = _vec8(d_skip)

    def body(dg_ref, yy_ref, u_ref, xr_ref, xi_ref, xrp_ref, xip_ref, wbr, wbi, wcr, wci, pw_ref, dv_ref,
             du_ref, dwbr, dwbi, dwcr, dwci, da_ref, dd_ref, sr, si, car):
        i = pl.program_id(1)

        @pl.when(i == 0)
        def _():
            car[...] = jnp.zeros_like(car)
            for r in (dwbr, dwbi, dwcr, dwci, da_ref, dd_ref):
                r[...] = jnp.zeros_like(r)

        uu = u_ref[...]
        dyy = dg_ref[...] * _gelu_grad(yy_ref[...])
        dd_ref[0:1, :] += jnp.sum(dyy * uu, axis=0, keepdims=True)
        dyb = dyy.astype(BF16)
        sr[...] = lax.dot_general(dyb, wcr[...], _DIMS["nt"], preferred_element_type=F32)
        si[...] = lax.dot_general(dyb, wci[...], _DIMS["nt"], preferred_element_type=F32)
        _scan_tile(sr, si, pw_ref, car, ts // SUBLANES, True)
        gr, gi = sr[...], si[...]
        grb, gib = gr.astype(BF16), gi.astype(BF16)
        xrt, xit = xr_ref[...], xi_ref[...]
        dwcr[...] += lax.dot_general(xrt.astype(BF16), dyb, _DIMS["tn"], preferred_element_type=F32)
        dwci[...] += lax.dot_general(xit.astype(BF16), dyb, _DIMS["tn"], preferred_element_type=F32)
        ub = uu.astype(BF16)
        dwbr[...] += lax.dot_general(ub, grb, _DIMS["tn"], preferred_element_type=F32)
        dwbi[...] += lax.dot_general(ub, gib, _DIMS["tn"], preferred_element_type=F32)
        du_ref[...] = (lax.dot_general(grb, wbr[...], _DIMS["nt"], preferred_element_type=F32)
                       + lax.dot_general(gib, wbi[...], _DIMS["nt"], preferred_element_type=F32)
                       + dyy * dv_ref[0:1, :])
        has_prev = (i < nt - 1).astype(F32)
        rowid = lax.broadcasted_iota(jnp.int32, (ts, nsb), 0)
        pr = jnp.broadcast_to(xrp_ref[SUBLANES - 1:SUBLANES, :] * has_prev, (ts, nsb))
        pi = jnp.broadcast_to(xip_ref[SUBLANES - 1:SUBLANES, :] * has_prev, (ts, nsb))
        xpr = jnp.where(rowid == 0, pr, pltpu.roll(xrt, 1, axis=0))
        xpi = jnp.where(rowid == 0, pi, pltpu.roll(xit, 1, axis=0))
        da_ref[0:1, :] += jnp.sum(gr * xpr + gi * xpi, axis=0, keepdims=True)
        da_ref[1:2, :] += jnp.sum(gi * xpr - gr * xpi, axis=0, keepdims=True)

    cspec = pl.BlockSpec((ts, kb), lambda k, i: (nt - 1 - i, k))
    sspec = pl.BlockSpec((ts, nsb), lambda k, i: (nt - 1 - i, k))
    pspec = pl.BlockSpec((SUBLANES, nsb), lambda k, i: (jnp.maximum((nt - 1 - i) * (ts // SUBLANES) - 1, 0), k))
    wbspec = pl.BlockSpec((None, kb, nsb), lambda k, i: (k, 0, 0))
    wcspec = pl.BlockSpec((None, nsb, kb), lambda k, i: (k, 0, 0))
    v8s = pl.BlockSpec((SUBLANES, nsb), lambda k, i: (0, k))
    v8c = pl.BlockSpec((SUBLANES, kb), lambda k, i: (0, k))
    return pl.pallas_call(
        body, name=name, grid=(nkb, nt),
        in_specs=[cspec, cspec, cspec, sspec, sspec, pspec, pspec, wbspec, wbspec, wcspec, wcspec,
                  pl.BlockSpec((64, nsb), lambda k, i: (0, k)), v8c],
        out_specs=[cspec, wbspec, wbspec, wcspec, wcspec, v8s, v8c],
        out_shape=[jax.ShapeDtypeStruct((s, d), F32),
                   jax.ShapeDtypeStruct((nkb, kb, nsb), F32), jax.ShapeDtypeStruct((nkb, kb, nsb), F32),
                   jax.ShapeDtypeStruct((nkb, nsb, kb), F32), jax.ShapeDtypeStruct((nkb, nsb, kb), F32),
                   jax.ShapeDtypeStruct((SUBLANES, ns), F32), jax.ShapeDtypeStruct((SUBLANES, d), F32)],
        scratch_shapes=[pltpu.VMEM((ts, nsb), F32), pltpu.VMEM((ts, nsb), F32), pltpu.VMEM((2 * SUBLANES, nsb), F32)],
        compiler_params=_cp(("parallel", "arbitrary")),
    )(dg, yy, u, xr, xi, xr, xi, wb_r, wb_i, wc_r, wc_i, pwb, dvec)


def s5_operands(lam_re, lam_im, log_dt, b_re, b_im, c_re, c_im, d):
    ar, ai, bbr, bbi = _s5_discretize(lam_re, lam_im, log_dt, b_re, b_im)
    nkb = max(d // S5_KB, 1)
    wb_r = _blockdiag(bbr.transpose(0, 2, 1), nkb).astype(BF16)
    wb_i = _blockdiag(bbi.transpose(0, 2, 1), nkb).astype(BF16)
    wc_r = _blockdiag(c_re.transpose(0, 2, 1), nkb).astype(BF16)
    wc_i = _blockdiag(-c_im.transpose(0, 2, 1), nkb).astype(BF16)
    return wb_r, wb_i, wc_r, wc_i, _scan_powers(ar, ai, False), _scan_powers(ar, ai, True)


MESH = pl.DeviceIdType.MESH
HBM_SPEC = pl.BlockSpec(memory_space=pltpu.HBM)


def _me():
    return 4 * lax.axis_index("x") + 2 * lax.axis_index("y") + lax.axis_index("c")


def all_gather(shard, name):
    m, n = shard.shape

    def body(x_ref, out_ref, send_sems, recv_sems, local_sem):
        x, y, c = lax.axis_index("x"), lax.axis_index("y"), lax.axis_index("c")
        me, sibling = (x, y, c), (x, y, 1 - c)
        chips = [(1 - x, y), (x, 1 - y), (1 - x, 1 - y)]

        def slot(px, py, pc):
            return out_ref.at[4 * px + 2 * py + pc]

        def copy(k, block, to, src=None):
            return pltpu.make_async_remote_copy(
                src_ref=slot(*block) if src is None else src, dst_ref=slot(*block),
                send_sem=send_sems.at[k], recv_sem=recv_sems.at[k], device_id=to, device_id_type=MESH)

        mine = pltpu.make_async_copy(x_ref, slot(*me), local_sem)
        mine.start()
        first = [copy(0, me, sibling, src=x_ref)]
        first += [copy(1 + j, me, (*chip, c), src=x_ref) for j, chip in enumerate(chips)]
        for cp in first:
            cp.start()
        passed = [copy(4 + j, (*chip, c), sibling) for j, chip in enumerate(chips)]
        for j, chip in enumerate(chips):
            copy(1 + j, (*chip, c), me).wait_recv()
            passed[j].start()
        copy(0, sibling, me).wait_recv()
        for j, chip in enumerate(chips):
            copy(4 + j, (*chip, 1 - c), me).wait_recv()
        for cp in first + passed:
            cp.wait_send()
        mine.wait()

    return pl.pallas_call(
        body, name=name, out_shape=jax.ShapeDtypeStruct((N_DEV, m, n), shard.dtype),
        in_specs=[HBM_SPEC], out_specs=HBM_SPEC,
        scratch_shapes=[pltpu.SemaphoreType.DMA((7,)), pltpu.SemaphoreType.DMA((7,)), pltpu.SemaphoreType.DMA],
    )(shard)


def exchange(slabs, name):
    _, m, n = slabs.shape

    def body(x_ref, out_ref, send_sems, recv_sems, local_sem):
        x, y, c = lax.axis_index("x"), lax.axis_index("y"), lax.axis_index("c")
        me = 4 * x + 2 * y + c
        mine = pltpu.make_async_copy(x_ref.at[me], out_ref.at[me], local_sem)
        mine.start()
        copies = []
        for k in range(1, N_DEV):
            px = 1 - x if k & 4 else x
            py = 1 - y if k & 2 else y
            pc = 1 - c if k & 1 else c
            peer = 4 * px + 2 * py + pc
            cp = pltpu.make_async_remote_copy(
                src_ref=x_ref.at[peer], dst_ref=out_ref.at[me], send_sem=send_sems.at[k - 1],
                recv_sem=recv_sems.at[k - 1], device_id=(px, py, pc), device_id_type=MESH)
            cp.start()
            copies.append(cp)
        for cp in copies:
            cp.wait_recv()
        for cp in copies:
            cp.wait_send()
        mine.wait()

    return pl.pallas_call(
        body, name=name, out_shape=jax.ShapeDtypeStruct((N_DEV, m, n), slabs.dtype),
        in_specs=[HBM_SPEC], out_specs=HBM_SPEC,
        scratch_shapes=[pltpu.SemaphoreType.DMA((7,)), pltpu.SemaphoreType.DMA((7,)), pltpu.SemaphoreType.DMA],
    )(slabs)


def sum_slots(parts, name):
    _, m, n = parts.shape
    tm = _tile(m, 256, SUBLANES)

    def body(p_ref, o_ref):
        acc = p_ref[0].astype(F32)
        for q in range(1, N_DEV):
            acc = acc + p_ref[q].astype(F32)
        o_ref[...] = acc

    return pl.pallas_call(
        body, name=name, grid=(m // tm,), in_specs=[pl.BlockSpec((N_DEV, tm, n), lambda i: (0, i, 0))],
        out_specs=pl.BlockSpec((tm, n), lambda i: (i, 0)), out_shape=jax.ShapeDtypeStruct((m, n), F32),
        compiler_params=_cp(("parallel",)),
    )(parts)


PACK_COLS = 1024
PACK_ROWS = 16


def _pack_flat(pieces, dtype):
    lead = pieces[0].shape[:-1]
    flat = jnp.concatenate([p.astype(dtype) for p in pieces], axis=-1)
    unit = PACK_COLS * PACK_ROWS
    total = -(-flat.shape[-1] // unit) * unit
    flat = jnp.pad(flat, [(0, 0)] * len(lead) + [(0, total - flat.shape[-1])])
    return flat.reshape(*lead, total // PACK_COLS, PACK_COLS)


def _unpack_flat(packed, sizes):
    lead = packed.shape[:-2]
    flat = packed.reshape(*lead, -1)
    out, off = [], 0
    for n in sizes:
        out.append(flat[..., off:off + n])
        off += n
    return out


def mod_fwd(c_all, w_mod, b_cols, name):
    nl, d, n = w_mod.shape

    def body(c_ref, w_ref, b_ref, o_ref):
        cv = c_ref[...]
        sc = (cv * _sigmoid(cv)).astype(BF16)
        o_ref[...] = jnp.dot(sc, w_ref[...].astype(BF16), preferred_element_type=F32) + b_ref[...]

    return pl.pallas_call(
        body, name=name, grid=(nl,),
        in_specs=[pl.BlockSpec((N_DEV, d), lambda l: (0, 0)), pl.BlockSpec((None, d, n), lambda l: (l, 0, 0)),
                  pl.BlockSpec((None, 1, n), lambda l: (l, 0, 0))],
        out_specs=pl.BlockSpec((None, N_DEV, n), lambda l: (l, 0, 0)),
        out_shape=jax.ShapeDtypeStruct((nl, N_DEV, n), F32), compiler_params=_cp(("parallel",)),
    )(c_all, w_mod, b_cols.reshape(nl, 1, n))


def mod_bwd(c_all, dmod_cols, name):
    nl, _, n = dmod_cols.shape
    d = c_all.shape[1]

    def body(c_ref, g_ref, o_ref):
        cv = c_ref[...]
        sc = (cv * _sigmoid(cv)).astype(BF16)
        o_ref[...] = lax.dot_general(sc, g_ref[...].astype(BF16), _DIMS["tn"], preferred_element_type=F32)

    return pl.pallas_call(
        body, name=name, grid=(nl,),
        in_specs=[pl.BlockSpec((N_DEV, d), lambda l: (0, 0)), pl.BlockSpec((None, N_DEV, n), lambda l: (l, 0, 0))],
        out_specs=pl.BlockSpec((None, d, n), lambda l: (l, 0, 0)),
        out_shape=jax.ShapeDtypeStruct((nl, d, n), F32), compiler_params=_cp(("parallel",)),
    )(c_all, dmod_cols)


def adamw(w, g, m, v, name):
    r, c = w.shape
    tr = _tile(r, 512, SUBLANES)
    c1 = 1.0 - ADAM_B1 ** ADAM_STEP
    c2 = 1.0 - ADAM_B2 ** ADAM_STEP

    def body(w_ref, g_ref, m_ref, v_ref, d_ref, nm_ref, nv_ref):
        gg = g_ref[...]
        nm = ADAM_B1 * m_ref[...] + (1.0 - ADAM_B1) * gg
        nv = ADAM_B2 * v_ref[...] + (1.0 - ADAM_B2) * (gg * gg)
        nm_ref[...] = nm
        nv_ref[...] = nv
        d_ref[...] = -ADAM_LR * ((nm / c1) / (jnp.sqrt(nv / c2) + ADAM_EPS) + ADAM_WD * w_ref[...])

    spec = pl.BlockSpec((tr, c), lambda i: (i, 0))
    sd = jax.ShapeDtypeStruct((r, c), F32)
    return pl.pallas_call(
        body, name=name, grid=(r // tr,), in_specs=[spec] * 4, out_specs=[spec] * 3, out_shape=[sd, sd, sd],
        compiler_params=_cp(("parallel",)),
    )(w, g, m, v)


WEIGHTS = ["norm_g", "w_mod", "b_mod", "sb_w_qkv", "sb_w_o", "s5_lam_re", "s5_lam_im", "s5_log_dt", "s5_b_re",
           "s5_b_im", "s5_c_re", "s5_c_im", "s5_d", "s5_w_glu", "s5_b_glu", "cv_w_pw1", "cv_b_pw1", "cv_w_dw",
           "cv_b_dw", "cv_ln_g", "cv_ln_b", "cv_w_pw2", "cv_b_pw2", "ffn_w_gate", "ffn_w_up", "ffn_w_down"]
BIG = ["w_mod", "sb_w_qkv", "sb_w_o", "s5_w_glu", "cv_w_pw1", "cv_w_pw2", "ffn_w_gate", "ffn_w_up", "ffn_w_down"]
SMALL_SHARDED = ["norm_g", "cv_b_pw1", "cv_w_dw", "cv_b_dw", "cv_ln_g", "cv_ln_b", "cv_b_pw2"]
SMALL = [n for n in WEIGHTS if n not in BIG]


def _unshard_last(part, local_shape):
    a = jnp.moveaxis(part.reshape((N_DEV,) + tuple(local_shape)), 0, -2)
    return a.reshape(tuple(local_shape[:-1]) + (N_DEV * local_shape[-1],))


def kernel(x, c, norm_g, w_mod, b_mod, sb_w_qkv, sb_w_o, s5_lam_re, s5_lam_im, s5_log_dt, s5_b_re, s5_b_im, s5_c_re, s5_c_im, s5_d, s5_w_glu, s5_b_glu, cv_w_pw1, cv_b_pw1, cv_w_dw, cv_b_dw, cv_ln_g, cv_ln_b, cv_w_pw2, cv_b_pw2, ffn_w_gate, ffn_w_up, ffn_w_down, loss_target, m_norm_g, m_w_mod, m_b_mod, m_sb_w_qkv, m_sb_w_o, m_s5_lam_re, m_s5_lam_im, m_s5_log_dt, m_s5_b_re, m_s5_b_im, m_s5_c_re, m_s5_c_im, m_s5_d, m_s5_w_glu, m_s5_b_glu, m_cv_w_pw1, m_cv_b_pw1, m_cv_w_dw, m_cv_b_dw, m_cv_ln_g, m_cv_ln_b, m_cv_w_pw2, m_cv_b_pw2, m_ffn_w_gate, m_ffn_w_up, m_ffn_w_down, v_norm_g, v_w_mod, v_b_mod, v_sb_w_qkv, v_sb_w_o, v_s5_lam_re, v_s5_lam_im, v_s5_log_dt, v_s5_b_re, v_s5_b_im, v_s5_c_re, v_s5_c_im, v_s5_d, v_s5_w_glu, v_s5_b_glu, v_cv_w_pw1, v_cv_b_pw1, v_cv_w_dw, v_cv_b_dw, v_cv_ln_g, v_cv_ln_b, v_cv_w_pw2, v_cv_b_pw2, v_ffn_w_gate, v_ffn_w_up, v_ffn_w_down):
    p = dict(locals())
    me = _me()
    s, d = x.shape[1], x.shape[2]
    depth = norm_g.shape[0]
    h = x.reshape(s, d)
    target = loss_target.reshape(s, d)

    pieces = [p[n].reshape(-1) for n in SMALL_SHARDED] + [c.reshape(-1)]
    parts = _unpack_flat(all_gather(_pack_flat(pieces, F32), "ag_small"), [q.shape[0] for q in pieces])
    full = {n: _unshard_last(part, p[n].shape) for n, part in zip(SMALL_SHARDED, parts)}
    c_all = parts[-1]

    nmod = w_mod.shape[2]
    b_cols = lax.dynamic_slice_in_dim(b_mod, me * nmod, nmod, axis=1)
    mod_cols = mod_fwd(c_all, w_mod, b_cols, "mod_fwd")
    g_mod = all_gather(mod_cols.reshape(depth * N_DEV, nmod), "ag_mod").reshape(N_DEV, depth, N_DEV, nmod)
    mod = jnp.moveaxis(lax.dynamic_index_in_dim(g_mod, me, axis=2, keepdims=False), 0, 1).reshape(depth, N_DEV * nmod)
    ng = full["norm_g"]

    def layer_pieces(l):
        kind, j = l % 3, l // 3
        if kind == 0:
            ps = [("qkv", sb_w_qkv[j], "col"), ("o", sb_w_o[j], "row")]
        elif kind == 1:
            ps = [("glu", s5_w_glu[j], "col")]
        else:
            ps = [("pw1", cv_w_pw1[j], "col"), ("pw2", cv_w_pw2[j], "row")]
        return ps + [("gate", ffn_w_gate[l], "col"), ("up", ffn_w_up[l], "col"), ("down", ffn_w_down[l], "row")]

    def gather_layer(l):
        ps = layer_pieces(l)
        got = all_gather(_pack_flat([a.reshape(-1) for _, a, _ in ps], BF16), f"ag_w{l}")
        out = {}
        for (key, a, how), part in zip(ps, _unpack_flat(got, [a.size for _, a, _ in ps])):
            r, cc = a.shape
            blk = part.reshape(N_DEV, r, cc)
            out[key] = blk.transpose(1, 0, 2).reshape(r, N_DEV * cc) if how == "col" else blk.reshape(N_DEV * r, cc)
        return out

    def scatter_layer(l, grads):
        ps = layer_pieces(l)
        slabs = []
        for key, a, how in ps:
            r, cc = a.shape
            g = grads[key]
            slabs.append(g.reshape(r, N_DEV, cc).transpose(1, 0, 2).reshape(N_DEV, r * cc) if how == "col"
                         else g.reshape(N_DEV, r * cc))
        tot = sum_slots(exchange(_pack_flat(slabs, BF16), f"rs_x{l}"), f"rs_sum{l}")
        return {key: part.reshape(a.shape)
                for (key, a, _), part in zip(ps, _unpack_flat(tot, [a.size for _, a, _ in ps]))}

    saved = []
    for l in range(depth):
        kind, j = l % 3, l // 3
        w = gather_layer(l)
        sh_m, sc_m, g_m, sh_f, sc_f, g_f = jnp.split(mod[l], 6)
        st = {"w": w, "h": h}
        if kind == 0:
            (u,) = norm_mod_fwd(h, ng[l, 0], sc_m, sh_m, [BF16], f"nm_a{l}")
            qkv = mm([(u, w["qkv"])], "nn", BF16, name=f"qkv{l}")
            o = attn_fwd(qkv, f"attn_fwd{l}")
            m = mm([(o, w["o"])], "nn", F32, name=f"attn_o{l}")
            st.update(u=u, qkv=qkv, o=o)
        elif kind == 1:
            (u,) = norm_mod_fwd(h, ng[l, 0], sc_m, sh_m, [F32], f"nm_a{l}")
            ops = s5_operands(s5_lam_re[j], s5_lam_im[j], s5_log_dt[j], s5_b_re[j], s5_b_im[j], s5_c_re[j],
                              s5_c_im[j], d)
            xr, xi, yy, gl = s5_fwd(u, *ops[:5], s5_d[j], f"s5_fwd{l}")
            p1, p2, m = mm_dual(gl, w["glu"], s5_b_glu[j], "glu", F32, F32, f"s5_glu{l}")
            st.update(u=u, ops=ops, xr=xr, xi=xi, yy=yy, gl=gl, p1=p1, p2=p2)
        else:
            (u,) = norm_mod_fwd(h, ng[l, 0], sc_m, sh_m, [BF16], f"nm_a{l}")
            p1, p2, hg = mm_dual(u, w["pw1"], full["cv_b_pw1"][j], "glu", F32, F32, f"cv_pw1{l}")
            hc, hs = conv_mid_fwd(hg, full["cv_w_dw"][j], full["cv_b_dw"][j], full["cv_ln_g"][j],
                                  full["cv_ln_b"][j], f"cv_mid{l}")
            m = mm([(hs, w["pw2"])], "nn", F32, bias=full["cv_b_pw2"][j], name=f"cv_pw2{l}")
            st.update(u=u, p1=p1, p2=p2, hg=hg, hc=hc, hs=hs)
        h2 = resid_fwd(h, m, ng[l, 1], g_m, f"res_a{l}")
        (u2,) = norm_mod_fwd(h2, ng[l, 2], sc_f, sh_f, [BF16], f"nm_f{l}")
        w_gu = jnp.concatenate([w["gate"], w["up"]], axis=1)
        f1, f2, z = mm_dual(u2, w_gu, None, "swiglu", BF16, BF16, f"ffn_up{l}")
        f = mm([(z, w["down"])], "nn", F32, name=f"ffn_down{l}")
        h = resid_fwd(h2, f, ng[l, 3], g_f, f"res_f{l}")
        st.update(m=m, h2=h2, u2=u2, f1=f1, f2=f2, z=z, f=f)
        saved.append(st)

    loss_arr, dh = loss_and_grad(h, target, "loss")
    loss = lax.psum(loss_arr[0, 0], AXES)

    nl_sb, nl_s5, nl_cv = sb_w_qkv.shape[0], s5_w_glu.shape[0], cv_w_pw1.shape[0]
    gbig = {n: [None] * p[n].shape[0] for n in BIG if n != "w_mod"}
    dng = [None] * depth
    dmod = [None] * depth
    gs5 = {n: [None] * nl_s5 for n in SMALL if n.startswith("s5_")}
    gcv = {n: [None] * nl_cv for n in SMALL if n.startswith("cv_")}
    for l in reversed(range(depth)):
        kind, j = l % 3, l // 3
        st = saved[l]
        w = st["w"]
        sh_m, sc_m, g_m, sh_f, sc_f, g_f = jnp.split(mod[l], 6)
        gw = {}
        df, s_rf = resid_bwd(dh, st["f"], ng[l, 3], g_f, BF16, f"res_f_bwd{l}")
        d1, d2 = mm_act_bwd(df, w["down"], st["f1"], st["f2"], "swiglu", f"ffn_dz{l}")
        gw["down"] = mm([(st["z"], df)], "tn", BF16, name=f"ffn_dwd{l}")
        du2 = mm([(d1, w["gate"]), (d2, w["up"])], "nt", F32, name=f"ffn_du{l}")
        gw["gate"] = mm([(st["u2"], d1)], "tn", BF16, name=f"ffn_dwg{l}")
        gw["up"] = mm([(st["u2"], d2)], "tn", BF16, name=f"ffn_dwu{l}")
        dh2, s_nf = norm_mod_bwd([du2], st["h2"], dh, ng[l, 2], sc_f, f"nm_f_bwd{l}")
        dm, s_rm = resid_bwd(dh2, st["m"], ng[l, 1], g_m, F32 if kind == 1 else BF16, f"res_a_bwd{l}")
        if kind == 0:
            do = mm([(dm, w["o"])], "nt", BF16, name=f"attn_do{l}")
            gw["o"] = mm([(st["o"], dm)], "tn", BF16, name=f"attn_dwo{l}")
            dq, dk, dv = attn_bwd(st["qkv"], do, f"attn_bwd{l}")
            wq = w["qkv"]
            dus = [mm([(dq, wq[:, :d]), (dk, wq[:, d:2 * d]), (dv, wq[:, 2 * d:])], "nt", F32, name=f"qkv_du{l}")]
            gw["qkv"] = jnp.concatenate([mm([(st["u"], t)], "tn", BF16, name=f"qkv_dw{l}_{i}")
                                         for i, t in enumerate((dq, dk, dv))], axis=1)
        elif kind == 1:
            d1, d2, cs = dual_bwd(dm, st["p1"], st["p2"], "glu", f"s5_glu_bwd{l}")
            gs5["s5_b_glu"][j] = jnp.concatenate([cs[0], cs[1]])
            wg = w["glu"]
            dgl = mm([(d1, wg[:, :d]), (d2, wg[:, d:])], "nt", F32, name=f"s5_dgl{l}")
            gw["glu"] = jnp.concatenate([mm([(st["gl"], t)], "tn", BF16, name=f"s5_dwglu{l}_{i}")
                                         for i, t in enumerate((d1, d2))], axis=1)
            ops = st["ops"]
            du, dwbr, dwbi, dwcr, dwci, da, dd = s5_bwd(dgl, st["yy"], st["u"], st["xr"], st["xi"], *ops[:4],
                                                        ops[5], s5_d[j], f"s5_bwd{l}")
            dus = [du]
            ngrp = s5_lam_re.shape[1]
            ext = lambda t: _blockdiag_extract(t, ngrp).transpose(0, 2, 1)
            _, disc_vjp = jax.vjp(_s5_discretize, s5_lam_re[j], s5_lam_im[j], s5_log_dt[j], s5_b_re[j], s5_b_im[j])
            shp = s5_lam_re[j].shape
            dlr, dli, dldt, dbr, dbi = disc_vjp((da[0].reshape(shp), da[1].reshape(shp), ext(dwbr), ext(dwbi)))
            for n, t in (("s5_lam_re", dlr), ("s5_lam_im", dli), ("s5_log_dt", dldt), ("s5_b_re", dbr),
                         ("s5_b_im", dbi), ("s5_c_re", ext(dwcr)), ("s5_c_im", -ext(dwci)), ("s5_d", dd[0])):
                gs5[n][j] = t
        else:
            dhs = mm([(dm, w["pw2"])], "nt", BF16, name=f"cv_dhs{l}")
            gw["pw2"] = mm([(st["hs"], dm)], "tn", BF16, name=f"cv_dwpw2{l}")
            dhg, dwdw, s_cv = conv_mid_bwd(dhs, st["hc"], st["hg"], full["cv_w_dw"][j], full["cv_ln_g"][j],
                                           full["cv_ln_b"][j], f"cv_mid_bwd{l}")
            d1, d2, cs = dual_bwd(dhg, st["p1"], st["p2"], "glu", f"cv_glu_bwd{l}")
            wp = w["pw1"]
            dus = [mm([(d1, wp[:, :d]), (d2, wp[:, d:])], "nt", F32, name=f"cv_du{l}")]
            gw["pw1"] = jnp.concatenate([mm([(st["u"], t)], "tn", BF16, name=f"cv_dwpw1{l}_{i}")
                                         for i, t in enumerate((d1, d2))], axis=1)
            for n, t in (("cv_b_pw1", jnp.concatenate([cs[0], cs[1]])), ("cv_w_dw", dwdw[:cv_w_dw.shape[1]]),
                         ("cv_b_dw", s_cv[0]), ("cv_ln_g", s_cv[1]), ("cv_ln_b", s_cv[2]), ("cv_b_pw2", s_rm[2])):
                gcv[n][j] = t
        dh, s_nm = norm_mod_bwd(dus, st["h"], dh2, ng[l, 0], sc_m, f"nm_a_bwd{l}")
        dng[l] = jnp.stack([s_nm[2], s_rm[1], s_nf[2], s_rf[1]])
        dmod[l] = jnp.concatenate([s_nm[0], s_nm[1], s_rm[0], s_nf[0], s_nf[1], s_rf[0]])
        got = scatter_layer(l, gw)
        for key, name in (("qkv", "sb_w_qkv"), ("o", "sb_w_o"), ("glu", "s5_w_glu"), ("pw1", "cv_w_pw1"),
                          ("pw2", "cv_w_pw2")):
            if key in got:
                gbig[name][j] = got[key]
        for key, name in (("gate", "ffn_w_gate"), ("up", "ffn_w_up"), ("down", "ffn_w_down")):
            gbig[name][l] = got[key]

    local = {"norm_g": jnp.stack(dng), "b_mod": jnp.stack(dmod)}
    local.update({n: jnp.stack(t) for n, t in gs5.items()})
    local.update({n: jnp.stack(t) for n, t in gcv.items()})
    pieces = [local[n].reshape(-1) for n in SMALL]
    sizes = [q.shape[0] for q in pieces]
    gathered = all_gather(_pack_flat(pieces, F32), "ag_grads")
    sums = _unpack_flat(sum_slots(gathered, "sum_grads"), sizes)
    grads = {}
    for n, t in zip(SMALL, sums):
        t = t.reshape(local[n].shape)
        if n in SMALL_SHARDED:
            nsh = p[n].shape[-1]
            t = lax.dynamic_slice_in_dim(t, me * nsh, nsh, axis=t.ndim - 1)
        grads[n] = t.reshape(p[n].shape)
    dmod_all = _unpack_flat(gathered, sizes)[SMALL.index("b_mod")].reshape(N_DEV, depth, N_DEV * nmod)
    dmod_cols = jnp.moveaxis(lax.dynamic_slice_in_dim(dmod_all, me * nmod, nmod, axis=2), 0, 1)
    grads["w_mod"] = mod_bwd(c_all, dmod_cols, "mod_bwd")
    for n in gbig:
        grads[n] = jnp.stack(gbig[n])

    delta, new_m, new_v = {}, {}, {}
    for n in BIG:
        shp = p[n].shape
        two = lambda t: t.reshape(-1, shp[-1])
        delta[n], new_m[n], new_v[n] = (t.reshape(shp) for t in
                                        adamw(two(p[n]), two(grads[n]), two(p["m_" + n]), two(p["v_" + n]), f"adamw_{n}"))
    sizes = [p[n].size for n in SMALL]
    packs = [_pack_flat([t[n].reshape(-1) for n in SMALL], F32)
             for t in (p, grads, {n: p["m_" + n] for n in SMALL}, {n: p["v_" + n] for n in SMALL})]
    for res, out in zip(adamw(*packs, "adamw_small"), (delta, new_m, new_v)):
        for n, t in zip(SMALL, _unpack_flat(res, sizes)):
            out[n] = t.reshape(p[n].shape)

    return (loss, dh.reshape(x.shape), *[grads[n] for n in WEIGHTS], *[delta[n] for n in WEIGHTS],
            *[new_m[n] for n in WEIGHTS], *[new_v[n] for n in WEIGHTS])
```

```python
import functools
import math

import jax
import jax.numpy as jnp
from jax import lax
from jax.experimental import pallas as pl
from jax.experimental.pallas import tpu as pltpu

F32, BF16 = jnp.float32, jnp.bfloat16
N_DEV = 8
AXES = ("x", "y", "c")
EPS = 1e-6
HEAD_DIM = 64
LANES = 128
SUBLANES = 8
VMEM_LIMIT = 56 * 1024 * 1024
S5_GROUP = 16
ADAM_LR, ADAM_B1, ADAM_B2, ADAM_EPS, ADAM_WD, ADAM_STEP = 0.001, 0.9, 0.999, 1e-08, 0.01, 10
NEG_CUTOFF = -104.0


def _cp(sem):
    return pltpu.CompilerParams(dimension_semantics=sem, vmem_limit_bytes=VMEM_LIMIT)


def _tile(n, cap, mult=LANES):
    best = None
    for t in range(mult, min(n, cap) + 1, mult):
        if n % t == 0:
            best = t
    return best if best is not None else n


def _sigmoid(x):
    return 1.0 / (1.0 + jnp.exp(-x))


_DIMS = {"nn": (((1,), (0,)), ((), ())), "nt": (((1,), (1,)), ((), ())), "tn": (((0,), (0,)), ((), ()))}


CAPS_FULL_K = (1024, 1536, 4096)
CAPS_PAIRS = (512, 1024, 1408)
CAPS_TN = (4096, 4096, 512)


def _host_call(body, name, grid, in_specs, out_specs, out_shape, scratch, sem, args, car):
    if car is None:
        return pl.pallas_call(body, name=name, grid=grid, in_specs=in_specs, out_specs=out_specs, out_shape=out_shape,
                              scratch_shapes=scratch, compiler_params=_cp(sem))(*args), None
    n_in, n_out = len(in_specs), len(out_specs)
    cspecs, cshapes, cscratch = car.call_args()

    def wrapped(*refs):
        host, crefs = _carry_refs(refs, n_in, n_out, car)
        after = _carry_steps(car, crefs, grid)
        body(*host)
        after()

    res = pl.pallas_call(
        wrapped, name=name, grid=grid, in_specs=in_specs + cspecs, out_specs=out_specs + cspecs,
        out_shape=out_shape + cshapes, scratch_shapes=scratch + cscratch,
        compiler_params=_cp(("arbitrary",) * len(grid)))(*args, *car.operands)
    return res[:n_out], res[n_out:]


def mm(pairs, mode, out_dtype=F32, bias=None, name="mm", caps=None, car=None):
    a0, b0 = pairs[0]
    if mode == "nn":
        (m, k), n = a0.shape, b0.shape[1]
    elif mode == "nt":
        (m, k), n = a0.shape, b0.shape[0]
    else:
        (k, m), n = a0.shape, b0.shape[1]
    if caps is None:
        caps = CAPS_TN if mode == "tn" else (CAPS_FULL_K if len(pairs) == 1 or k <= 1024 else CAPS_PAIRS)
    tm, tn, tk = _tile(m, caps[0]), _tile(n, caps[1]), _tile(k, caps[2])
    nk = k // tk
    npairs = len(pairs)
    dims = _DIMS[mode]

    def body(*refs):
        ins = refs[:2 * npairs]
        bias_ref = refs[2 * npairs] if bias is not None else None
        o_ref, acc = refs[-2], refs[-1]
        kk = pl.program_id(2)
        part = None
        for p in range(npairs):
            d = lax.dot_general(ins[2 * p][...].astype(BF16), ins[2 * p + 1][...].astype(BF16), dims,
                                preferred_element_type=F32)
            part = d if part is None else part + d

        def finish(r):
            if bias_ref is not None:
                r = r + bias_ref[...]
            o_ref[...] = r.astype(o_ref.dtype)

        if nk == 1:
            finish(part)
        else:
            @pl.when(kk == 0)
            def _():
                acc[...] = part

            @pl.when(kk > 0)
            def _():
                acc[...] += part

            @pl.when(kk == nk - 1)
            def _():
                finish(acc[...])

    if mode == "nn":
        sa, sb = pl.BlockSpec((tm, tk), lambda i, j, kk: (i, kk)), pl.BlockSpec((tk, tn), lambda i, j, kk: (kk, j))
    elif mode == "nt":
        sa, sb = pl.BlockSpec((tm, tk), lambda i, j, kk: (i, kk)), pl.BlockSpec((tn, tk), lambda i, j, kk: (j, kk))
    else:
        sa, sb = pl.BlockSpec((tk, tm), lambda i, j, kk: (kk, i)), pl.BlockSpec((tk, tn), lambda i, j, kk: (kk, j))
    in_specs, args = [], []
    for a, b in pairs:
        in_specs += [sa, sb]
        args += [a, b]
    if bias is not None:
        in_specs.append(pl.BlockSpec((1, tn), lambda i, j, kk: (0, j)))
        args.append(bias.reshape(1, n).astype(F32))
    outs, carried = _host_call(
        body, name, (m // tm, n // tn, nk), in_specs, [pl.BlockSpec((tm, tn), lambda i, j, kk: (i, j))],
        [jax.ShapeDtypeStruct((m, n), out_dtype)], [pltpu.VMEM((tm, tn) if nk > 1 else (SUBLANES, LANES), F32)],
        ("parallel", "parallel", "arbitrary"), args, car)
    return outs[0] if car is None else (outs[0], carried)


def _act_fwd(kind, p1, p2):
    if kind == "swiglu":
        return p1 * _sigmoid(p1) * p2
    return p1 * _sigmoid(p2)


def _act_bwd(kind, d, p1, p2):
    if kind == "swiglu":
        s = _sigmoid(p1)
        return d * p2 * s * (1.0 + p1 * (1.0 - s)), d * (p1 * s)
    s = _sigmoid(p2)
    return d * s, d * p1 * s * (1.0 - s)


def mm_dual(a, w, bias, kind, pre_dtype, act_dtype, name, car=None):
    m, k = a.shape
    n = w.shape[1] // 2
    tm, tn = _tile(m, 512), _tile(n, 1536)
    nb = n // tn

    def body(*refs):
        a_ref, w1_ref, w2_ref = refs[:3]
        p1_ref, p2_ref, act_ref = refs[-3:]
        av = a_ref[...].astype(BF16)
        p1 = jnp.dot(av, w1_ref[...].astype(BF16), preferred_element_type=F32)
        p2 = jnp.dot(av, w2_ref[...].astype(BF16), preferred_element_type=F32)
        if bias is not None:
            p1 = p1 + refs[3][...]
            p2 = p2 + refs[4][...]
        p1_ref[...] = p1.astype(p1_ref.dtype)
        p2_ref[...] = p2.astype(p2_ref.dtype)
        act_ref[...] = _act_fwd(kind, p1, p2).astype(act_ref.dtype)

    in_specs = [pl.BlockSpec((tm, k), lambda i, j: (i, 0)), pl.BlockSpec((k, tn), lambda i, j: (0, j)),
                pl.BlockSpec((k, tn), lambda i, j: (0, j + nb))]
    args = [a, w, w]
    if bias is not None:
        b2 = bias.reshape(1, 2 * n).astype(F32)
        in_specs += [pl.BlockSpec((1, tn), lambda i, j: (0, j)), pl.BlockSpec((1, tn), lambda i, j: (0, j + nb))]
        args += [b2, b2]
    ospec = pl.BlockSpec((tm, tn), lambda i, j: (i, j))
    outs, carried = _host_call(
        body, name, (m // tm, nb), in_specs, [ospec, ospec, ospec],
        [jax.ShapeDtypeStruct((m, n), pre_dtype), jax.ShapeDtypeStruct((m, n), pre_dtype),
         jax.ShapeDtypeStruct((m, n), act_dtype)], [], ("parallel", "parallel"), args, car)
    return outs if car is None else (outs, carried)


def mm_act_bwd(dy, w, p1, p2, kind, name, car=None):
    m, k = dy.shape
    n = w.shape[0]
    tm, tn = _tile(m, 512), _tile(n, 1536)

    def body(dy_ref, w_ref, p1_ref, p2_ref, d1_ref, d2_ref):
        dact = lax.dot_general(dy_ref[...].astype(BF16), w_ref[...].astype(BF16), _DIMS["nt"],
                               preferred_element_type=F32)
        d1, d2 = _act_bwd(kind, dact, p1_ref[...].astype(F32), p2_ref[...].astype(F32))
        d1_ref[...] = d1.astype(BF16)
        d2_ref[...] = d2.astype(BF16)

    spec = pl.BlockSpec((tm, tn), lambda i, j: (i, j))
    sd = jax.ShapeDtypeStruct((m, n), BF16)
    outs, carried = _host_call(
        body, name, (m // tm, n // tn),
        [pl.BlockSpec((tm, k), lambda i, j: (i, 0)), pl.BlockSpec((tn, k), lambda i, j: (j, 0)), spec, spec],
        [spec, spec], [sd, sd], [], ("parallel", "parallel"), [dy, w, p1, p2], car)
    return outs if car is None else (outs, carried)


def dual_bwd(dact, p1, p2, kind, name):
    m, n = dact.shape
    tm, tn = _tile(m, 512), _tile(n, 512)

    def body(d_ref, p1_ref, p2_ref, d1_ref, d2_ref, s_ref):
        d1, d2 = _act_bwd(kind, d_ref[...].astype(F32), p1_ref[...].astype(F32), p2_ref[...].astype(F32))
        d1_ref[...] = d1.astype(BF16)
        d2_ref[...] = d2.astype(BF16)

        @pl.when(pl.program_id(1) == 0)
        def _():
            s_ref[...] = jnp.zeros_like(s_ref)

        s_ref[0:1, :] += jnp.sum(d1, axis=0, keepdims=True)
        s_ref[1:2, :] += jnp.sum(d2, axis=0, keepdims=True)

    spec = pl.BlockSpec((tm, tn), lambda j, i: (i, j))
    return pl.pallas_call(
        body, name=name, grid=(n // tn, m // tm), in_specs=[spec, spec, spec],
        out_specs=[spec, spec, pl.BlockSpec((SUBLANES, tn), lambda j, i: (0, j))],
        out_shape=[jax.ShapeDtypeStruct((m, n), BF16), jax.ShapeDtypeStruct((m, n), BF16),
                   jax.ShapeDtypeStruct((SUBLANES, n), F32)],
        compiler_params=_cp(("parallel", "arbitrary")),
    )(dact, p1, p2)


def _rms(x):
    r = lax.rsqrt(jnp.mean(x * x, axis=-1, keepdims=True) + EPS)
    return x * r, r


def _vec8(*rows):
    d = rows[0].shape[-1]
    out = jnp.zeros((SUBLANES, d), F32)
    for i, r in enumerate(rows):
        out = out.at[i].set(r.reshape(d).astype(F32))
    return out


def norm_mod_fwd(h, g, scale, shift, out_dtypes, name):
    s, d = h.shape
    ts = _tile(s, 512, SUBLANES)
    vec = _vec8(g, 1.0 + scale, shift)

    def body(h_ref, v_ref, *outs):
        hh, _ = _rms(h_ref[...])
        u = hh * v_ref[0:1, :] * v_ref[1:2, :] + v_ref[2:3, :]
        for o in outs:
            o[...] = u.astype(o.dtype)

    spec = pl.BlockSpec((ts, d), lambda i: (i, 0))
    return pl.pallas_call(
        body, name=name, grid=(s // ts,), in_specs=[spec, pl.BlockSpec((SUBLANES, d), lambda i: (0, 0))],
        out_specs=[spec] * len(out_dtypes), out_shape=[jax.ShapeDtypeStruct((s, d), t) for t in out_dtypes],
        compiler_params=_cp(("parallel",)),
    )(h, vec)


def norm_mod_bwd(dus, h, dh_in, g, scale, name):
    s, d = h.shape
    ts = _tile(s, 256, SUBLANES)
    vec = _vec8(g, 1.0 + scale)
    nd = len(dus)

    def body(*refs):
        du = refs[0][...].astype(F32)
        for r in refs[1:nd]:
            du = du + r[...].astype(F32)
        h_ref, dhi_ref, v_ref, dh_ref, s_ref = refs[nd:]
        hh, r = _rms(h_ref[...])
        gg, sc1 = v_ref[0:1, :], v_ref[1:2, :]
        dn = du * sc1
        dhh = dn * gg
        dh = r * (dhh - hh * jnp.mean(dhh * hh, axis=-1, keepdims=True))
        dh_ref[...] = dhi_ref[...] + dh

        @pl.when(pl.program_id(0) == 0)
        def _():
            s_ref[...] = jnp.zeros_like(s_ref)

        s_ref[0:1, :] += jnp.sum(du, axis=0, keepdims=True)
        s_ref[1:2, :] += jnp.sum(du * (hh * gg), axis=0, keepdims=True)
        s_ref[2:3, :] += jnp.sum(dn * hh, axis=0, keepdims=True)

    spec = pl.BlockSpec((ts, d), lambda i: (i, 0))
    vspec = pl.BlockSpec((SUBLANES, d), lambda i: (0, 0))
    return pl.pallas_call(
        body, name=name, grid=(s // ts,), in_specs=[spec] * (nd + 2) + [vspec], out_specs=[spec, vspec],
        out_shape=[jax.ShapeDtypeStruct((s, d), F32), jax.ShapeDtypeStruct((SUBLANES, d), F32)],
        compiler_params=_cp(("arbitrary",)),
    )(*dus, h, dh_in, vec)


def resid_fwd(h, m, g, gate, name):
    s, d = h.shape
    ts = _tile(s, 512, SUBLANES)
    vec = _vec8(g, gate)

    def body(h_ref, m_ref, v_ref, o_ref):
        mh, _ = _rms(m_ref[...])
        o_ref[...] = h_ref[...] + v_ref[1:2, :] * (mh * v_ref[0:1, :])

    spec = pl.BlockSpec((ts, d), lambda i: (i, 0))
    return pl.pallas_call(
        body, name=name, grid=(s // ts,), in_specs=[spec, spec, pl.BlockSpec((SUBLANES, d), lambda i: (0, 0))],
        out_specs=spec, out_shape=jax.ShapeDtypeStruct((s, d), F32), compiler_params=_cp(("parallel",)),
    )(h, m, vec)


def resid_bwd(dh2, m, g, gate, out_dtype, name):
    s, d = m.shape
    ts = _tile(s, 256, SUBLANES)
    vec = _vec8(g, gate)

    def body(d_ref, m_ref, v_ref, dm_ref, s_ref):
        dh = d_ref[...]
        mh, r = _rms(m_ref[...])
        gg, gt = v_ref[0:1, :], v_ref[1:2, :]
        dmh = dh * (gt * gg)
        dm = r * (dmh - mh * jnp.mean(dmh * mh, axis=-1, keepdims=True))
        dm_ref[...] = dm.astype(dm_ref.dtype)

        @pl.when(pl.program_id(0) == 0)
        def _():
            s_ref[...] = jnp.zeros_like(s_ref)

        s_ref[0:1, :] += jnp.sum(dh * (mh * gg), axis=0, keepdims=True)
        s_ref[1:2, :] += jnp.sum(dh * gt * mh, axis=0, keepdims=True)
        s_ref[2:3, :] += jnp.sum(dm, axis=0, keepdims=True)

    spec = pl.BlockSpec((ts, d), lambda i: (i, 0))
    vspec = pl.BlockSpec((SUBLANES, d), lambda i: (0, 0))
    return pl.pallas_call(
        body, name=name, grid=(s // ts,), in_specs=[spec, spec, vspec], out_specs=[spec, vspec],
        out_shape=[jax.ShapeDtypeStruct((s, d), out_dtype), jax.ShapeDtypeStruct((SUBLANES, d), F32)],
        compiler_params=_cp(("arbitrary",)),
    )(dh2, m, vec)


def loss_and_grad(h, target, name):
    s, d = h.shape
    ts = _tile(s, 512, SUBLANES)

    def body(h_ref, t_ref, l_ref, dy_ref):
        e = h_ref[...] - t_ref[...]
        dy_ref[...] = e * (1.0 / d)

        @pl.when(pl.program_id(0) == 0)
        def _():
            l_ref[...] = jnp.zeros_like(l_ref)

        l_ref[...] += (0.5 / d) * jnp.sum(e * e)

    spec = pl.BlockSpec((ts, d), lambda i: (i, 0))
    lspec = pl.BlockSpec((SUBLANES, LANES), lambda i: (0, 0))
    return pl.pallas_call(
        body, name=name, grid=(s // ts,), in_specs=[spec, spec], out_specs=[lspec, spec],
        out_shape=[jax.ShapeDtypeStruct((SUBLANES, LANES), F32), jax.ShapeDtypeStruct((s, d), F32)],
        compiler_params=_cp(("arbitrary",)),
    )(h, target)


TK = 128
N_PAIR_HEADS = LANES // HEAD_DIM
TQ_FWD, SLAB_FWD = 256, 4
TQ_BWD, SLAB_BWD = 128, 3


def _cum_matrices():
    a = jnp.arange(TK)
    ones = jnp.ones((TK, TK), F32)
    suffix = (a[:, None] > a[None, :]).astype(F32)
    prefix = (a[:, None] < a[None, :]).astype(F32)
    mk = lambda u: jnp.tile(jnp.concatenate([u, ones], axis=1), (2, 1)).astype(BF16)
    return mk(suffix), mk(prefix)


def _split_dot(x, cum):
    hi = x.astype(BF16)
    lo = (x - hi.astype(F32)).astype(BF16)
    return jnp.dot(jnp.concatenate([hi, lo], axis=1), cum, preferred_element_type=F32)


def _sb_slab(qs, k_slab, cum, nb, r, mask):
    m = qs.shape[0]
    z = lax.dot_general(qs, k_slab, _DIMS["nt"], preferred_element_type=F32)
    lb = jnp.minimum(z, 0.0) - jnp.log(1.0 + jnp.exp(-jnp.abs(z)))
    lk = lb - z
    if mask is not None:
        lk = jnp.where(mask, lk, 0.0)
    t = _split_dot(jnp.concatenate([lk[:, b * TK:(b + 1) * TK] for b in range(nb)], axis=0), cum)
    cs = [None] * nb
    for b in reversed(range(nb)):
        tb = t[b * m:(b + 1) * m]
        cs[b] = tb[:, :TK] + r
        r = r + tb[:, TK:]
    a = jnp.exp(lb + jnp.concatenate(cs, axis=1))
    if mask is not None:
        a = jnp.where(mask, a, 0.0)
    return lb, a, r


def _stack_heads(x, heads):
    return jnp.concatenate([jnp.where(hm, x, jnp.zeros_like(x)) for hm in heads], axis=0)


def _unstack_heads(xs, heads, tq):
    return jnp.where(heads[0], xs[0:tq], xs[tq:2 * tq])


def _head_masks(rows):
    lane = lax.broadcasted_iota(jnp.int32, (rows, LANES), 1)
    return [(lane >= hh * HEAD_DIM) & (lane < (hh + 1) * HEAD_DIM) for hh in range(N_PAIR_HEADS)]


def _slab_geometry(i, tq, nb):
    m = N_PAIR_HEADS * tq
    sb = jnp.maximum((i + 1) * (tq // TK) - nb, 0)
    rowq = lax.broadcasted_iota(jnp.int32, (m, nb * TK), 0) & (tq - 1)
    col = lax.broadcasted_iota(jnp.int32, (m, nb * TK), 1)
    mask = (col - rowq) < (i * tq - sb * TK)
    return sb, mask


def attn_fwd(qkv, name):
    s, d3 = qkv.shape
    d = d3 // 3
    npair = d // LANES
    tq = min(TQ_FWD, s)
    nb = SLAB_FWD
    m = N_PAIR_HEADS * tq
    scale = HEAD_DIM ** -0.5
    cum_s, _ = _cum_matrices()

    def body(q_ref, k_ref, v_ref, c_ref, o_ref):
        i = pl.program_id(1)
        cum = c_ref[...]
        heads = _head_masks(tq)
        qs = _stack_heads((q_ref[...].astype(F32) * scale).astype(BF16), heads)
        sb, mask = _slab_geometry(i, tq, nb)
        off = pl.multiple_of(sb * TK, TK)
        _, w, r = _sb_slab(qs, k_ref[pl.ds(off, nb * TK), :], cum, nb, jnp.zeros((m, TK), F32), mask)
        acc = jnp.dot(w.astype(BF16), v_ref[pl.ds(off, nb * TK), :], preferred_element_type=F32)

        def cond(c):
            return jnp.logical_and(c[0] >= 0, jnp.max(c[1]) > NEG_CUTOFF)

        def step(c):
            j, r, acc = c
            off = pl.multiple_of(j * TK, TK)
            _, w, r = _sb_slab(qs, k_ref[pl.ds(off, TK), :], cum, 1, r, None)
            return j - 1, r, acc + jnp.dot(w.astype(BF16), v_ref[pl.ds(off, TK), :], preferred_element_type=F32)

        _, _, acc = lax.while_loop(cond, step, (sb - 1, r, acc))
        o_ref[...] = _unstack_heads(acc, heads, tq).astype(o_ref.dtype)

    return pl.pallas_call(
        body, name=name, grid=(npair, s // tq),
        in_specs=[pl.BlockSpec((tq, LANES), lambda p, i: (i, p)),
                  pl.BlockSpec((s, LANES), lambda p, i: (0, npair + p)),
                  pl.BlockSpec((s, LANES), lambda p, i: (0, 2 * npair + p)),
                  pl.BlockSpec((2 * TK, 2 * TK), lambda p, i: (0, 0))],
        out_specs=pl.BlockSpec((tq, LANES), lambda p, i: (i, p)),
        out_shape=jax.ShapeDtypeStruct((s, d), BF16),
        compiler_params=_cp(("parallel", "arbitrary")),
    )(qkv, qkv, qkv, cum_s)


def attn_bwd(qkv, do, name):
    s, d3 = qkv.shape
    d = d3 // 3
    npair = d // LANES
    tq = min(TQ_BWD, s)
    nb = SLAB_BWD
    nq = s // tq
    m = N_PAIR_HEADS * tq
    scale = HEAD_DIM ** -0.5
    cum_s, cum_p = _cum_matrices()

    def body(q_ref, k_ref, v_ref, do_ref, cs_ref, cp_ref, dq_ref, dk_ref, dv_ref, dk_acc, dv_acc, e_scr, b_scr):
        i = pl.program_id(1)

        @pl.when(i == 0)
        def _():
            dk_acc[...] = jnp.zeros_like(dk_acc)
            dv_acc[...] = jnp.zeros_like(dv_acc)

        cum_suf = cs_ref[...]
        cum_pre = cp_ref[...]
        heads = _head_masks(tq)
        qs = _stack_heads((q_ref[...].astype(F32) * scale).astype(BF16), heads)
        dos = _stack_heads(do_ref[...], heads)

        def left(off, n, r, mask):
            rows = pl.ds(off, n * TK)
            lb, a, r = _sb_slab(qs, k_ref[rows, :], cum_suf, n, r, mask)
            da = lax.dot_general(dos, v_ref[rows, :], _DIMS["nt"], preferred_element_type=F32)
            dv_acc[rows, :] += lax.dot_general(a.astype(BF16), dos, _DIMS["tn"], preferred_element_type=F32)
            return da * a, jnp.exp(lb), r

        def right(off, n, e, beta, pe, mask):
            rows = pl.ds(off, n * TK)
            t = _split_dot(jnp.concatenate([e[:, b * TK:(b + 1) * TK] for b in range(n)], axis=0), cum_pre)
            ps = [None] * n
            for b in range(n):
                tb = t[b * m:(b + 1) * m]
                ps[b] = tb[:, :TK] + pe
                pe = pe + tb[:, TK:]
            dz = e * (1.0 - beta) - beta * jnp.concatenate(ps, axis=1)
            if mask is not None:
                dz = jnp.where(mask, dz, 0.0)
            dzb = dz.astype(BF16)
            dk_acc[rows, :] += lax.dot_general(dzb, qs, _DIMS["tn"], preferred_element_type=F32)
            return pe, jnp.dot(dzb, k_ref[rows, :], preferred_element_type=F32)

        sb, mask = _slab_geometry(i, tq, nb)
        off0 = pl.multiple_of(sb * TK, TK)
        e0, beta0, r = left(off0, nb, jnp.zeros((m, TK), F32), mask)

        def cond(c):
            return jnp.logical_and(c[0] >= 0, jnp.max(c[1]) > NEG_CUTOFF)

        def tail_left(c):
            j, r = c
            e_scr[j], b_scr[j], r = left(pl.multiple_of(j * TK, TK), 1, r, None)
            return j - 1, r

        jend, _ = lax.while_loop(cond, tail_left, (sb - 1, r))

        def tail_right(j, c):
            pe, dq = c
            pe, dqj = right(pl.multiple_of(j * TK, TK), 1, e_scr[j], b_scr[j], pe, None)
            return pe, dq + dqj

        pe, dq = lax.fori_loop(jend + 1, sb, tail_right, (jnp.zeros((m, TK), F32), jnp.zeros((m, LANES), F32)))
        _, dq0 = right(off0, nb, e0, beta0, pe, mask)
        dq_ref[...] = (_unstack_heads(dq + dq0, heads, tq) * scale).astype(dq_ref.dtype)

        @pl.when(i == nq - 1)
        def _():
            dk_ref[...] = dk_acc[...].astype(dk_ref.dtype)
            dv_ref[...] = dv_acc[...].astype(dv_ref.dtype)

    qspec = pl.BlockSpec((tq, LANES), lambda p, i: (i, p))
    full = lambda base: pl.BlockSpec((s, LANES), lambda p, i: (0, base + p))
    cspec = pl.BlockSpec((2 * TK, 2 * TK), lambda p, i: (0, 0))
    sd = jax.ShapeDtypeStruct((s, d), BF16)
    return pl.pallas_call(
        body, name=name, grid=(npair, nq),
        in_specs=[qspec, full(npair), full(2 * npair), qspec, cspec, cspec],
        out_specs=[qspec, full(0), full(0)], out_shape=[sd, sd, sd],
        scratch_shapes=[pltpu.VMEM((s, LANES), F32), pltpu.VMEM((s, LANES), F32),
                        pltpu.VMEM((s // TK, m, TK), F32), pltpu.VMEM((s // TK, m, TK), F32)],
        compiler_params=_cp(("parallel", "arbitrary")),
    )(qkv, qkv, qkv, do, cum_s, cum_p)


HALO = 32
CONV_ROWS = 128


def _ln_swish(hc, g, b):
    mu = jnp.mean(hc, axis=-1, keepdims=True)
    xc = hc - mu
    rstd = lax.rsqrt(jnp.mean(xc * xc, axis=-1, keepdims=True) + EPS)
    xh = xc * rstd
    hn = xh * g + b
    return xh, rstd, hn


def conv_mid_fwd(x, w_dw, b_dw, ln_g, ln_b, name):
    s, d = x.shape
    width = w_dw.shape[0]
    ts = _tile(s, 256, CONV_ROWS)
    base = HALO - (width - 1)
    wpad = jnp.zeros((HALO, d), F32).at[:width].set(w_dw.astype(F32))
    vec = _vec8(b_dw, ln_g, ln_b)

    def body(x_ref, w_ref, v_ref, hc_ref, hs_ref, win):
        i = pl.program_id(0)

        @pl.when(i == 0)
        def _():
            win[0:HALO, :] = jnp.zeros((HALO, d), F32)

        @pl.when(i > 0)
        def _():
            win[0:HALO, :] = win[ts:ts + HALO, :]

        win[HALO:HALO + ts, :] = x_ref[...]
        for rc in range(ts // CONV_ROWS):
            for lc in range(d // LANES):
                cols = slice(lc * LANES, (lc + 1) * LANES)
                acc = jnp.broadcast_to(v_ref[0:1, cols], (CONV_ROWS, LANES))
                for k in range(width):
                    acc = acc + w_ref[k:k + 1, cols] * win[pl.ds(rc * CONV_ROWS + base + k, CONV_ROWS), cols]
                hc_ref[rc * CONV_ROWS:(rc + 1) * CONV_ROWS, cols] = acc
        _, _, hn = _ln_swish(hc_ref[...], v_ref[1:2, :], v_ref[2:3, :])
        hs_ref[...] = (hn * _sigmoid(hn)).astype(hs_ref.dtype)

    spec = pl.BlockSpec((ts, d), lambda i: (i, 0))
    return pl.pallas_call(
        body, name=name, grid=(s // ts,),
        in_specs=[spec, pl.BlockSpec((HALO, d), lambda i: (0, 0)), pl.BlockSpec((SUBLANES, d), lambda i: (0, 0))],
        out_specs=[spec, spec], out_shape=[jax.ShapeDtypeStruct((s, d), F32), jax.ShapeDtypeStruct((s, d), BF16)],
        scratch_shapes=[pltpu.VMEM((ts + HALO, d), F32)],
        compiler_params=_cp(("arbitrary",)),
    )(x, wpad, vec)


def conv_mid_bwd(dhs, hc, x, w_dw, ln_g, ln_b, name):
    s, d = x.shape
    width = w_dw.shape[0]
    ts = _tile(s, 256, CONV_ROWS)
    nt = s // ts
    base = HALO - (width - 1)
    wpad = jnp.zeros((HALO, d), F32).at[:width].set(w_dw.astype(F32))
    vec = _vec8(ln_g, ln_b)

    def body(dhs_ref, hc_ref, x_ref, xh_ref, w_ref, v_ref, dx_ref, dw_ref, s_ref, dwin, xwin):
        i = pl.program_id(0)

        @pl.when(i == 0)
        def _():
            dwin[ts:ts + HALO, :] = jnp.zeros((HALO, d), F32)
            dw_ref[...] = jnp.zeros_like(dw_ref)
            s_ref[...] = jnp.zeros_like(s_ref)

        @pl.when(i > 0)
        def _():
            dwin[ts:ts + HALO, :] = dwin[0:HALO, :]

        @pl.when(i == nt - 1)
        def _():
            xwin[0:HALO, :] = jnp.zeros((HALO, d), F32)

        @pl.when(i < nt - 1)
        def _():
            xwin[0:HALO, :] = xh_ref[...]

        xwin[HALO:HALO + ts, :] = x_ref[...]
        g = v_ref[0:1, :]
        xh, rstd, hn = _ln_swish(hc_ref[...], g, v_ref[1:2, :])
        sig = _sigmoid(hn)
        dhn = dhs_ref[...].astype(F32) * (sig * (1.0 + hn * (1.0 - sig)))
        dxh = dhn * g
        dhc = rstd * (dxh - jnp.mean(dxh, axis=-1, keepdims=True) - xh * jnp.mean(dxh * xh, axis=-1, keepdims=True))
        dwin[0:ts, :] = dhc
        s_ref[0:1, :] += jnp.sum(dhc, axis=0, keepdims=True)
        s_ref[1:2, :] += jnp.sum(dhn * xh, axis=0, keepdims=True)
        s_ref[2:3, :] += jnp.sum(dhn, axis=0, keepdims=True)
        for rc in range(ts // CONV_ROWS):
            for lc in range(d // LANES):
                cols = slice(lc * LANES, (lc + 1) * LANES)
                r0 = rc * CONV_ROWS
                dch = dwin[r0:r0 + CONV_ROWS, cols]
                acc = jnp.zeros((CONV_ROWS, LANES), F32)
                for k in range(width):
                    acc = acc + w_ref[k:k + 1, cols] * dwin[pl.ds(r0 + (width - 1) - k, CONV_ROWS), cols]
                    dw_ref[k:k + 1, cols] += jnp.sum(dch * xwin[pl.ds(r0 + base + k, CONV_ROWS), cols], axis=0,
                                                     keepdims=True)
                dx_ref[r0:r0 + CONV_ROWS, cols] = acc

    rev = pl.BlockSpec((ts, d), lambda i: (nt - 1 - i, 0))
    halo = pl.BlockSpec((HALO, d), lambda i: (jnp.maximum((nt - 1 - i) * (ts // HALO) - 1, 0), 0))
    vspec = pl.BlockSpec((SUBLANES, d), lambda i: (0, 0))
    wspec = pl.BlockSpec((HALO, d), lambda i: (0, 0))
    return pl.pallas_call(
        body, name=name, grid=(nt,), in_specs=[rev, rev, rev, halo, wspec, vspec],
        out_specs=[rev, wspec, vspec],
        out_shape=[jax.ShapeDtypeStruct((s, d), F32), jax.ShapeDtypeStruct((HALO, d), F32),
                   jax.ShapeDtypeStruct((SUBLANES, d), F32)],
        scratch_shapes=[pltpu.VMEM((ts + HALO, d), F32), pltpu.VMEM((ts + HALO, d), F32)],
        compiler_params=_cp(("arbitrary",)),
    )(dhs, hc, x, x, wpad, vec)


S5_KB = 256
SCAN_LANES = 256
GELU_C = math.sqrt(2.0 / math.pi)
GELU_A = 0.044715


def _gelu(x):
    return 0.5 * x * (1.0 + jnp.tanh(GELU_C * (x + GELU_A * x * x * x)))


def _gelu_grad(x):
    th = jnp.tanh(GELU_C * (x + GELU_A * x * x * x))
    return 0.5 * (1.0 + th) + 0.5 * x * (1.0 - th * th) * GELU_C * (1.0 + 3.0 * GELU_A * x * x)


def _s5_discretize(lam_re, lam_im, log_dt, b_re, b_im):
    dt = jnp.exp(log_dt)[:, None]
    mag = jnp.exp(lam_re * dt)
    ar, ai = mag * jnp.cos(lam_im * dt), mag * jnp.sin(lam_im * dt)
    den = lam_re * lam_re + lam_im * lam_im
    er = ((ar - 1) * lam_re + ai * lam_im) / den
    ei = (ai * lam_re - (ar - 1) * lam_im) / den
    bbr = er[..., None] * b_re - ei[..., None] * b_im
    bbi = er[..., None] * b_im + ei[..., None] * b_re
    return ar, ai, bbr, bbi


def _blockdiag(w, nkb):
    g, r, c = w.shape
    gpb = g // nkb
    eye = jnp.eye(gpb, dtype=w.dtype)
    return jnp.einsum("kgrc,gh->kgrhc", w.reshape(nkb, gpb, r, c), eye).reshape(nkb, gpb * r, gpb * c)


def _blockdiag_extract(m, g):
    nkb = m.shape[0]
    gpb = g // nkb
    r, c = m.shape[1] // gpb, m.shape[2] // gpb
    eye = jnp.eye(gpb, dtype=m.dtype)
    return jnp.einsum("kgrhc,gh->kgrc", m.reshape(nkb, gpb, r, gpb, c), eye).reshape(g, r, c)


def _scan_powers(ar, ai, reverse):
    ar = ar.reshape(-1)
    ai = (-ai if reverse else ai).reshape(-1)
    cmul = lambda x, y: (x[0] * y[0] - x[1] * y[1], x[0] * y[1] + x[1] * y[0])
    a1 = (ar, ai)
    a2 = cmul(a1, a1)
    a4 = cmul(a2, a2)
    r = jnp.arange(SUBLANES)[:, None]
    rows = []
    for sft, p in ((1, a1), (2, a2), (4, a4)):
        keep = (r + sft <= SUBLANES - 1) if reverse else (r >= sft)
        rows += [jnp.where(keep, p[0][None, :], 0.0), jnp.where(keep, p[1][None, :], 0.0)]
    pows = [a1]
    for _ in range(SUBLANES - 1):
        pows.append(cmul(pows[-1], a1))
    if reverse:
        pows = pows[::-1]
    rows += [jnp.stack([p[0] for p in pows]), jnp.stack([p[1] for p in pows])]
    return jnp.concatenate(rows, axis=0).astype(F32)


def _scan_tile(sr, si, pw_ref, car, nrg, reverse):
    nsb = sr.shape[1]
    ch = min(SCAN_LANES, nsb)
    nch = nsb // ch
    row = 0 if reverse else SUBLANES - 1

    def step(t, carry):
        rg = (nrg - 1 - t) if reverse else t
        off = pl.multiple_of(rg * SUBLANES, SUBLANES)
        out = []
        for c in range(nch):
            cols = slice(c * ch, (c + 1) * ch)
            cr, ci = carry[2 * c], carry[2 * c + 1]
            br = sr[pl.ds(off, SUBLANES), cols]
            bi = si[pl.ds(off, SUBLANES), cols]
            for idx, sft in enumerate((1, 2, 4)):
                sh = SUBLANES - sft if reverse else sft
                tr = pltpu.roll(br, sh, axis=0)
                ti = pltpu.roll(bi, sh, axis=0)
                mr = pw_ref[16 * idx:16 * idx + 8, cols]
                mi = pw_ref[16 * idx + 8:16 * idx + 16, cols]
                br, bi = br + mr * tr - mi * ti, bi + mr * ti + mi * tr
            apr, api = pw_ref[48:56, cols], pw_ref[56:64, cols]
            xr = br + apr * cr - api * ci
            xi = bi + apr * ci + api * cr
            sr[pl.ds(off, SUBLANES), cols] = xr
            si[pl.ds(off, SUBLANES), cols] = xi
            out += [jnp.broadcast_to(xr[row:row + 1, :], xr.shape), jnp.broadcast_to(xi[row:row + 1, :], xi.shape)]
        return tuple(out)

    init = []
    for c in range(nch):
        cols = slice(c * ch, (c + 1) * ch)
        init += [car[0:SUBLANES, cols], car[SUBLANES:2 * SUBLANES, cols]]
    fin = lax.fori_loop(0, nrg, step, tuple(init))
    for c in range(nch):
        cols = slice(c * ch, (c + 1) * ch)
        car[0:SUBLANES, cols] = fin[2 * c]
        car[SUBLANES:2 * SUBLANES, cols] = fin[2 * c + 1]


def s5_fwd(u, wb_r, wb_i, wc_r, wc_i, pw, d_skip, name):
    s, d = u.shape
    nkb, kb, nsb = wb_r.shape
    ts = _tile(s, 256, SUBLANES)
    dvec = _vec8(d_skip)

    def body(u_ref, wbr, wbi, wcr, wci, pw_ref, dv_ref, xr_ref, xi_ref, yy_ref, g_ref, sr, si, car):
        @pl.when(pl.program_id(1) == 0)
        def _():
            car[...] = jnp.zeros_like(car)

        uu = u_ref[...]
        ub = uu.astype(BF16)
        sr[...] = jnp.dot(ub, wbr[...], preferred_element_type=F32)
        si[...] = jnp.dot(ub, wbi[...], preferred_element_type=F32)
        _scan_tile(sr, si, pw_ref, car, ts // SUBLANES, False)
        xr, xi = sr[...], si[...]
        xr_ref[...] = xr
        xi_ref[...] = xi
        y = (jnp.dot(xr.astype(BF16), wcr[...], preferred_element_type=F32)
             + jnp.dot(xi.astype(BF16), wci[...], preferred_element_type=F32) + dv_ref[0:1, :] * uu)
        yy_ref[...] = y
        g_ref[...] = _gelu(y).astype(g_ref.dtype)

    cspec = pl.BlockSpec((ts, kb), lambda k, i: (i, k))
    sspec = pl.BlockSpec((ts, nsb), lambda k, i: (i, k))
    wbspec = pl.BlockSpec((None, kb, nsb), lambda k, i: (k, 0, 0))
    wcspec = pl.BlockSpec((None, nsb, kb), lambda k, i: (k, 0, 0))
    ns = nkb * nsb
    return pl.pallas_call(
        body, name=name, grid=(nkb, s // ts),
        in_specs=[cspec, wbspec, wbspec, wcspec, wcspec, pl.BlockSpec((64, nsb), lambda k, i: (0, k)),
                  pl.BlockSpec((SUBLANES, kb), lambda k, i: (0, k))],
        out_specs=[sspec, sspec, cspec, cspec],
        out_shape=[jax.ShapeDtypeStruct((s, ns), F32), jax.ShapeDtypeStruct((s, ns), F32),
                   jax.ShapeDtypeStruct((s, d), F32), jax.ShapeDtypeStruct((s, d), BF16)],
        scratch_shapes=[pltpu.VMEM((ts, nsb), F32), pltpu.VMEM((ts, nsb), F32), pltpu.VMEM((2 * SUBLANES, nsb), F32)],
        compiler_params=_cp(("parallel", "arbitrary")),
    )(u, wb_r, wb_i, wc_r, wc_i, pw, dvec)


def s5_bwd(dg, yy, u, xr, xi, wb_r, wb_i, wc_r, wc_i, pwb, d_skip, name):
    s, d = u.shape
    nkb, kb, nsb = wb_r.shape
    ns = nkb * nsb
    ts = _tile(s, 256, SUBLANES)
    nt = s // ts
    dvec = _vec8(d_skip)

    def body(dg_ref, yy_ref, u_ref, xr_ref, xi_ref, xrp_ref, xip_ref, wbr, wbi, wcr, wci, pw_ref, dv_ref,
             du_ref, dwbr, dwbi, dwcr, dwci, da_ref, dd_ref, sr, si, car):
        i = pl.program_id(1)

        @pl.when(i == 0)
        def _():
            car[...] = jnp.zeros_like(car)
            for r in (dwbr, dwbi, dwcr, dwci, da_ref, dd_ref):
                r[...] = jnp.zeros_like(r)

        uu = u_ref[...]
        dyy = dg_ref[...] * _gelu_grad(yy_ref[...])
        dd_ref[0:1, :] += jnp.sum(dyy * uu, axis=0, keepdims=True)
        dyb = dyy.astype(BF16)
        sr[...] = lax.dot_general(dyb, wcr[...], _DIMS["nt"], preferred_element_type=F32)
        si[...] = lax.dot_general(dyb, wci[...], _DIMS["nt"], preferred_element_type=F32)
        _scan_tile(sr, si, pw_ref, car, ts // SUBLANES, True)
        gr, gi = sr[...], si[...]
        grb, gib = gr.astype(BF16), gi.astype(BF16)
        xrt, xit = xr_ref[...], xi_ref[...]
        dwcr[...] += lax.dot_general(xrt.astype(BF16), dyb, _DIMS["tn"], preferred_element_type=F32)
        dwci[...] += lax.dot_general(xit.astype(BF16), dyb, _DIMS["tn"], preferred_element_type=F32)
        ub = uu.astype(BF16)
        dwbr[...] += lax.dot_general(ub, grb, _DIMS["tn"], preferred_element_type=F32)
        dwbi[...] += lax.dot_general(ub, gib, _DIMS["tn"], preferred_element_type=F32)
        du_ref[...] = (lax.dot_general(grb, wbr[...], _DIMS["nt"], preferred_element_type=F32)
                       + lax.dot_general(gib, wbi[...], _DIMS["nt"], preferred_element_type=F32)
                       + dyy * dv_ref[0:1, :])
        has_prev = (i < nt - 1).astype(F32)
        rowid = lax.broadcasted_iota(jnp.int32, (ts, nsb), 0)
        pr = jnp.broadcast_to(xrp_ref[SUBLANES - 1:SUBLANES, :] * has_prev, (ts, nsb))
        pi = jnp.broadcast_to(xip_ref[SUBLANES - 1:SUBLANES, :] * has_prev, (ts, nsb))
        xpr = jnp.where(rowid == 0, pr, pltpu.roll(xrt, 1, axis=0))
        xpi = jnp.where(rowid == 0, pi, pltpu.roll(xit, 1, axis=0))
        da_ref[0:1, :] += jnp.sum(gr * xpr + gi * xpi, axis=0, keepdims=True)
        da_ref[1:2, :] += jnp.sum(gi * xpr - gr * xpi, axis=0, keepdims=True)

    cspec = pl.BlockSpec((ts, kb), lambda k, i: (nt - 1 - i, k))
    sspec = pl.BlockSpec((ts, nsb), lambda k, i: (nt - 1 - i, k))
    pspec = pl.BlockSpec((SUBLANES, nsb), lambda k, i: (jnp.maximum((nt - 1 - i) * (ts // SUBLANES) - 1, 0), k))
    wbspec = pl.BlockSpec((None, kb, nsb), lambda k, i: (k, 0, 0))
    wcspec = pl.BlockSpec((None, nsb, kb), lambda k, i: (k, 0, 0))
    v8s = pl.BlockSpec((SUBLANES, nsb), lambda k, i: (0, k))
    v8c = pl.BlockSpec((SUBLANES, kb), lambda k, i: (0, k))
    return pl.pallas_call(
        body, name=name, grid=(nkb, nt),
        in_specs=[cspec, cspec, cspec, sspec, sspec, pspec, pspec, wbspec, wbspec, wcspec, wcspec,
                  pl.BlockSpec((64, nsb), lambda k, i: (0, k)), v8c],
        out_specs=[cspec, wbspec, wbspec, wcspec, wcspec, v8s, v8c],
        out_shape=[jax.ShapeDtypeStruct((s, d), F32),
                   jax.ShapeDtypeStruct((nkb, kb, nsb), F32), jax.ShapeDtypeStruct((nkb, kb, nsb), F32),
                   jax.ShapeDtypeStruct((nkb, nsb, kb), F32), jax.ShapeDtypeStruct((nkb, nsb, kb), F32),
                   jax.ShapeDtypeStruct((SUBLANES, ns), F32), jax.ShapeDtypeStruct((SUBLANES, d), F32)],
        scratch_shapes=[pltpu.VMEM((ts, nsb), F32), pltpu.VMEM((ts, nsb), F32), pltpu.VMEM((2 * SUBLANES, nsb), F32)],
        compiler_params=_cp(("parallel", "arbitrary")),
    )(dg, yy, u, xr, xi, xr, xi, wb_r, wb_i, wc_r, wc_i, pwb, dvec)


def s5_operands(lam_re, lam_im, log_dt, b_re, b_im, c_re, c_im, d):
    ar, ai, bbr, bbi = _s5_discretize(lam_re, lam_im, log_dt, b_re, b_im)
    nkb = max(d // S5_KB, 1)
    wb_r = _blockdiag(bbr.transpose(0, 2, 1), nkb).astype(BF16)
    wb_i = _blockdiag(bbi.transpose(0, 2, 1), nkb).astype(BF16)
    wc_r = _blockdiag(c_re.transpose(0, 2, 1), nkb).astype(BF16)
    wc_i = _blockdiag(-c_im.transpose(0, 2, 1), nkb).astype(BF16)
    return wb_r, wb_i, wc_r, wc_i, _scan_powers(ar, ai, False), _scan_powers(ar, ai, True)


MESH = pl.DeviceIdType.MESH
HBM_SPEC = pl.BlockSpec(memory_space=pltpu.HBM)


def _me():
    return 4 * lax.axis_index("x") + 2 * lax.axis_index("y") + lax.axis_index("c")


N_COPIES = N_DEV - 1


class Carried:
    def __init__(self, kind, operands):
        self.kind, self.operands = kind, list(operands)
        self.n = len(self.operands)

    def call_args(self):
        shapes = [jax.ShapeDtypeStruct((N_DEV,) + tuple(a.shape[-2:]), a.dtype) for a in self.operands]
        scratch = [pltpu.SemaphoreType.DMA((N_COPIES * self.n,)), pltpu.SemaphoreType.DMA((N_COPIES * self.n,)),
                   pltpu.SemaphoreType.DMA((self.n,))]
        return [HBM_SPEC] * self.n, shapes, scratch

    def _plan(self, t, x_ref, out_ref, send, recv, loc):
        x, y, c = lax.axis_index("x"), lax.axis_index("y"), lax.axis_index("c")
        me = 4 * x + 2 * y + c

        def rdma(k, src, dst, to):
            return pltpu.make_async_remote_copy(src_ref=src, dst_ref=dst, send_sem=send.at[N_COPIES * t + k],
                                                recv_sem=recv.at[N_COPIES * t + k], device_id=to, device_id_type=MESH)

        if self.kind == "ex":
            peers = [(1 - x if k & 4 else x, 1 - y if k & 2 else y, 1 - c if k & 1 else c) for k in range(1, N_DEV)]
            sends = [rdma(k, x_ref.at[4 * px + 2 * py + pc], out_ref.at[me], (px, py, pc))
                     for k, (px, py, pc) in enumerate(peers)]
            local = pltpu.make_async_copy(x_ref.at[me], out_ref.at[me], loc.at[t])
            return dict(local=local, first=sends, relay_on=[], relays=[], arrive=sends)
        sibling = (x, y, 1 - c)
        chips = [(1 - x, y), (x, 1 - y), (1 - x, 1 - y)]
        slot = lambda px, py, pc: out_ref.at[4 * px + 2 * py + pc]
        first = [rdma(0, x_ref, slot(x, y, c), sibling)]
        first += [rdma(1 + j, x_ref, slot(x, y, c), (*chip, c)) for j, chip in enumerate(chips)]
        relay_on = [rdma(1 + j, slot(*chip, c), slot(*chip, c), (x, y, c)) for j, chip in enumerate(chips)]
        relays = [rdma(4 + j, slot(*chip, c), slot(*chip, c), sibling) for j, chip in enumerate(chips)]
        arrive = [rdma(0, slot(*sibling), slot(*sibling), (x, y, c))]
        arrive += [rdma(4 + j, slot(*chip, 1 - c), slot(*chip, 1 - c), (x, y, c)) for j, chip in enumerate(chips)]
        local = pltpu.make_async_copy(x_ref, slot(x, y, c), loc.at[t])
        return dict(local=local, first=first, relay_on=relay_on, relays=relays, arrive=arrive)

    def _plans(self, refs):
        xs, outs, (send, recv, loc) = refs[:self.n], refs[self.n:2 * self.n], refs[2 * self.n:]
        return [self._plan(t, xs[t], outs[t], send, recv, loc) for t in range(self.n)]

    def begin(self, refs):
        for p in self._plans(refs):
            p["local"].start()
            for cp in p["first"]:
                cp.start()

    def finish(self, refs):
        plans = self._plans(refs)
        for p in plans:
            for landed, relay in zip(p["relay_on"], p["relays"]):
                landed.wait_recv()
                relay.start()
        for p in plans:
            for cp in p["arrive"]:
                cp.wait_recv()
            for cp in p["first"] + p["relays"]:
                cp.wait_send()
            p["local"].wait()


def _carry_refs(refs, n_in, n_out, car):
    if car is None:
        return list(refs), None
    n = car.n
    host = list(refs[:n_in]) + list(refs[n_in + n:n_in + n + n_out]) + list(refs[n_in + 2 * n + n_out:-3])
    return host, list(refs[n_in:n_in + n]) + list(refs[n_in + n + n_out:n_in + 2 * n + n_out]) + list(refs[-3:])


def _carry_steps(car, crefs, grid):
    if car is None:
        return lambda: None
    ids = [pl.program_id(a) for a in range(len(grid))]
    first = functools.reduce(jnp.logical_and, [i == 0 for i in ids])
    last = functools.reduce(jnp.logical_and, [i == g - 1 for i, g in zip(ids, grid)])

    @pl.when(first)
    def _():
        car.begin(crefs)

    def after():
        @pl.when(last)
        def _():
            car.finish(crefs)

    return after


def communicate(kind, operands, name):
    car = Carried(kind, operands)
    specs, shapes, scratch = car.call_args()

    def body(*refs):
        car.begin(refs)
        car.finish(refs)

    return pl.pallas_call(body, name=name, in_specs=specs, out_specs=specs, out_shape=shapes,
                          scratch_shapes=scratch)(*car.operands)


def all_gather(shard, name):
    return communicate("ag", [shard], name)[0]


def sum_slots(parts, name):
    _, m, n = parts.shape
    tm = _tile(m, 256, SUBLANES)

    def body(p_ref, o_ref):
        acc = p_ref[0].astype(F32)
        for q in range(1, N_DEV):
            acc = acc + p_ref[q].astype(F32)
        o_ref[...] = acc

    return pl.pallas_call(
        body, name=name, grid=(m // tm,), in_specs=[pl.BlockSpec((N_DEV, tm, n), lambda i: (0, i, 0))],
        out_specs=pl.BlockSpec((tm, n), lambda i: (i, 0)), out_shape=jax.ShapeDtypeStruct((m, n), F32),
        compiler_params=_cp(("parallel",)),
    )(parts)


PACK_COLS = 1024
PACK_ROWS = 16


def _pack_flat(pieces, dtype):
    lead = pieces[0].shape[:-1]
    flat = jnp.concatenate([p.astype(dtype) for p in pieces], axis=-1)
    unit = PACK_COLS * PACK_ROWS
    total = -(-flat.shape[-1] // unit) * unit
    flat = jnp.pad(flat, [(0, 0)] * len(lead) + [(0, total - flat.shape[-1])])
    return flat.reshape(*lead, total // PACK_COLS, PACK_COLS)


def _unpack_flat(packed, sizes):
    lead = packed.shape[:-2]
    flat = packed.reshape(*lead, -1)
    out, off = [], 0
    for n in sizes:
        out.append(flat[..., off:off + n])
        off += n
    return out


def mod_fwd(c_all, w_mod, b_cols, name):
    nl, d, n = w_mod.shape

    def body(c_ref, w_ref, b_ref, o_ref):
        cv = c_ref[...]
        sc = (cv * _sigmoid(cv)).astype(BF16)
        o_ref[...] = jnp.dot(sc, w_ref[...].astype(BF16), preferred_element_type=F32) + b_ref[...]

    return pl.pallas_call(
        body, name=name, grid=(nl,),
        in_specs=[pl.BlockSpec((N_DEV, d), lambda l: (0, 0)), pl.BlockSpec((None, d, n), lambda l: (l, 0, 0)),
                  pl.BlockSpec((None, 1, n), lambda l: (l, 0, 0))],
        out_specs=pl.BlockSpec((None, N_DEV, n), lambda l: (l, 0, 0)),
        out_shape=jax.ShapeDtypeStruct((nl, N_DEV, n), F32), compiler_params=_cp(("parallel",)),
    )(c_all, w_mod, b_cols.reshape(nl, 1, n))


def mod_bwd(c_all, dmod_cols, name):
    nl, _, n = dmod_cols.shape
    d = c_all.shape[1]

    def body(c_ref, g_ref, o_ref):
        cv = c_ref[...]
        sc = (cv * _sigmoid(cv)).astype(BF16)
        o_ref[...] = lax.dot_general(sc, g_ref[...].astype(BF16), _DIMS["tn"], preferred_element_type=F32)

    return pl.pallas_call(
        body, name=name, grid=(nl,),
        in_specs=[pl.BlockSpec((N_DEV, d), lambda l: (0, 0)), pl.BlockSpec((None, N_DEV, n), lambda l: (l, 0, 0))],
        out_specs=pl.BlockSpec((None, d, n), lambda l: (l, 0, 0)),
        out_shape=jax.ShapeDtypeStruct((nl, d, n), F32), compiler_params=_cp(("parallel",)),
    )(c_all, dmod_cols)


def adamw(w, g, m, v, name):
    r, c = w.shape
    tr = _tile(r, 512, SUBLANES)
    c1 = 1.0 - ADAM_B1 ** ADAM_STEP
    c2 = 1.0 - ADAM_B2 ** ADAM_STEP

    def body(w_ref, g_ref, m_ref, v_ref, d_ref, nm_ref, nv_ref):
        gg = g_ref[...]
        nm = ADAM_B1 * m_ref[...] + (1.0 - ADAM_B1) * gg
        nv = ADAM_B2 * v_ref[...] + (1.0 - ADAM_B2) * (gg * gg)
        nm_ref[...] = nm
        nv_ref[...] = nv
        d_ref[...] = -ADAM_LR * ((nm / c1) / (jnp.sqrt(nv / c2) + ADAM_EPS) + ADAM_WD * w_ref[...])

    spec = pl.BlockSpec((tr, c), lambda i: (i, 0))
    sd = jax.ShapeDtypeStruct((r, c), F32)
    return pl.pallas_call(
        body, name=name, grid=(r // tr,), in_specs=[spec] * 4, out_specs=[spec] * 3, out_shape=[sd, sd, sd],
        compiler_params=_cp(("parallel",)),
    )(w, g, m, v)


WEIGHTS = ["norm_g", "w_mod", "b_mod", "sb_w_qkv", "sb_w_o", "s5_lam_re", "s5_lam_im", "s5_log_dt", "s5_b_re",
           "s5_b_im", "s5_c_re", "s5_c_im", "s5_d", "s5_w_glu", "s5_b_glu", "cv_w_pw1", "cv_b_pw1", "cv_w_dw",
           "cv_b_dw", "cv_ln_g", "cv_ln_b", "cv_w_pw2", "cv_b_pw2", "ffn_w_gate", "ffn_w_up", "ffn_w_down"]
BIG = ["w_mod", "sb_w_qkv", "sb_w_o", "s5_w_glu", "cv_w_pw1", "cv_w_pw2", "ffn_w_gate", "ffn_w_up", "ffn_w_down"]
SMALL_SHARDED = ["norm_g", "cv_b_pw1", "cv_w_dw", "cv_b_dw", "cv_ln_g", "cv_ln_b", "cv_b_pw2"]
SMALL = [n for n in WEIGHTS if n not in BIG]
FFN_KEYS = ["gate", "up"]


def _unshard_last(part, local_shape):
    a = jnp.moveaxis(part.reshape((N_DEV,) + tuple(local_shape)), 0, -2)
    return a.reshape(tuple(local_shape[:-1]) + (N_DEV * local_shape[-1],))


def kernel(x, c, norm_g, w_mod, b_mod, sb_w_qkv, sb_w_o, s5_lam_re, s5_lam_im, s5_log_dt, s5_b_re, s5_b_im, s5_c_re, s5_c_im, s5_d, s5_w_glu, s5_b_glu, cv_w_pw1, cv_b_pw1, cv_w_dw, cv_b_dw, cv_ln_g, cv_ln_b, cv_w_pw2, cv_b_pw2, ffn_w_gate, ffn_w_up, ffn_w_down, loss_target, m_norm_g, m_w_mod, m_b_mod, m_sb_w_qkv, m_sb_w_o, m_s5_lam_re, m_s5_lam_im, m_s5_log_dt, m_s5_b_re, m_s5_b_im, m_s5_c_re, m_s5_c_im, m_s5_d, m_s5_w_glu, m_s5_b_glu, m_cv_w_pw1, m_cv_b_pw1, m_cv_w_dw, m_cv_b_dw, m_cv_ln_g, m_cv_ln_b, m_cv_w_pw2, m_cv_b_pw2, m_ffn_w_gate, m_ffn_w_up, m_ffn_w_down, v_norm_g, v_w_mod, v_b_mod, v_sb_w_qkv, v_sb_w_o, v_s5_lam_re, v_s5_lam_im, v_s5_log_dt, v_s5_b_re, v_s5_b_im, v_s5_c_re, v_s5_c_im, v_s5_d, v_s5_w_glu, v_s5_b_glu, v_cv_w_pw1, v_cv_b_pw1, v_cv_w_dw, v_cv_b_dw, v_cv_ln_g, v_cv_ln_b, v_cv_w_pw2, v_cv_b_pw2, v_ffn_w_gate, v_ffn_w_up, v_ffn_w_down):
    p = dict(locals())
    me = _me()
    s, d = x.shape[1], x.shape[2]
    depth = norm_g.shape[0]
    h = x.reshape(s, d)
    target = loss_target.reshape(s, d)

    pieces = [p[n].reshape(-1) for n in SMALL_SHARDED] + [c.reshape(-1)]
    parts = _unpack_flat(all_gather(_pack_flat(pieces, F32), "ag_small"), [q.shape[0] for q in pieces])
    full = {n: _unshard_last(part, p[n].shape) for n, part in zip(SMALL_SHARDED, parts)}
    c_all = parts[-1]

    nmod = w_mod.shape[2]
    b_cols = lax.dynamic_slice_in_dim(b_mod, me * nmod, nmod, axis=1)
    mod_cols = mod_fwd(c_all, w_mod, b_cols, "mod_fwd")
    g_mod = all_gather(mod_cols.reshape(depth * N_DEV, nmod), "ag_mod").reshape(N_DEV, depth, N_DEV, nmod)
    mod = jnp.moveaxis(lax.dynamic_index_in_dim(g_mod, me, axis=2, keepdims=False), 0, 1).reshape(depth, N_DEV * nmod)
    ng = full["norm_g"]

    def layer_pieces(l):
        kind, j = l % 3, l // 3
        if kind == 0:
            ps = [("qkv", sb_w_qkv[j], "col"), ("o", sb_w_o[j], "row")]
        elif kind == 1:
            ps = [("glu", s5_w_glu[j], "col")]
        else:
            ps = [("pw1", cv_w_pw1[j], "col"), ("pw2", cv_w_pw2[j], "row")]
        return ps + [("gate", ffn_w_gate[l], "col"), ("up", ffn_w_up[l], "col"), ("down", ffn_w_down[l], "row")]

    def weight_gather(l):
        return Carried("ag", [a.astype(BF16) for _, a, _ in layer_pieces(l)])

    def gathered_weights(l, got):
        out = {}
        for (key, a, how), blk in zip(layer_pieces(l), got):
            r, cc = a.shape
            out[key] = blk.transpose(1, 0, 2).reshape(r, N_DEV * cc) if how == "col" else blk.reshape(N_DEV * r, cc)
        return out

    def grad_exchange(l, grads, keys):
        slabs = []
        for key, a, how in layer_pieces(l):
            if key in keys:
                r, cc = a.shape
                g = grads[key]
                slabs.append(g.reshape(r, N_DEV, cc).transpose(1, 0, 2) if how == "col" else g.reshape(N_DEV, r, cc))
        return Carried("ex", slabs)

    def store_grads(l, keys, got):
        names = {"qkv": "sb_w_qkv", "o": "sb_w_o", "glu": "s5_w_glu", "pw1": "cv_w_pw1", "pw2": "cv_w_pw2",
                 "gate": "ffn_w_gate", "up": "ffn_w_up", "down": "ffn_w_down"}
        for key, parts in zip(keys, got):
            idx = l if key in ("gate", "up", "down") else l // 3
            gbig[names[key]][idx] = sum_slots(parts, f"rs_sum_{key}{l}")

    saved = []
    gbig = {n: [None] * p[n].shape[0] for n in BIG if n != "w_mod"}
    w_next = gathered_weights(0, communicate("ag", weight_gather(0).operands, "ag_w0"))
    for l in range(depth):
        kind, j = l % 3, l // 3
        w = w_next
        sh_m, sc_m, g_m, sh_f, sc_f, g_f = jnp.split(mod[l], 6)
        st = {"w": w, "h": h}
        if kind == 0:
            (u,) = norm_mod_fwd(h, ng[l, 0], sc_m, sh_m, [BF16], f"nm_a{l}")
            qkv = mm([(u, w["qkv"])], "nn", BF16, name=f"qkv{l}")
            o = attn_fwd(qkv, f"attn_fwd{l}")
            m = mm([(o, w["o"])], "nn", F32, name=f"attn_o{l}")
            st.update(u=u, qkv=qkv, o=o)
        elif kind == 1:
            (u,) = norm_mod_fwd(h, ng[l, 0], sc_m, sh_m, [F32], f"nm_a{l}")
            ops = s5_operands(s5_lam_re[j], s5_lam_im[j], s5_log_dt[j], s5_b_re[j], s5_b_im[j], s5_c_re[j],
                              s5_c_im[j], d)
            xr, xi, yy, gl = s5_fwd(u, *ops[:5], s5_d[j], f"s5_fwd{l}")
            p1, p2, m = mm_dual(gl, w["glu"], s5_b_glu[j], "glu", F32, F32, f"s5_glu{l}")
            st.update(u=u, ops=ops, xr=xr, xi=xi, yy=yy, gl=gl, p1=p1, p2=p2)
        else:
            (u,) = norm_mod_fwd(h, ng[l, 0], sc_m, sh_m, [BF16], f"nm_a{l}")
            p1, p2, hg = mm_dual(u, w["pw1"], full["cv_b_pw1"][j], "glu", F32, F32, f"cv_pw1{l}")
            hc, hs = conv_mid_fwd(hg, full["cv_w_dw"][j], full["cv_b_dw"][j], full["cv_ln_g"][j],
                                  full["cv_ln_b"][j], f"cv_mid{l}")
            m = mm([(hs, w["pw2"])], "nn", F32, bias=full["cv_b_pw2"][j], name=f"cv_pw2{l}")
            st.update(u=u, p1=p1, p2=p2, hg=hg, hc=hc, hs=hs)
        h2 = resid_fwd(h, m, ng[l, 1], g_m, f"res_a{l}")
        (u2,) = norm_mod_fwd(h2, ng[l, 2], sc_f, sh_f, [BF16], f"nm_f{l}")
        w_gu = jnp.concatenate([w["gate"], w["up"]], axis=1)
        if l + 1 < depth:
            (f1, f2, z), got = mm_dual(u2, w_gu, None, "swiglu", BF16, BF16, f"ffn_up{l}", car=weight_gather(l + 1))
            w_next = gathered_weights(l + 1, got)
        else:
            f1, f2, z = mm_dual(u2, w_gu, None, "swiglu", BF16, BF16, f"ffn_up{l}")
        f = mm([(z, w["down"])], "nn", F32, name=f"ffn_down{l}")
        h = resid_fwd(h2, f, ng[l, 3], g_f, f"res_f{l}")
        st.update(m=m, h2=h2, u2=u2, f1=f1, f2=f2, z=z, f=f)
        saved.append(st)

    loss_arr, dh = loss_and_grad(h, target, "loss")
    loss = lax.psum(loss_arr[0, 0], AXES)

    nl_sb, nl_s5, nl_cv = sb_w_qkv.shape[0], s5_w_glu.shape[0], cv_w_pw1.shape[0]
    pending = None
    dng = [None] * depth
    dmod = [None] * depth
    gs5 = {n: [None] * nl_s5 for n in SMALL if n.startswith("s5_")}
    gcv = {n: [None] * nl_cv for n in SMALL if n.startswith("cv_")}
    for l in reversed(range(depth)):
        kind, j = l % 3, l // 3
        st = saved[l]
        w = st["w"]
        sh_m, sc_m, g_m, sh_f, sc_f, g_f = jnp.split(mod[l], 6)
        gw = {}
        df, s_rf = resid_bwd(dh, st["f"], ng[l, 3], g_f, BF16, f"res_f_bwd{l}")
        if pending is None:
            d1, d2 = mm_act_bwd(df, w["down"], st["f1"], st["f2"], "swiglu", f"ffn_dz{l}")
        else:
            (d1, d2), got = mm_act_bwd(df, w["down"], st["f1"], st["f2"], "swiglu", f"ffn_dz{l}",
                                       car=grad_exchange(pending[0], pending[1], FFN_KEYS))
            store_grads(pending[0], FFN_KEYS, got)
        gw["down"] = mm([(st["z"], df)], "tn", BF16, name=f"ffn_dwd{l}")
        if pending is None:
            du2 = mm([(d1, w["gate"]), (d2, w["up"])], "nt", F32, name=f"ffn_du{l}")
        else:
            rest = [key for key, _, _ in layer_pieces(pending[0]) if key not in FFN_KEYS]
            du2, got = mm([(d1, w["gate"]), (d2, w["up"])], "nt", F32, name=f"ffn_du{l}",
                          car=grad_exchange(pending[0], pending[1], rest))
            store_grads(pending[0], rest, got)
        gw["gate"] = mm([(st["u2"], d1)], "tn", BF16, name=f"ffn_dwg{l}")
        gw["up"] = mm([(st["u2"], d2)], "tn", BF16, name=f"ffn_dwu{l}")
        dh2, s_nf = norm_mod_bwd([du2], st["h2"], dh, ng[l, 2], sc_f, f"nm_f_bwd{l}")
        dm, s_rm = resid_bwd(dh2, st["m"], ng[l, 1], g_m, F32 if kind == 1 else BF16, f"res_a_bwd{l}")
        if kind == 0:
            do = mm([(dm, w["o"])], "nt", BF16, name=f"attn_do{l}")
            gw["o"] = mm([(st["o"], dm)], "tn", BF16, name=f"attn_dwo{l}")
            dq, dk, dv = attn_bwd(st["qkv"], do, f"attn_bwd{l}")
            wq = w["qkv"]
            dus = [mm([(dq, wq[:, :d]), (dk, wq[:, d:2 * d]), (dv, wq[:, 2 * d:])], "nt", F32, name=f"qkv_du{l}")]
            gw["qkv"] = jnp.concatenate([mm([(st["u"], t)], "tn", BF16, name=f"qkv_dw{l}_{i}")
                                         for i, t in enumerate((dq, dk, dv))], axis=1)
        elif kind == 1:
            d1, d2, cs = dual_bwd(dm, st["p1"], st["p2"], "glu", f"s5_glu_bwd{l}")
            gs5["s5_b_glu"][j] = jnp.concatenate([cs[0], cs[1]])
            wg = w["glu"]
            dgl = mm([(d1, wg[:, :d]), (d2, wg[:, d:])], "nt", F32, name=f"s5_dgl{l}")
            gw["glu"] = jnp.concatenate([mm([(st["gl"], t)], "tn", BF16, name=f"s5_dwglu{l}_{i}")
                                         for i, t in enumerate((d1, d2))], axis=1)
            ops = st["ops"]
            du, dwbr, dwbi, dwcr, dwci, da, dd = s5_bwd(dgl, st["yy"], st["u"], st["xr"], st["xi"], *ops[:4],
                                                        ops[5], s5_d[j], f"s5_bwd{l}")
            dus = [du]
            ngrp = s5_lam_re.shape[1]
            ext = lambda t: _blockdiag_extract(t, ngrp).transpose(0, 2, 1)
            _, disc_vjp = jax.vjp(_s5_discretize, s5_lam_re[j], s5_lam_im[j], s5_log_dt[j], s5_b_re[j], s5_b_im[j])
            shp = s5_lam_re[j].shape
            dlr, dli, dldt, dbr, dbi = disc_vjp((da[0].reshape(shp), da[1].reshape(shp), ext(dwbr), ext(dwbi)))
            for n, t in (("s5_lam_re", dlr), ("s5_lam_im", dli), ("s5_log_dt", dldt), ("s5_b_re", dbr),
                         ("s5_b_im", dbi), ("s5_c_re", ext(dwcr)), ("s5_c_im", -ext(dwci)), ("s5_d", dd[0])):
                gs5[n][j] = t
        else:
            dhs = mm([(dm, w["pw2"])], "nt", BF16, name=f"cv_dhs{l}")
            gw["pw2"] = mm([(st["hs"], dm)], "tn", BF16, name=f"cv_dwpw2{l}")
            dhg, dwdw, s_cv = conv_mid_bwd(dhs, st["hc"], st["hg"], full["cv_w_dw"][j], full["cv_ln_g"][j],
                                           full["cv_ln_b"][j], f"cv_mid_bwd{l}")
            d1, d2, cs = dual_bwd(dhg, st["p1"], st["p2"], "glu", f"cv_glu_bwd{l}")
            wp = w["pw1"]
            dus = [mm([(d1, wp[:, :d]), (d2, wp[:, d:])], "nt", F32, name=f"cv_du{l}")]
            gw["pw1"] = jnp.concatenate([mm([(st["u"], t)], "tn", BF16, name=f"cv_dwpw1{l}_{i}")
                                         for i, t in enumerate((d1, d2))], axis=1)
            for n, t in (("cv_b_pw1", jnp.concatenate([cs[0], cs[1]])), ("cv_w_dw", dwdw[:cv_w_dw.shape[1]]),
                         ("cv_b_dw", s_cv[0]), ("cv_ln_g", s_cv[1]), ("cv_ln_b", s_cv[2]), ("cv_b_pw2", s_rm[2])):
                gcv[n][j] = t
        dh, s_nm = norm_mod_bwd(dus, st["h"], dh2, ng[l, 0], sc_m, f"nm_a_bwd{l}")
        dng[l] = jnp.stack([s_nm[2], s_rm[1], s_nf[2], s_rf[1]])
        dmod[l] = jnp.concatenate([s_nm[0], s_nm[1], s_rm[0], s_nf[0], s_nf[1], s_rf[0]])
        pending = (l, gw)
    keys = [key for key, _, _ in layer_pieces(pending[0])]
    store_grads(pending[0], keys, communicate("ex", grad_exchange(pending[0], pending[1], keys).operands, "rs_x_last"))

    local = {"norm_g": jnp.stack(dng), "b_mod": jnp.stack(dmod)}
    local.update({n: jnp.stack(t) for n, t in gs5.items()})
    local.update({n: jnp.stack(t) for n, t in gcv.items()})
    pieces = [local[n].reshape(-1) for n in SMALL]
    sizes = [q.shape[0] for q in pieces]
    gathered = all_gather(_pack_flat(pieces, F32), "ag_grads")
    sums = _unpack_flat(sum_slots(gathered, "sum_grads"), sizes)
    grads = {}
    for n, t in zip(SMALL, sums):
        t = t.reshape(local[n].shape)
        if n in SMALL_SHARDED:
            nsh = p[n].shape[-1]
            t = lax.dynamic_slice_in_dim(t, me * nsh, nsh, axis=t.ndim - 1)
        grads[n] = t.reshape(p[n].shape)
    dmod_all = _unpack_flat(gathered, sizes)[SMALL.index("b_mod")].reshape(N_DEV, depth, N_DEV * nmod)
    dmod_cols = jnp.moveaxis(lax.dynamic_slice_in_dim(dmod_all, me * nmod, nmod, axis=2), 0, 1)
    grads["w_mod"] = mod_bwd(c_all, dmod_cols, "mod_bwd")
    for n in gbig:
        grads[n] = jnp.stack(gbig[n])

    delta, new_m, new_v = {}, {}, {}
    for n in BIG:
        shp = p[n].shape
        two = lambda t: t.reshape(-1, shp[-1])
        delta[n], new_m[n], new_v[n] = (t.reshape(shp) for t in
                                        adamw(two(p[n]), two(grads[n]), two(p["m_" + n]), two(p["v_" + n]), f"adamw_{n}"))
    sizes = [p[n].size for n in SMALL]
    packs = [_pack_flat([t[n].reshape(-1) for n in SMALL], F32)
             for t in (p, grads, {n: p["m_" + n] for n in SMALL}, {n: p["v_" + n] for n in SMALL})]
    for res, out in zip(adamw(*packs, "adamw_small"), (delta, new_m, new_v)):
        for n, t in zip(SMALL, _unpack_flat(res, sizes)):
            out[n] = t.reshape(p[n].shape)

    return (loss, dh.reshape(x.shape), *[grads[n] for n in WEIGHTS], *[delta[n] for n in WEIGHTS],
            *[new_m[n] for n in WEIGHTS], *[new_v[n] for n in WEIGHTS])
```

```python
import functools
import math

import jax
import jax.numpy as jnp
from jax import lax
from jax.experimental import pallas as pl
from jax.experimental.pallas import tpu as pltpu

F32, BF16 = jnp.float32, jnp.bfloat16
N_DEV = 8
AXES = ("x", "y", "c")
EPS = 1e-6
HEAD_DIM = 64
LANES = 128
SUBLANES = 8
VMEM_LIMIT = 56 * 1024 * 1024
S5_GROUP = 16
ADAM_LR, ADAM_B1, ADAM_B2, ADAM_EPS, ADAM_WD, ADAM_STEP = 0.001, 0.9, 0.999, 1e-08, 0.01, 10
NEG_CUTOFF = -104.0


def _cp(sem):
    return pltpu.CompilerParams(dimension_semantics=sem, vmem_limit_bytes=VMEM_LIMIT)


def _tile(n, cap, mult=LANES):
    best = None
    for t in range(mult, min(n, cap) + 1, mult):
        if n % t == 0:
            best = t
    return best if best is not None else n


def _sigmoid(x):
    return 1.0 / (1.0 + jnp.exp(-x))


_DIMS = {"nn": (((1,), (0,)), ((), ())), "nt": (((1,), (1,)), ((), ())), "tn": (((0,), (0,)), ((), ()))}


CAPS_FULL_K = (1024, 1536, 4096)
CAPS_PAIRS = (512, 1024, 1408)
CAPS_TN = (4096, 1536, 1024)


def _host_call(body, name, grid, in_specs, out_specs, out_shape, scratch, sem, args, car):
    if car is None:
        return pl.pallas_call(body, name=name, grid=grid, in_specs=in_specs, out_specs=out_specs, out_shape=out_shape,
                              scratch_shapes=scratch, compiler_params=_cp(sem))(*args), None
    n_in, n_out = len(in_specs), len(out_specs)
    cspecs, cshapes, cscratch = car.call_args()

    def wrapped(*refs):
        host, crefs = _carry_refs(refs, n_in, n_out, car)
        after = _carry_steps(car, crefs, grid)
        body(*host)
        after()

    res = pl.pallas_call(
        wrapped, name=name, grid=grid, in_specs=in_specs + cspecs, out_specs=out_specs + cspecs,
        out_shape=out_shape + cshapes, scratch_shapes=scratch + cscratch,
        compiler_params=_cp(("arbitrary",) * len(grid)))(*args, *car.operands)
    return res[:n_out], res[n_out:]


def mm(pairs, mode, out_dtype=F32, bias=None, name="mm", caps=None, car=None):
    a0, b0 = pairs[0]
    if mode == "nn":
        (m, k), n = a0.shape, b0.shape[1]
    elif mode == "nt":
        (m, k), n = a0.shape, b0.shape[0]
    else:
        (k, m), n = a0.shape, b0.shape[1]
    if caps is None:
        caps = CAPS_TN if mode == "tn" else (CAPS_FULL_K if len(pairs) == 1 or k <= 1024 else CAPS_PAIRS)
    tm, tn, tk = _tile(m, caps[0]), _tile(n, caps[1]), _tile(k, caps[2])
    nk = k // tk
    npairs = len(pairs)
    dims = _DIMS[mode]

    def body(*refs):
        ins = refs[:2 * npairs]
        bias_ref = refs[2 * npairs] if bias is not None else None
        o_ref, acc = refs[-2], refs[-1]
        kk = pl.program_id(2)
        part = None
        for p in range(npairs):
            d = lax.dot_general(ins[2 * p][...].astype(BF16), ins[2 * p + 1][...].astype(BF16), dims,
                                preferred_element_type=F32)
            part = d if part is None else part + d

        def finish(r):
            if bias_ref is not None:
                r = r + bias_ref[...]
            o_ref[...] = r.astype(o_ref.dtype)

        if nk == 1:
            finish(part)
        else:
            @pl.when(kk == 0)
            def _():
                acc[...] = part

            @pl.when(kk > 0)
            def _():
                acc[...] += part

            @pl.when(kk == nk - 1)
            def _():
                finish(acc[...])

    if mode == "nn":
        sa, sb = pl.BlockSpec((tm, tk), lambda i, j, kk: (i, kk)), pl.BlockSpec((tk, tn), lambda i, j, kk: (kk, j))
    elif mode == "nt":
        sa, sb = pl.BlockSpec((tm, tk), lambda i, j, kk: (i, kk)), pl.BlockSpec((tn, tk), lambda i, j, kk: (j, kk))
    else:
        sa, sb = pl.BlockSpec((tk, tm), lambda i, j, kk: (kk, i)), pl.BlockSpec((tk, tn), lambda i, j, kk: (kk, j))
    in_specs, args = [], []
    for a, b in pairs:
        in_specs += [sa, sb]
        args += [a, b]
    if bias is not None:
        in_specs.append(pl.BlockSpec((1, tn), lambda i, j, kk: (0, j)))
        args.append(bias.reshape(1, n).astype(F32))
    outs, carried = _host_call(
        body, name, (m // tm, n // tn, nk), in_specs, [pl.BlockSpec((tm, tn), lambda i, j, kk: (i, j))],
        [jax.ShapeDtypeStruct((m, n), out_dtype)], [pltpu.VMEM((tm, tn) if nk > 1 else (SUBLANES, LANES), F32)],
        ("parallel", "parallel", "arbitrary"), args, car)
    return outs[0] if car is None else (outs[0], carried)


def _act_fwd(kind, p1, p2):
    if kind == "swiglu":
        return p1 * _sigmoid(p1) * p2
    return p1 * _sigmoid(p2)


def _act_bwd(kind, d, p1, p2):
    if kind == "swiglu":
        s = _sigmoid(p1)
        return d * p2 * s * (1.0 + p1 * (1.0 - s)), d * (p1 * s)
    s = _sigmoid(p2)
    return d * s, d * p1 * s * (1.0 - s)


def mm_dual(a, w, bias, kind, pre_dtype, act_dtype, name, car=None):
    m, k = a.shape
    w1, w2 = w if isinstance(w, tuple) else (w, w)
    n = w1.shape[1] if isinstance(w, tuple) else w.shape[1] // 2
    tm, tn = _tile(m, 512), _tile(n, 1536)
    nb = n // tn
    nb2 = 0 if isinstance(w, tuple) else nb

    def body(*refs):
        a_ref, w1_ref, w2_ref = refs[:3]
        p1_ref, p2_ref, act_ref = refs[-3:]
        av = a_ref[...].astype(BF16)
        p1 = jnp.dot(av, w1_ref[...].astype(BF16), preferred_element_type=F32)
        p2 = jnp.dot(av, w2_ref[...].astype(BF16), preferred_element_type=F32)
        if bias is not None:
            p1 = p1 + refs[3][...]
            p2 = p2 + refs[4][...]
        p1_ref[...] = p1.astype(p1_ref.dtype)
        p2_ref[...] = p2.astype(p2_ref.dtype)
        act_ref[...] = _act_fwd(kind, p1, p2).astype(act_ref.dtype)

    in_specs = [pl.BlockSpec((tm, k), lambda i, j: (i, 0)), pl.BlockSpec((k, tn), lambda i, j: (0, j)),
                pl.BlockSpec((k, tn), lambda i, j: (0, j + nb2))]
    args = [a, w1, w2]
    if bias is not None:
        b2 = bias.reshape(1, 2 * n).astype(F32)
        in_specs += [pl.BlockSpec((1, tn), lambda i, j: (0, j)), pl.BlockSpec((1, tn), lambda i, j: (0, j + nb))]
        args += [b2, b2]
    ospec = pl.BlockSpec((tm, tn), lambda i, j: (i, j))
    outs, carried = _host_call(
        body, name, (m // tm, nb), in_specs, [ospec, ospec, ospec],
        [jax.ShapeDtypeStruct((m, n), pre_dtype), jax.ShapeDtypeStruct((m, n), pre_dtype),
         jax.ShapeDtypeStruct((m, n), act_dtype)], [], ("parallel", "parallel"), args, car)
    return outs if car is None else (outs, carried)


def mm_act_bwd(dy, w, p1, p2, kind, name, car=None):
    m, k = dy.shape
    n = w.shape[0]
    tm, tn = _tile(m, 512), _tile(n, 1536)

    def body(dy_ref, w_ref, p1_ref, p2_ref, d1_ref, d2_ref):
        dact = lax.dot_general(dy_ref[...].astype(BF16), w_ref[...].astype(BF16), _DIMS["nt"],
                               preferred_element_type=F32)
        d1, d2 = _act_bwd(kind, dact, p1_ref[...].astype(F32), p2_ref[...].astype(F32))
        d1_ref[...] = d1.astype(BF16)
        d2_ref[...] = d2.astype(BF16)

    spec = pl.BlockSpec((tm, tn), lambda i, j: (i, j))
    sd = jax.ShapeDtypeStruct((m, n), BF16)
    outs, carried = _host_call(
        body, name, (m // tm, n // tn),
        [pl.BlockSpec((tm, k), lambda i, j: (i, 0)), pl.BlockSpec((tn, k), lambda i, j: (j, 0)), spec, spec],
        [spec, spec], [sd, sd], [], ("parallel", "parallel"), [dy, w, p1, p2], car)
    return outs if car is None else (outs, carried)


def dual_bwd(dact, p1, p2, kind, name):
    m, n = dact.shape
    tm, tn = _tile(m, 512), _tile(n, 512)

    def body(d_ref, p1_ref, p2_ref, d1_ref, d2_ref, s_ref):
        d1, d2 = _act_bwd(kind, d_ref[...].astype(F32), p1_ref[...].astype(F32), p2_ref[...].astype(F32))
        d1_ref[...] = d1.astype(BF16)
        d2_ref[...] = d2.astype(BF16)

        @pl.when(pl.program_id(1) == 0)
        def _():
            s_ref[...] = jnp.zeros_like(s_ref)

        s_ref[0:1, :] += jnp.sum(d1, axis=0, keepdims=True)
        s_ref[1:2, :] += jnp.sum(d2, axis=0, keepdims=True)

    spec = pl.BlockSpec((tm, tn), lambda j, i: (i, j))
    return pl.pallas_call(
        body, name=name, grid=(n // tn, m // tm), in_specs=[spec, spec, spec],
        out_specs=[spec, spec, pl.BlockSpec((SUBLANES, tn), lambda j, i: (0, j))],
        out_shape=[jax.ShapeDtypeStruct((m, n), BF16), jax.ShapeDtypeStruct((m, n), BF16),
                   jax.ShapeDtypeStruct((SUBLANES, n), F32)],
        compiler_params=_cp(("parallel", "arbitrary")),
    )(dact, p1, p2)


def _rms(x):
    r = lax.rsqrt(jnp.mean(x * x, axis=-1, keepdims=True) + EPS)
    return x * r, r


def _vec8(*rows):
    d = rows[0].shape[-1]
    out = jnp.zeros((SUBLANES, d), F32)
    for i, r in enumerate(rows):
        out = out.at[i].set(r.reshape(d).astype(F32))
    return out


def norm_mod_fwd(h, g, scale, shift, out_dtypes, name):
    s, d = h.shape
    ts = _tile(s, 512, SUBLANES)
    vec = _vec8(g, 1.0 + scale, shift)

    def body(h_ref, v_ref, *outs):
        hh, _ = _rms(h_ref[...])
        u = hh * v_ref[0:1, :] * v_ref[1:2, :] + v_ref[2:3, :]
        for o in outs:
            o[...] = u.astype(o.dtype)

    spec = pl.BlockSpec((ts, d), lambda i: (i, 0))
    return pl.pallas_call(
        body, name=name, grid=(s // ts,), in_specs=[spec, pl.BlockSpec((SUBLANES, d), lambda i: (0, 0))],
        out_specs=[spec] * len(out_dtypes), out_shape=[jax.ShapeDtypeStruct((s, d), t) for t in out_dtypes],
        compiler_params=_cp(("parallel",)),
    )(h, vec)


def norm_mod_bwd(dus, h, dh_in, g, scale, name):
    s, d = h.shape
    ts = _tile(s, 256, SUBLANES)
    vec = _vec8(g, 1.0 + scale)
    nd = len(dus)

    def body(*refs):
        du = refs[0][...].astype(F32)
        for r in refs[1:nd]:
            du = du + r[...].astype(F32)
        h_ref, dhi_ref, v_ref, dh_ref, s_ref = refs[nd:]
        hh, r = _rms(h_ref[...])
        gg, sc1 = v_ref[0:1, :], v_ref[1:2, :]
        duh = du * hh
        dhh = du * (sc1 * gg)
        dh = r * (dhh - hh * jnp.mean(duh * (sc1 * gg), axis=-1, keepdims=True))
        dh_ref[...] = dhi_ref[...] + dh

        @pl.when(pl.program_id(0) == 0)
        def _():
            s_ref[...] = jnp.zeros_like(s_ref)

        col = jnp.sum(duh, axis=0, keepdims=True)
        s_ref[0:1, :] += jnp.sum(du, axis=0, keepdims=True)
        s_ref[1:2, :] += col * gg
        s_ref[2:3, :] += col * sc1

    spec = pl.BlockSpec((ts, d), lambda i: (i, 0))
    vspec = pl.BlockSpec((SUBLANES, d), lambda i: (0, 0))
    return pl.pallas_call(
        body, name=name, grid=(s // ts,), in_specs=[spec] * (nd + 2) + [vspec], out_specs=[spec, vspec],
        out_shape=[jax.ShapeDtypeStruct((s, d), F32), jax.ShapeDtypeStruct((SUBLANES, d), F32)],
        compiler_params=_cp(("arbitrary",)),
    )(*dus, h, dh_in, vec)


def resid_fwd(h, m, g, gate, name):
    s, d = h.shape
    ts = _tile(s, 512, SUBLANES)
    vec = _vec8(g, gate)

    def body(h_ref, m_ref, v_ref, o_ref):
        mh, _ = _rms(m_ref[...])
        o_ref[...] = h_ref[...] + v_ref[1:2, :] * (mh * v_ref[0:1, :])

    spec = pl.BlockSpec((ts, d), lambda i: (i, 0))
    return pl.pallas_call(
        body, name=name, grid=(s // ts,), in_specs=[spec, spec, pl.BlockSpec((SUBLANES, d), lambda i: (0, 0))],
        out_specs=spec, out_shape=jax.ShapeDtypeStruct((s, d), F32), compiler_params=_cp(("parallel",)),
    )(h, m, vec)


def resid_bwd(dh2, m, g, gate, out_dtype, name):
    s, d = m.shape
    ts = _tile(s, 256, SUBLANES)
    vec = _vec8(g, gate)

    def body(d_ref, m_ref, v_ref, dm_ref, s_ref):
        dh = d_ref[...]
        mh, r = _rms(m_ref[...])
        gg, gt = v_ref[0:1, :], v_ref[1:2, :]
        dhm = dh * mh
        dm = r * (dh * (gt * gg) - mh * jnp.mean(dhm * (gt * gg), axis=-1, keepdims=True))
        dm_ref[...] = dm.astype(dm_ref.dtype)

        @pl.when(pl.program_id(0) == 0)
        def _():
            s_ref[...] = jnp.zeros_like(s_ref)

        col = jnp.sum(dhm, axis=0, keepdims=True)
        s_ref[0:1, :] += col * gg
        s_ref[1:2, :] += col * gt
        s_ref[2:3, :] += jnp.sum(dm, axis=0, keepdims=True)

    spec = pl.BlockSpec((ts, d), lambda i: (i, 0))
    vspec = pl.BlockSpec((SUBLANES, d), lambda i: (0, 0))
    return pl.pallas_call(
        body, name=name, grid=(s // ts,), in_specs=[spec, spec, vspec], out_specs=[spec, vspec],
        out_shape=[jax.ShapeDtypeStruct((s, d), out_dtype), jax.ShapeDtypeStruct((SUBLANES, d), F32)],
        compiler_params=_cp(("arbitrary",)),
    )(dh2, m, vec)


def loss_and_grad(h, target, name):
    s, d = h.shape
    ts = _tile(s, 512, SUBLANES)

    def body(h_ref, t_ref, l_ref, dy_ref):
        e = h_ref[...] - t_ref[...]
        dy_ref[...] = e * (1.0 / d)

        @pl.when(pl.program_id(0) == 0)
        def _():
            l_ref[...] = jnp.zeros_like(l_ref)

        l_ref[...] += (0.5 / d) * jnp.sum(e * e)

    spec = pl.BlockSpec((ts, d), lambda i: (i, 0))
    lspec = pl.BlockSpec((SUBLANES, LANES), lambda i: (0, 0))
    return pl.pallas_call(
        body, name=name, grid=(s // ts,), in_specs=[spec, spec], out_specs=[lspec, spec],
        out_shape=[jax.ShapeDtypeStruct((SUBLANES, LANES), F32), jax.ShapeDtypeStruct((s, d), F32)],
        compiler_params=_cp(("arbitrary",)),
    )(h, target)


TK = 128
N_PAIR_HEADS = LANES // HEAD_DIM
TQ_FWD, SLAB_FWD = 256, 4
TQ_BWD, SLAB_BWD = 128, 3


def _cum_matrices():
    a = jnp.arange(TK)
    ones = jnp.ones((TK, TK), F32)
    suffix = (a[:, None] > a[None, :]).astype(F32)
    prefix = (a[:, None] < a[None, :]).astype(F32)
    mk = lambda u: jnp.tile(jnp.concatenate([u, ones], axis=1), (2, 1)).astype(BF16)
    return mk(suffix), mk(prefix)


def _split_dot(x, cum):
    hi = x.astype(BF16)
    lo = (x - hi.astype(F32)).astype(BF16)
    return jnp.dot(jnp.concatenate([hi, lo], axis=1), cum, preferred_element_type=F32)


def _sb_slab(qs, k_slab, cum, nb, r, mask):
    m = qs.shape[0]
    z = lax.dot_general(qs, k_slab, _DIMS["nt"], preferred_element_type=F32)
    lb = jnp.minimum(z, 0.0) - jnp.log(1.0 + jnp.exp(-jnp.abs(z)))
    lk = lb - z
    if mask is not None:
        lk = jnp.where(mask, lk, 0.0)
    t = _split_dot(jnp.concatenate([lk[:, b * TK:(b + 1) * TK] for b in range(nb)], axis=0), cum)
    cs = [None] * nb
    for b in reversed(range(nb)):
        tb = t[b * m:(b + 1) * m]
        cs[b] = tb[:, :TK] + r
        r = r + tb[:, TK:]
    a = jnp.exp(lb + jnp.concatenate(cs, axis=1))
    if mask is not None:
        a = jnp.where(mask, a, 0.0)
    return lb, a, r


def _stack_heads(x, heads):
    return jnp.concatenate([jnp.where(hm, x, jnp.zeros_like(x)) for hm in heads], axis=0)


def _unstack_heads(xs, heads, tq):
    return jnp.where(heads[0], xs[0:tq], xs[tq:2 * tq])


def _head_masks(rows):
    lane = lax.broadcasted_iota(jnp.int32, (rows, LANES), 1)
    return [(lane >= hh * HEAD_DIM) & (lane < (hh + 1) * HEAD_DIM) for hh in range(N_PAIR_HEADS)]


def _slab_geometry(i, tq, nb):
    m = N_PAIR_HEADS * tq
    sb = jnp.maximum((i + 1) * (tq // TK) - nb, 0)
    rowq = lax.broadcasted_iota(jnp.int32, (m, nb * TK), 0) & (tq - 1)
    col = lax.broadcasted_iota(jnp.int32, (m, nb * TK), 1)
    mask = (col - rowq) < (i * tq - sb * TK)
    return sb, mask


def attn_fwd(qkv, name):
    s, d3 = qkv.shape
    d = d3 // 3
    npair = d // LANES
    tq = min(TQ_FWD, s)
    nb = SLAB_FWD
    m = N_PAIR_HEADS * tq
    scale = HEAD_DIM ** -0.5
    cum_s, _ = _cum_matrices()

    def body(q_ref, k_ref, v_ref, c_ref, o_ref):
        i = pl.program_id(1)
        cum = c_ref[...]
        heads = _head_masks(tq)
        qs = _stack_heads((q_ref[...].astype(F32) * scale).astype(BF16), heads)
        sb, mask = _slab_geometry(i, tq, nb)
        off = pl.multiple_of(sb * TK, TK)
        _, w, r = _sb_slab(qs, k_ref[pl.ds(off, nb * TK), :], cum, nb, jnp.zeros((m, TK), F32), mask)
        acc = jnp.dot(w.astype(BF16), v_ref[pl.ds(off, nb * TK), :], preferred_element_type=F32)

        def cond(c):
            return jnp.logical_and(c[0] >= 0, jnp.max(c[1]) > NEG_CUTOFF)

        def step(c):
            j, r, acc = c
            off = pl.multiple_of(j * TK, TK)
            _, w, r = _sb_slab(qs, k_ref[pl.ds(off, TK), :], cum, 1, r, None)
            return j - 1, r, acc + jnp.dot(w.astype(BF16), v_ref[pl.ds(off, TK), :], preferred_element_type=F32)

        _, _, acc = lax.while_loop(cond, step, (sb - 1, r, acc))
        o_ref[...] = _unstack_heads(acc, heads, tq).astype(o_ref.dtype)

    return pl.pallas_call(
        body, name=name, grid=(npair, s // tq),
        in_specs=[pl.BlockSpec((tq, LANES), lambda p, i: (i, p)),
                  pl.BlockSpec((s, LANES), lambda p, i: (0, npair + p)),
                  pl.BlockSpec((s, LANES), lambda p, i: (0, 2 * npair + p)),
                  pl.BlockSpec((2 * TK, 2 * TK), lambda p, i: (0, 0))],
        out_specs=pl.BlockSpec((tq, LANES), lambda p, i: (i, p)),
        out_shape=jax.ShapeDtypeStruct((s, d), BF16),
        compiler_params=_cp(("parallel", "arbitrary")),
    )(qkv, qkv, qkv, cum_s)


def attn_bwd(qkv, do, name):
    s, d3 = qkv.shape
    d = d3 // 3
    npair = d // LANES
    tq = min(TQ_BWD, s)
    nb = SLAB_BWD
    nq = s // tq
    m = N_PAIR_HEADS * tq
    scale = HEAD_DIM ** -0.5
    cum_s, cum_p = _cum_matrices()

    def body(q_ref, k_ref, v_ref, do_ref, cs_ref, cp_ref, dq_ref, dk_ref, dv_ref, dk_acc, dv_acc, e_scr, b_scr):
        i = pl.program_id(1)

        @pl.when(i == 0)
        def _():
            dk_acc[...] = jnp.zeros_like(dk_acc)
            dv_acc[...] = jnp.zeros_like(dv_acc)

        cum_suf = cs_ref[...]
        cum_pre = cp_ref[...]
        heads = _head_masks(tq)
        qs = _stack_heads((q_ref[...].astype(F32) * scale).astype(BF16), heads)
        dos = _stack_heads(do_ref[...], heads)

        def left(off, n, r, mask):
            rows = pl.ds(off, n * TK)
            lb, a, r = _sb_slab(qs, k_ref[rows, :], cum_suf, n, r, mask)
            da = lax.dot_general(dos, v_ref[rows, :], _DIMS["nt"], preferred_element_type=F32)
            dv_acc[rows, :] += lax.dot_general(a.astype(BF16), dos, _DIMS["tn"], preferred_element_type=F32)
            return da * a, jnp.exp(lb), r

        def right(off, n, e, beta, pe, mask):
            rows = pl.ds(off, n * TK)
            t = _split_dot(jnp.concatenate([e[:, b * TK:(b + 1) * TK] for b in range(n)], axis=0), cum_pre)
            ps = [None] * n
            for b in range(n):
                tb = t[b * m:(b + 1) * m]
                ps[b] = tb[:, :TK] + pe
                pe = pe + tb[:, TK:]
            dz = e * (1.0 - beta) - beta * jnp.concatenate(ps, axis=1)
            if mask is not None:
                dz = jnp.where(mask, dz, 0.0)
            dzb = dz.astype(BF16)
            dk_acc[rows, :] += lax.dot_general(dzb, qs, _DIMS["tn"], preferred_element_type=F32)
            return pe, jnp.dot(dzb, k_ref[rows, :], preferred_element_type=F32)

        sb, mask = _slab_geometry(i, tq, nb)
        off0 = pl.multiple_of(sb * TK, TK)
        e0, beta0, r = left(off0, nb, jnp.zeros((m, TK), F32), mask)

        def cond(c):
            return jnp.logical_and(c[0] >= 0, jnp.max(c[1]) > NEG_CUTOFF)

        def tail_left(c):
            j, r = c
            e_scr[j], b_scr[j], r = left(pl.multiple_of(j * TK, TK), 1, r, None)
            return j - 1, r

        jend, _ = lax.while_loop(cond, tail_left, (sb - 1, r))

        def tail_right(j, c):
            pe, dq = c
            pe, dqj = right(pl.multiple_of(j * TK, TK), 1, e_scr[j], b_scr[j], pe, None)
            return pe, dq + dqj

        pe, dq = lax.fori_loop(jend + 1, sb, tail_right, (jnp.zeros((m, TK), F32), jnp.zeros((m, LANES), F32)))
        _, dq0 = right(off0, nb, e0, beta0, pe, mask)
        dq_ref[...] = (_unstack_heads(dq + dq0, heads, tq) * scale).astype(dq_ref.dtype)

        @pl.when(i == nq - 1)
        def _():
            dk_ref[...] = dk_acc[...].astype(dk_ref.dtype)
            dv_ref[...] = dv_acc[...].astype(dv_ref.dtype)

    qspec = pl.BlockSpec((tq, LANES), lambda p, i: (i, p))
    full = lambda base: pl.BlockSpec((s, LANES), lambda p, i: (0, base + p))
    cspec = pl.BlockSpec((2 * TK, 2 * TK), lambda p, i: (0, 0))
    sd = jax.ShapeDtypeStruct((s, d), BF16)
    return pl.pallas_call(
        body, name=name, grid=(npair, nq),
        in_specs=[qspec, full(npair), full(2 * npair), qspec, cspec, cspec],
        out_specs=[qspec, full(0), full(0)], out_shape=[sd, sd, sd],
        scratch_shapes=[pltpu.VMEM((s, LANES), F32), pltpu.VMEM((s, LANES), F32),
                        pltpu.VMEM((s // TK, m, TK), F32), pltpu.VMEM((s // TK, m, TK), F32)],
        compiler_params=_cp(("parallel", "arbitrary")),
    )(qkv, qkv, qkv, do, cum_s, cum_p)


HALO = 32
CONV_ROWS = 128


def _ln_swish(hc, g, b):
    mu = jnp.mean(hc, axis=-1, keepdims=True)
    xc = hc - mu
    rstd = lax.rsqrt(jnp.mean(xc * xc, axis=-1, keepdims=True) + EPS)
    xh = xc * rstd
    hn = xh * g + b
    return xh, rstd, hn


def conv_mid_fwd(x, w_dw, b_dw, ln_g, ln_b, name):
    s, d = x.shape
    width = w_dw.shape[0]
    ts = _tile(s, 256, CONV_ROWS)
    base = HALO - (width - 1)
    wpad = jnp.zeros((HALO, d), F32).at[:width].set(w_dw.astype(F32))
    vec = _vec8(b_dw, ln_g, ln_b)

    def body(x_ref, w_ref, v_ref, hc_ref, hs_ref, win):
        i = pl.program_id(0)

        @pl.when(i == 0)
        def _():
            win[0:HALO, :] = jnp.zeros((HALO, d), F32)

        @pl.when(i > 0)
        def _():
            win[0:HALO, :] = win[ts:ts + HALO, :]

        win[HALO:HALO + ts, :] = x_ref[...]
        for rc in range(ts // CONV_ROWS):
            for lc in range(d // LANES):
                cols = slice(lc * LANES, (lc + 1) * LANES)
                acc = jnp.broadcast_to(v_ref[0:1, cols], (CONV_ROWS, LANES))
                for k in range(width):
                    acc = acc + w_ref[k:k + 1, cols] * win[pl.ds(rc * CONV_ROWS + base + k, CONV_ROWS), cols]
                hc_ref[rc * CONV_ROWS:(rc + 1) * CONV_ROWS, cols] = acc
        _, _, hn = _ln_swish(hc_ref[...], v_ref[1:2, :], v_ref[2:3, :])
        hs_ref[...] = (hn * _sigmoid(hn)).astype(hs_ref.dtype)

    spec = pl.BlockSpec((ts, d), lambda i: (i, 0))
    return pl.pallas_call(
        body, name=name, grid=(s // ts,),
        in_specs=[spec, pl.BlockSpec((HALO, d), lambda i: (0, 0)), pl.BlockSpec((SUBLANES, d), lambda i: (0, 0))],
        out_specs=[spec, spec], out_shape=[jax.ShapeDtypeStruct((s, d), F32), jax.ShapeDtypeStruct((s, d), BF16)],
        scratch_shapes=[pltpu.VMEM((ts + HALO, d), F32)],
        compiler_params=_cp(("arbitrary",)),
    )(x, wpad, vec)


def conv_mid_bwd(dhs, hc, x, w_dw, ln_g, ln_b, name):
    s, d = x.shape
    width = w_dw.shape[0]
    ts = _tile(s, 256, CONV_ROWS)
    nt = s // ts
    base = HALO - (width - 1)
    wpad = jnp.zeros((HALO, d), F32).at[:width].set(w_dw.astype(F32))
    vec = _vec8(ln_g, ln_b)

    def body(dhs_ref, hc_ref, x_ref, xh_ref, w_ref, v_ref, dx_ref, dw_ref, s_ref, dwin, xwin):
        i = pl.program_id(0)

        @pl.when(i == 0)
        def _():
            dwin[ts:ts + HALO, :] = jnp.zeros((HALO, d), F32)
            dw_ref[...] = jnp.zeros_like(dw_ref)
            s_ref[...] = jnp.zeros_like(s_ref)

        @pl.when(i > 0)
        def _():
            dwin[ts:ts + HALO, :] = dwin[0:HALO, :]

        @pl.when(i == nt - 1)
        def _():
            xwin[0:HALO, :] = jnp.zeros((HALO, d), F32)

        @pl.when(i < nt - 1)
        def _():
            xwin[0:HALO, :] = xh_ref[...]

        xwin[HALO:HALO + ts, :] = x_ref[...]
        g = v_ref[0:1, :]
        xh, rstd, hn = _ln_swish(hc_ref[...], g, v_ref[1:2, :])
        sig = _sigmoid(hn)
        dhn = dhs_ref[...].astype(F32) * (sig * (1.0 + hn * (1.0 - sig)))
        dxh = dhn * g
        dhc = rstd * (dxh - jnp.mean(dxh, axis=-1, keepdims=True) - xh * jnp.mean(dxh * xh, axis=-1, keepdims=True))
        dwin[0:ts, :] = dhc
        s_ref[0:1, :] += jnp.sum(dhc, axis=0, keepdims=True)
        s_ref[1:2, :] += jnp.sum(dhn * xh, axis=0, keepdims=True)
        s_ref[2:3, :] += jnp.sum(dhn, axis=0, keepdims=True)
        for rc in range(ts // CONV_ROWS):
            for lc in range(d // LANES):
                cols = slice(lc * LANES, (lc + 1) * LANES)
                r0 = rc * CONV_ROWS
                dch = dwin[r0:r0 + CONV_ROWS, cols]
                acc = jnp.zeros((CONV_ROWS, LANES), F32)
                for k in range(width):
                    acc = acc + w_ref[k:k + 1, cols] * dwin[pl.ds(r0 + (width - 1) - k, CONV_ROWS), cols]
                    dw_ref[k:k + 1, cols] += jnp.sum(dch * xwin[pl.ds(r0 + base + k, CONV_ROWS), cols], axis=0,
                                                     keepdims=True)
                dx_ref[r0:r0 + CONV_ROWS, cols] = acc

    rev = pl.BlockSpec((ts, d), lambda i: (nt - 1 - i, 0))
    halo = pl.BlockSpec((HALO, d), lambda i: (jnp.maximum((nt - 1 - i) * (ts // HALO) - 1, 0), 0))
    vspec = pl.BlockSpec((SUBLANES, d), lambda i: (0, 0))
    wspec = pl.BlockSpec((HALO, d), lambda i: (0, 0))
    return pl.pallas_call(
        body, name=name, grid=(nt,), in_specs=[rev, rev, rev, halo, wspec, vspec],
        out_specs=[rev, wspec, vspec],
        out_shape=[jax.ShapeDtypeStruct((s, d), F32), jax.ShapeDtypeStruct((HALO, d), F32),
                   jax.ShapeDtypeStruct((SUBLANES, d), F32)],
        scratch_shapes=[pltpu.VMEM((ts + HALO, d), F32), pltpu.VMEM((ts + HALO, d), F32)],
        compiler_params=_cp(("arbitrary",)),
    )(dhs, hc, x, x, wpad, vec)


S5_KB = 256
SCAN_LANES = 256
SCAN_UNROLL = 4
GELU_C = math.sqrt(2.0 / math.pi)
GELU_A = 0.044715


def _gelu(x):
    return 0.5 * x * (1.0 + jnp.tanh(GELU_C * (x + GELU_A * x * x * x)))


def _gelu_grad(x):
    th = jnp.tanh(GELU_C * (x + GELU_A * x * x * x))
    return 0.5 * (1.0 + th) + 0.5 * x * (1.0 - th * th) * GELU_C * (1.0 + 3.0 * GELU_A * x * x)


def _s5_discretize(lam_re, lam_im, log_dt, b_re, b_im):
    dt = jnp.exp(log_dt)[:, None]
    mag = jnp.exp(lam_re * dt)
    ar, ai = mag * jnp.cos(lam_im * dt), mag * jnp.sin(lam_im * dt)
    den = lam_re * lam_re + lam_im * lam_im
    er = ((ar - 1) * lam_re + ai * lam_im) / den
    ei = (ai * lam_re - (ar - 1) * lam_im) / den
    bbr = er[..., None] * b_re - ei[..., None] * b_im
    bbi = er[..., None] * b_im + ei[..., None] * b_re
    return ar, ai, bbr, bbi


def _blockdiag(w, nkb):
    g, r, c = w.shape
    gpb = g // nkb
    eye = jnp.eye(gpb, dtype=w.dtype)
    return jnp.einsum("kgrc,gh->kgrhc", w.reshape(nkb, gpb, r, c), eye).reshape(nkb, gpb * r, gpb * c)


def _blockdiag_extract(m, g):
    nkb = m.shape[0]
    gpb = g // nkb
    r, c = m.shape[1] // gpb, m.shape[2] // gpb
    eye = jnp.eye(gpb, dtype=m.dtype)
    return jnp.einsum("kgrhc,gh->kgrc", m.reshape(nkb, gpb, r, gpb, c), eye).reshape(g, r, c)


def _scan_powers(ar, ai, reverse):
    ar = ar.reshape(-1)
    ai = (-ai if reverse else ai).reshape(-1)
    cmul = lambda x, y: (x[0] * y[0] - x[1] * y[1], x[0] * y[1] + x[1] * y[0])
    a1 = (ar, ai)
    a2 = cmul(a1, a1)
    a4 = cmul(a2, a2)
    r = jnp.arange(SUBLANES)[:, None]
    rows = []
    for sft, p in ((1, a1), (2, a2), (4, a4)):
        keep = (r + sft <= SUBLANES - 1) if reverse else (r >= sft)
        rows += [jnp.where(keep, p[0][None, :], 0.0), jnp.where(keep, p[1][None, :], 0.0)]
    pows = [a1]
    for _ in range(SUBLANES - 1):
        pows.append(cmul(pows[-1], a1))
    if reverse:
        pows = pows[::-1]
    rows += [jnp.stack([p[0] for p in pows]), jnp.stack([p[1] for p in pows])]
    return jnp.concatenate(rows, axis=0).astype(F32)


def _scan_tile(sr, si, pw_ref, car, nrg, reverse):
    nsb = sr.shape[1]
    ch = min(SCAN_LANES, nsb)
    nch = nsb // ch
    row = 0 if reverse else SUBLANES - 1

    unroll = SCAN_UNROLL if nrg % SCAN_UNROLL == 0 else 1

    def step(t, carry):
        carry = list(carry)
        for u in range(unroll):
            g = t * unroll + u
            rg = (nrg - 1 - g) if reverse else g
            off = pl.multiple_of(rg * SUBLANES, SUBLANES)
            for c in range(nch):
                cols = slice(c * ch, (c + 1) * ch)
                cr, ci = carry[2 * c], carry[2 * c + 1]
                br = sr[pl.ds(off, SUBLANES), cols]
                bi = si[pl.ds(off, SUBLANES), cols]
                for idx, sft in enumerate((1, 2, 4)):
                    sh = SUBLANES - sft if reverse else sft
                    tr = pltpu.roll(br, sh, axis=0)
                    ti = pltpu.roll(bi, sh, axis=0)
                    mr = pw_ref[16 * idx:16 * idx + 8, cols]
                    mi = pw_ref[16 * idx + 8:16 * idx + 16, cols]
                    br, bi = br + mr * tr - mi * ti, bi + mr * ti + mi * tr
                apr, api = pw_ref[48:56, cols], pw_ref[56:64, cols]
                xr = br + apr * cr - api * ci
                xi = bi + apr * ci + api * cr
                sr[pl.ds(off, SUBLANES), cols] = xr
                si[pl.ds(off, SUBLANES), cols] = xi
                carry[2 * c] = jnp.broadcast_to(xr[row:row + 1, :], xr.shape)
                carry[2 * c + 1] = jnp.broadcast_to(xi[row:row + 1, :], xi.shape)
        return tuple(carry)

    init = []
    for c in range(nch):
        cols = slice(c * ch, (c + 1) * ch)
        init += [car[0:SUBLANES, cols], car[SUBLANES:2 * SUBLANES, cols]]
    fin = lax.fori_loop(0, nrg // unroll, step, tuple(init))
    for c in range(nch):
        cols = slice(c * ch, (c + 1) * ch)
        car[0:SUBLANES, cols] = fin[2 * c]
        car[SUBLANES:2 * SUBLANES, cols] = fin[2 * c + 1]


def s5_fwd(u, wb_r, wb_i, wc_r, wc_i, pw, d_skip, name):
    s, d = u.shape
    nkb, kb, nsb = wb_r.shape
    ts = _tile(s, 256, SUBLANES)
    dvec = _vec8(d_skip)

    def body(u_ref, wbr, wbi, wcr, wci, pw_ref, dv_ref, xr_ref, xi_ref, yy_ref, g_ref, sr, si, car):
        @pl.when(pl.program_id(1) == 0)
        def _():
            car[...] = jnp.zeros_like(car)

        uu = u_ref[...]
        ub = uu.astype(BF16)
        sr[...] = jnp.dot(ub, wbr[...], preferred_element_type=F32)
        si[...] = jnp.dot(ub, wbi[...], preferred_element_type=F32)
        _scan_tile(sr, si, pw_ref, car, ts // SUBLANES, False)
        xr, xi = sr[...], si[...]
        xr_ref[...] = xr
        xi_ref[...] = xi
        y = (jnp.dot(xr.astype(BF16), wcr[...], preferred_element_type=F32)
             + jnp.dot(xi.astype(BF16), wci[...], preferred_element_type=F32) + dv_ref[0:1, :] * uu)
        yy_ref[...] = y
        g_ref[...] = _gelu(y).astype(g_ref.dtype)

    cspec = pl.BlockSpec((ts, kb), lambda k, i: (i, k))
    sspec = pl.BlockSpec((ts, nsb), lambda k, i: (i, k))
    wbspec = pl.BlockSpec((None, kb, nsb), lambda k, i: (k, 0, 0))
    wcspec = pl.BlockSpec((None, nsb, kb), lambda k, i: (k, 0, 0))
    ns = nkb * nsb
    return pl.pallas_call(
        body, name=name, grid=(nkb, s // ts),
        in_specs=[cspec, wbspec, wbspec, wcspec, wcspec, pl.BlockSpec((64, nsb), lambda k, i: (0, k)),
                  pl.BlockSpec((SUBLANES, kb), lambda k, i: (0, k))],
        out_specs=[sspec, sspec, cspec, cspec],
        out_shape=[jax.ShapeDtypeStruct((s, ns), F32), jax.ShapeDtypeStruct((s, ns), F32),
                   jax.ShapeDtypeStruct((s, d), F32), jax.ShapeDtypeStruct((s, d), BF16)],
        scratch_shapes=[pltpu.VMEM((ts, nsb), F32), pltpu.VMEM((ts, nsb), F32), pltpu.VMEM((2 * SUBLANES, nsb), F32)],
        compiler_params=_cp(("parallel", "arbitrary")),
    )(u, wb_r, wb_i, wc_r, wc_i, pw, dvec)


def s5_bwd(dg, yy, u, xr, xi, wb_r, wb_i, wc_r, wc_i, pwb, d_skip, name):
    s, d = u.shape
    nkb, kb, nsb = wb_r.shape
    ns = nkb * nsb
    ts = _tile(s, 256, SUBLANES)
    nt = s // ts
    dvec = _vec8(d_skip)

    def body(dg_ref, yy_ref, u_ref, xr_ref, xi_ref, xrp_ref, xip_ref, wbr, wbi, wcr, wci, pw_ref, dv_ref,
             du_ref, dwbr, dwbi, dwcr, dwci, da_ref, dd_ref, sr, si, car):
        i = pl.program_id(1)

        @pl.when(i == 0)
        def _():
            car[...] = jnp.zeros_like(car)
            for r in (dwbr, dwbi, dwcr, dwci, da_ref, dd_ref):
                r[...] = jnp.zeros_like(r)

        uu = u_ref[...]
        dyy = dg_ref[...] * _gelu_grad(yy_ref[...])
        dd_ref[0:1, :] += jnp.sum(dyy * uu, axis=0, keepdims=True)
        dyb = dyy.astype(BF16)
        sr[...] = lax.dot_general(dyb, wcr[...], _DIMS["nt"], preferred_element_type=F32)
        si[...] = lax.dot_general(dyb, wci[...], _DIMS["nt"], preferred_element_type=F32)
        _scan_tile(sr, si, pw_ref, car, ts // SUBLANES, True)
        gr, gi = sr[...], si[...]
        grb, gib = gr.astype(BF16), gi.astype(BF16)
        xrt, xit = xr_ref[...], xi_ref[...]
        dwcr[...] += lax.dot_general(xrt.astype(BF16), dyb, _DIMS["tn"], preferred_element_type=F32)
        dwci[...] += lax.dot_general(xit.astype(BF16), dyb, _DIMS["tn"], preferred_element_type=F32)
        ub = uu.astype(BF16)
        dwbr[...] += lax.dot_general(ub, grb, _DIMS["tn"], preferred_element_type=F32)
        dwbi[...] += lax.dot_general(ub, gib, _DIMS["tn"], preferred_element_type=F32)
        du_ref[...] = (lax.dot_general(grb, wbr[...], _DIMS["nt"], preferred_element_type=F32)
                       + lax.dot_general(gib, wbi[...], _DIMS["nt"], preferred_element_type=F32)
                       + dyy * dv_ref[0:1, :])
        has_prev = (i < nt - 1).astype(F32)
        rowid = lax.broadcasted_iota(jnp.int32, (ts, nsb), 0)
        pr = jnp.broadcast_to(xrp_ref[SUBLANES - 1:SUBLANES, :] * has_prev, (ts, nsb))
        pi = jnp.broadcast_to(xip_ref[SUBLANES - 1:SUBLANES, :] * has_prev, (ts, nsb))
        xpr = jnp.where(rowid == 0, pr, pltpu.roll(xrt, 1, axis=0))
        xpi = jnp.where(rowid == 0, pi, pltpu.roll(xit, 1, axis=0))
        da_ref[0:1, :] += jnp.sum(gr * xpr + gi * xpi, axis=0, keepdims=True)
        da_ref[1:2, :] += jnp.sum(gi * xpr - gr * xpi, axis=0, keepdims=True)

    cspec = pl.BlockSpec((ts, kb), lambda k, i: (nt - 1 - i, k))
    sspec = pl.BlockSpec((ts, nsb), lambda k, i: (nt - 1 - i, k))
    pspec = pl.BlockSpec((SUBLANES, nsb), lambda k, i: (jnp.maximum((nt - 1 - i) * (ts // SUBLANES) - 1, 0), k))
    wbspec = pl.BlockSpec((None, kb, nsb), lambda k, i: (k, 0, 0))
    wcspec = pl.BlockSpec((None, nsb, kb), lambda k, i: (k, 0, 0))
    v8s = pl.BlockSpec((SUBLANES, nsb), lambda k, i: (0, k))
    v8c = pl.BlockSpec((SUBLANES, kb), lambda k, i: (0, k))
    return pl.pallas_call(
        body, name=name, grid=(nkb, nt),
        in_specs=[cspec, cspec, cspec, sspec, sspec, pspec, pspec, wbspec, wbspec, wcspec, wcspec,
                  pl.BlockSpec((64, nsb), lambda k, i: (0, k)), v8c],
        out_specs=[cspec, wbspec, wbspec, wcspec, wcspec, v8s, v8c],
        out_shape=[jax.ShapeDtypeStruct((s, d), F32),
                   jax.ShapeDtypeStruct((nkb, kb, nsb), F32), jax.ShapeDtypeStruct((nkb, kb, nsb), F32),
                   jax.ShapeDtypeStruct((nkb, nsb, kb), F32), jax.ShapeDtypeStruct((nkb, nsb, kb), F32),
                   jax.ShapeDtypeStruct((SUBLANES, ns), F32), jax.ShapeDtypeStruct((SUBLANES, d), F32)],
        scratch_shapes=[pltpu.VMEM((ts, nsb), F32), pltpu.VMEM((ts, nsb), F32), pltpu.VMEM((2 * SUBLANES, nsb), F32)],
        compiler_params=_cp(("parallel", "arbitrary")),
    )(dg, yy, u, xr, xi, xr, xi, wb_r, wb_i, wc_r, wc_i, pwb, dvec)


def s5_operands(lam_re, lam_im, log_dt, b_re, b_im, c_re, c_im, d):
    ar, ai, bbr, bbi = _s5_discretize(lam_re, lam_im, log_dt, b_re, b_im)
    nkb = max(d // S5_KB, 1)
    wb_r = _blockdiag(bbr.transpose(0, 2, 1), nkb).astype(BF16)
    wb_i = _blockdiag(bbi.transpose(0, 2, 1), nkb).astype(BF16)
    wc_r = _blockdiag(c_re.transpose(0, 2, 1), nkb).astype(BF16)
    wc_i = _blockdiag(-c_im.transpose(0, 2, 1), nkb).astype(BF16)
    return wb_r, wb_i, wc_r, wc_i, _scan_powers(ar, ai, False), _scan_powers(ar, ai, True)


MESH = pl.DeviceIdType.MESH
HBM_SPEC = pl.BlockSpec(memory_space=pltpu.HBM)


def _me():
    return 4 * lax.axis_index("x") + 2 * lax.axis_index("y") + lax.axis_index("c")


N_COPIES = N_DEV - 1


class Carried:
    def __init__(self, kind, operands):
        self.kind, self.operands = kind, list(operands)
        self.n = len(self.operands)

    def call_args(self):
        shapes = [jax.ShapeDtypeStruct((N_DEV,) + tuple(a.shape[-2:]), a.dtype) for a in self.operands]
        scratch = [pltpu.SemaphoreType.DMA((N_COPIES * self.n,)), pltpu.SemaphoreType.DMA((N_COPIES * self.n,)),
                   pltpu.SemaphoreType.DMA((self.n,))]
        return [HBM_SPEC] * self.n, shapes, scratch

    def _plan(self, t, x_ref, out_ref, send, recv, loc):
        x, y, c = lax.axis_index("x"), lax.axis_index("y"), lax.axis_index("c")
        me = 4 * x + 2 * y + c

        def rdma(k, src, dst, to):
            return pltpu.make_async_remote_copy(src_ref=src, dst_ref=dst, send_sem=send.at[N_COPIES * t + k],
                                                recv_sem=recv.at[N_COPIES * t + k], device_id=to, device_id_type=MESH)

        if self.kind == "ex":
            peers = [(1 - x if k & 4 else x, 1 - y if k & 2 else y, 1 - c if k & 1 else c) for k in range(1, N_DEV)]
            sends = [rdma(k, x_ref.at[4 * px + 2 * py + pc], out_ref.at[me], (px, py, pc))
                     for k, (px, py, pc) in enumerate(peers)]
            local = pltpu.make_async_copy(x_ref.at[me], out_ref.at[me], loc.at[t])
            return dict(local=local, first=sends, relay_on=[], relays=[], arrive=sends)
        sibling = (x, y, 1 - c)
        chips = [(1 - x, y), (x, 1 - y), (1 - x, 1 - y)]
        slot = lambda px, py, pc: out_ref.at[4 * px + 2 * py + pc]
        first = [rdma(0, x_ref, slot(x, y, c), sibling)]
        first += [rdma(1 + j, x_ref, slot(x, y, c), (*chip, c)) for j, chip in enumerate(chips)]
        relay_on = [rdma(1 + j, slot(*chip, c), slot(*chip, c), (x, y, c)) for j, chip in enumerate(chips)]
        relays = [rdma(4 + j, slot(*chip, c), slot(*chip, c), sibling) for j, chip in enumerate(chips)]
        arrive = [rdma(0, slot(*sibling), slot(*sibling), (x, y, c))]
        arrive += [rdma(4 + j, slot(*chip, 1 - c), slot(*chip, 1 - c), (x, y, c)) for j, chip in enumerate(chips)]
        local = pltpu.make_async_copy(x_ref, slot(x, y, c), loc.at[t])
        return dict(local=local, first=first, relay_on=relay_on, relays=relays, arrive=arrive)

    def _plans(self, refs):
        xs, outs, (send, recv, loc) = refs[:self.n], refs[self.n:2 * self.n], refs[2 * self.n:]
        return [self._plan(t, xs[t], outs[t], send, recv, loc) for t in range(self.n)]

    def begin(self, refs):
        for p in self._plans(refs):
            p["local"].start()
            for cp in p["first"]:
                cp.start()

    def finish(self, refs):
        plans = self._plans(refs)
        for p in plans:
            for landed, relay in zip(p["relay_on"], p["relays"]):
                landed.wait_recv()
                relay.start()
        for p in plans:
            for cp in p["arrive"]:
                cp.wait_recv()
            for cp in p["first"] + p["relays"]:
                cp.wait_send()
            p["local"].wait()


def _carry_refs(refs, n_in, n_out, car):
    if car is None:
        return list(refs), None
    n = car.n
    host = list(refs[:n_in]) + list(refs[n_in + n:n_in + n + n_out]) + list(refs[n_in + 2 * n + n_out:-3])
    return host, list(refs[n_in:n_in + n]) + list(refs[n_in + n + n_out:n_in + 2 * n + n_out]) + list(refs[-3:])


def _carry_steps(car, crefs, grid):
    if car is None:
        return lambda: None
    ids = [pl.program_id(a) for a in range(len(grid))]
    first = functools.reduce(jnp.logical_and, [i == 0 for i in ids])
    last = functools.reduce(jnp.logical_and, [i == g - 1 for i, g in zip(ids, grid)])

    @pl.when(first)
    def _():
        car.begin(crefs)

    def after():
        @pl.when(last)
        def _():
            car.finish(crefs)

    return after


def communicate(kind, operands, name):
    car = Carried(kind, operands)
    specs, shapes, scratch = car.call_args()

    def body(*refs):
        car.begin(refs)
        car.finish(refs)

    return pl.pallas_call(body, name=name, in_specs=specs, out_specs=specs, out_shape=shapes,
                          scratch_shapes=scratch)(*car.operands)


def all_gather(shard, name):
    return communicate("ag", [shard], name)[0]


def sum_slots(parts, name):
    _, m, n = parts.shape
    tm = _tile(m, 256, SUBLANES)

    def body(p_ref, o_ref):
        acc = p_ref[0].astype(F32)
        for q in range(1, N_DEV):
            acc = acc + p_ref[q].astype(F32)
        o_ref[...] = acc

    return pl.pallas_call(
        body, name=name, grid=(m // tm,), in_specs=[pl.BlockSpec((N_DEV, tm, n), lambda i: (0, i, 0))],
        out_specs=pl.BlockSpec((tm, n), lambda i: (i, 0)), out_shape=jax.ShapeDtypeStruct((m, n), F32),
        compiler_params=_cp(("parallel",)),
    )(parts)


PACK_COLS = 1024
PACK_ROWS = 16


def _pack_flat(pieces, dtype):
    lead = pieces[0].shape[:-1]
    flat = jnp.concatenate([p.astype(dtype) for p in pieces], axis=-1)
    unit = PACK_COLS * PACK_ROWS
    total = -(-flat.shape[-1] // unit) * unit
    flat = jnp.pad(flat, [(0, 0)] * len(lead) + [(0, total - flat.shape[-1])])
    return flat.reshape(*lead, total // PACK_COLS, PACK_COLS)


def _unpack_flat(packed, sizes):
    lead = packed.shape[:-2]
    flat = packed.reshape(*lead, -1)
    out, off = [], 0
    for n in sizes:
        out.append(flat[..., off:off + n])
        off += n
    return out


def mod_fwd(c_all, w_mod, b_cols, name):
    nl, d, n = w_mod.shape

    def body(c_ref, w_ref, b_ref, o_ref):
        cv = c_ref[...]
        sc = (cv * _sigmoid(cv)).astype(BF16)
        o_ref[...] = jnp.dot(sc, w_ref[...].astype(BF16), preferred_element_type=F32) + b_ref[...]

    return pl.pallas_call(
        body, name=name, grid=(nl,),
        in_specs=[pl.BlockSpec((N_DEV, d), lambda l: (0, 0)), pl.BlockSpec((None, d, n), lambda l: (l, 0, 0)),
                  pl.BlockSpec((None, 1, n), lambda l: (l, 0, 0))],
        out_specs=pl.BlockSpec((None, N_DEV, n), lambda l: (l, 0, 0)),
        out_shape=jax.ShapeDtypeStruct((nl, N_DEV, n), F32), compiler_params=_cp(("parallel",)),
    )(c_all, w_mod, b_cols.reshape(nl, 1, n))


def mod_bwd(c_all, dmod_cols, name):
    nl, _, n = dmod_cols.shape
    d = c_all.shape[1]

    def body(c_ref, g_ref, o_ref):
        cv = c_ref[...]
        sc = (cv * _sigmoid(cv)).astype(BF16)
        o_ref[...] = lax.dot_general(sc, g_ref[...].astype(BF16), _DIMS["tn"], preferred_element_type=F32)

    return pl.pallas_call(
        body, name=name, grid=(nl,),
        in_specs=[pl.BlockSpec((N_DEV, d), lambda l: (0, 0)), pl.BlockSpec((None, N_DEV, n), lambda l: (l, 0, 0))],
        out_specs=pl.BlockSpec((None, d, n), lambda l: (l, 0, 0)),
        out_shape=jax.ShapeDtypeStruct((nl, d, n), F32), compiler_params=_cp(("parallel",)),
    )(c_all, dmod_cols)


def adamw(w, g, m, v, name):
    r, c = w.shape
    tr = _tile(r, 512, SUBLANES)
    c1 = 1.0 - ADAM_B1 ** ADAM_STEP
    c2 = 1.0 - ADAM_B2 ** ADAM_STEP

    def body(w_ref, g_ref, m_ref, v_ref, d_ref, nm_ref, nv_ref):
        gg = g_ref[...]
        nm = ADAM_B1 * m_ref[...] + (1.0 - ADAM_B1) * gg
        nv = ADAM_B2 * v_ref[...] + (1.0 - ADAM_B2) * (gg * gg)
        nm_ref[...] = nm
        nv_ref[...] = nv
        d_ref[...] = -ADAM_LR * ((nm / c1) / (jnp.sqrt(nv / c2) + ADAM_EPS) + ADAM_WD * w_ref[...])

    spec = pl.BlockSpec((tr, c), lambda i: (i, 0))
    sd = jax.ShapeDtypeStruct((r, c), F32)
    return pl.pallas_call(
        body, name=name, grid=(r // tr,), in_specs=[spec] * 4, out_specs=[spec] * 3, out_shape=[sd, sd, sd],
        compiler_params=_cp(("parallel",)),
    )(w, g, m, v)


WEIGHTS = ["norm_g", "w_mod", "b_mod", "sb_w_qkv", "sb_w_o", "s5_lam_re", "s5_lam_im", "s5_log_dt", "s5_b_re",
           "s5_b_im", "s5_c_re", "s5_c_im", "s5_d", "s5_w_glu", "s5_b_glu", "cv_w_pw1", "cv_b_pw1", "cv_w_dw",
           "cv_b_dw", "cv_ln_g", "cv_ln_b", "cv_w_pw2", "cv_b_pw2", "ffn_w_gate", "ffn_w_up", "ffn_w_down"]
BIG = ["w_mod", "sb_w_qkv", "sb_w_o", "s5_w_glu", "cv_w_pw1", "cv_w_pw2", "ffn_w_gate", "ffn_w_up", "ffn_w_down"]
SMALL_SHARDED = ["norm_g", "cv_b_pw1", "cv_w_dw", "cv_b_dw", "cv_ln_g", "cv_ln_b", "cv_b_pw2"]
SMALL = [n for n in WEIGHTS if n not in BIG]
FFN_KEYS = ["gate", "up"]


def _unshard_last(part, local_shape):
    a = jnp.moveaxis(part.reshape((N_DEV,) + tuple(local_shape)), 0, -2)
    return a.reshape(tuple(local_shape[:-1]) + (N_DEV * local_shape[-1],))


def kernel(x, c, norm_g, w_mod, b_mod, sb_w_qkv, sb_w_o, s5_lam_re, s5_lam_im, s5_log_dt, s5_b_re, s5_b_im, s5_c_re, s5_c_im, s5_d, s5_w_glu, s5_b_glu, cv_w_pw1, cv_b_pw1, cv_w_dw, cv_b_dw, cv_ln_g, cv_ln_b, cv_w_pw2, cv_b_pw2, ffn_w_gate, ffn_w_up, ffn_w_down, loss_target, m_norm_g, m_w_mod, m_b_mod, m_sb_w_qkv, m_sb_w_o, m_s5_lam_re, m_s5_lam_im, m_s5_log_dt, m_s5_b_re, m_s5_b_im, m_s5_c_re, m_s5_c_im, m_s5_d, m_s5_w_glu, m_s5_b_glu, m_cv_w_pw1, m_cv_b_pw1, m_cv_w_dw, m_cv_b_dw, m_cv_ln_g, m_cv_ln_b, m_cv_w_pw2, m_cv_b_pw2, m_ffn_w_gate, m_ffn_w_up, m_ffn_w_down, v_norm_g, v_w_mod, v_b_mod, v_sb_w_qkv, v_sb_w_o, v_s5_lam_re, v_s5_lam_im, v_s5_log_dt, v_s5_b_re, v_s5_b_im, v_s5_c_re, v_s5_c_im, v_s5_d, v_s5_w_glu, v_s5_b_glu, v_cv_w_pw1, v_cv_b_pw1, v_cv_w_dw, v_cv_b_dw, v_cv_ln_g, v_cv_ln_b, v_cv_w_pw2, v_cv_b_pw2, v_ffn_w_gate, v_ffn_w_up, v_ffn_w_down):
    p = dict(locals())
    me = _me()
    s, d = x.shape[1], x.shape[2]
    depth = norm_g.shape[0]
    h = x.reshape(s, d)
    target = loss_target.reshape(s, d)

    pieces = [p[n].reshape(-1) for n in SMALL_SHARDED] + [c.reshape(-1)]
    parts = _unpack_flat(all_gather(_pack_flat(pieces, F32), "ag_small"), [q.shape[0] for q in pieces])
    full = {n: _unshard_last(part, p[n].shape) for n, part in zip(SMALL_SHARDED, parts)}
    c_all = parts[-1]

    nmod = w_mod.shape[2]
    b_cols = lax.dynamic_slice_in_dim(b_mod, me * nmod, nmod, axis=1)
    mod_cols = mod_fwd(c_all, w_mod, b_cols, "mod_fwd")
    g_mod = all_gather(mod_cols.reshape(depth * N_DEV, nmod), "ag_mod").reshape(N_DEV, depth, N_DEV, nmod)
    mod = jnp.moveaxis(lax.dynamic_index_in_dim(g_mod, me, axis=2, keepdims=False), 0, 1).reshape(depth, N_DEV * nmod)
    ng = full["norm_g"]

    def layer_pieces(l):
        kind, j = l % 3, l // 3
        if kind == 0:
            ps = [("qkv", sb_w_qkv[j], "col"), ("o", sb_w_o[j], "row")]
        elif kind == 1:
            ps = [("glu", s5_w_glu[j], "col")]
        else:
            ps = [("pw1", cv_w_pw1[j], "col"), ("pw2", cv_w_pw2[j], "row")]
        return ps + [("gate", ffn_w_gate[l], "col"), ("up", ffn_w_up[l], "col"), ("down", ffn_w_down[l], "row")]

    def weight_gather(l):
        return Carried("ag", [a.astype(BF16) for _, a, _ in layer_pieces(l)])

    def gathered_weights(l, got):
        out = {}
        for (key, a, how), blk in zip(layer_pieces(l), got):
            r, cc = a.shape
            out[key] = blk.transpose(1, 0, 2).reshape(r, N_DEV * cc) if how == "col" else blk.reshape(N_DEV * r, cc)
        return out

    def grad_exchange(l, grads, keys):
        slabs = []
        for key, a, how in layer_pieces(l):
            if key in keys:
                r, cc = a.shape
                g = grads[key]
                slabs.append(g.reshape(r, N_DEV, cc).transpose(1, 0, 2) if how == "col" else g.reshape(N_DEV, r, cc))
        return Carried("ex", slabs)

    def store_grads(l, keys, got):
        names = {"qkv": "sb_w_qkv", "o": "sb_w_o", "glu": "s5_w_glu", "pw1": "cv_w_pw1", "pw2": "cv_w_pw2",
                 "gate": "ffn_w_gate", "up": "ffn_w_up", "down": "ffn_w_down"}
        for key, parts in zip(keys, got):
            idx = l if key in ("gate", "up", "down") else l // 3
            gbig[names[key]][idx] = sum_slots(parts, f"rs_sum_{key}{l}")

    saved = []
    gbig = {n: [None] * p[n].shape[0] for n in BIG if n != "w_mod"}
    w_next = gathered_weights(0, communicate("ag", weight_gather(0).operands, "ag_w0"))
    for l in range(depth):
        kind, j = l % 3, l // 3
        w = w_next
        sh_m, sc_m, g_m, sh_f, sc_f, g_f = jnp.split(mod[l], 6)
        st = {"w": w, "h": h}
        if kind == 0:
            (u,) = norm_mod_fwd(h, ng[l, 0], sc_m, sh_m, [BF16], f"nm_a{l}")
            qkv = mm([(u, w["qkv"])], "nn", BF16, name=f"qkv{l}")
            o = attn_fwd(qkv, f"attn_fwd{l}")
            m = mm([(o, w["o"])], "nn", F32, name=f"attn_o{l}")
            st.update(u=u, qkv=qkv, o=o)
        elif kind == 1:
            (u,) = norm_mod_fwd(h, ng[l, 0], sc_m, sh_m, [F32], f"nm_a{l}")
            ops = s5_operands(s5_lam_re[j], s5_lam_im[j], s5_log_dt[j], s5_b_re[j], s5_b_im[j], s5_c_re[j],
                              s5_c_im[j], d)
            xr, xi, yy, gl = s5_fwd(u, *ops[:5], s5_d[j], f"s5_fwd{l}")
            p1, p2, m = mm_dual(gl, w["glu"], s5_b_glu[j], "glu", F32, F32, f"s5_glu{l}")
            st.update(u=u, ops=ops, xr=xr, xi=xi, yy=yy, gl=gl, p1=p1, p2=p2)
        else:
            (u,) = norm_mod_fwd(h, ng[l, 0], sc_m, sh_m, [BF16], f"nm_a{l}")
            p1, p2, hg = mm_dual(u, w["pw1"], full["cv_b_pw1"][j], "glu", F32, F32, f"cv_pw1{l}")
            hc, hs = conv_mid_fwd(hg, full["cv_w_dw"][j], full["cv_b_dw"][j], full["cv_ln_g"][j],
                                  full["cv_ln_b"][j], f"cv_mid{l}")
            m = mm([(hs, w["pw2"])], "nn", F32, bias=full["cv_b_pw2"][j], name=f"cv_pw2{l}")
            st.update(u=u, p1=p1, p2=p2, hg=hg, hc=hc, hs=hs)
        h2 = resid_fwd(h, m, ng[l, 1], g_m, f"res_a{l}")
        (u2,) = norm_mod_fwd(h2, ng[l, 2], sc_f, sh_f, [BF16], f"nm_f{l}")
        w_gu = (w["gate"], w["up"])
        if l + 1 < depth:
            (f1, f2, z), got = mm_dual(u2, w_gu, None, "swiglu", BF16, BF16, f"ffn_up{l}", car=weight_gather(l + 1))
            w_next = gathered_weights(l + 1, got)
        else:
            f1, f2, z = mm_dual(u2, w_gu, None, "swiglu", BF16, BF16, f"ffn_up{l}")
        f = mm([(z, w["down"])], "nn", F32, name=f"ffn_down{l}")
        h = resid_fwd(h2, f, ng[l, 3], g_f, f"res_f{l}")
        st.update(m=m, h2=h2, u2=u2, f1=f1, f2=f2, z=z, f=f)
        saved.append(st)

    loss_arr, dh = loss_and_grad(h, target, "loss")
    loss = lax.psum(loss_arr[0, 0], AXES)

    nl_sb, nl_s5, nl_cv = sb_w_qkv.shape[0], s5_w_glu.shape[0], cv_w_pw1.shape[0]
    pending = None
    dng = [None] * depth
    dmod = [None] * depth
    gs5 = {n: [None] * nl_s5 for n in SMALL if n.startswith("s5_")}
    gcv = {n: [None] * nl_cv for n in SMALL if n.startswith("cv_")}
    for l in reversed(range(depth)):
        kind, j = l % 3, l // 3
        st = saved[l]
        w = st["w"]
        sh_m, sc_m, g_m, sh_f, sc_f, g_f = jnp.split(mod[l], 6)
        gw = {}
        df, s_rf = resid_bwd(dh, st["f"], ng[l, 3], g_f, BF16, f"res_f_bwd{l}")
        if pending is None:
            d1, d2 = mm_act_bwd(df, w["down"], st["f1"], st["f2"], "swiglu", f"ffn_dz{l}")
        else:
            (d1, d2), got = mm_act_bwd(df, w["down"], st["f1"], st["f2"], "swiglu", f"ffn_dz{l}",
                                       car=grad_exchange(pending[0], pending[1], FFN_KEYS))
            store_grads(pending[0], FFN_KEYS, got)
        gw["down"] = mm([(st["z"], df)], "tn", BF16, name=f"ffn_dwd{l}")
        if pending is None:
            du2 = mm([(d1, w["gate"]), (d2, w["up"])], "nt", F32, name=f"ffn_du{l}")
        else:
            rest = [key for key, _, _ in layer_pieces(pending[0]) if key not in FFN_KEYS]
            du2, got = mm([(d1, w["gate"]), (d2, w["up"])], "nt", F32, name=f"ffn_du{l}",
                          car=grad_exchange(pending[0], pending[1], rest))
            store_grads(pending[0], rest, got)
        gw["gate"] = mm([(st["u2"], d1)], "tn", BF16, name=f"ffn_dwg{l}")
        gw["up"] = mm([(st["u2"], d2)], "tn", BF16, name=f"ffn_dwu{l}")
        dh2, s_nf = norm_mod_bwd([du2], st["h2"], dh, ng[l, 2], sc_f, f"nm_f_bwd{l}")
        dm, s_rm = resid_bwd(dh2, st["m"], ng[l, 1], g_m, F32 if kind == 1 else BF16, f"res_a_bwd{l}")
        if kind == 0:
            do = mm([(dm, w["o"])], "nt", BF16, name=f"attn_do{l}")
            gw["o"] = mm([(st["o"], dm)], "tn", BF16, name=f"attn_dwo{l}")
            dq, dk, dv = attn_bwd(st["qkv"], do, f"attn_bwd{l}")
            wq = w["qkv"]
            dus = [mm([(dq, wq[:, :d]), (dk, wq[:, d:2 * d]), (dv, wq[:, 2 * d:])], "nt", F32, name=f"qkv_du{l}")]
            gw["qkv"] = jnp.concatenate([mm([(st["u"], t)], "tn", BF16, name=f"qkv_dw{l}_{i}")
                                         for i, t in enumerate((dq, dk, dv))], axis=1)
        elif kind == 1:
            d1, d2, cs = dual_bwd(dm, st["p1"], st["p2"], "glu", f"s5_glu_bwd{l}")
            gs5["s5_b_glu"][j] = jnp.concatenate([cs[0], cs[1]])
            wg = w["glu"]
            dgl = mm([(d1, wg[:, :d]), (d2, wg[:, d:])], "nt", F32, name=f"s5_dgl{l}")
            gw["glu"] = jnp.concatenate([mm([(st["gl"], t)], "tn", BF16, name=f"s5_dwglu{l}_{i}")
                                         for i, t in enumerate((d1, d2))], axis=1)
            ops = st["ops"]
            du, dwbr, dwbi, dwcr, dwci, da, dd = s5_bwd(dgl, st["yy"], st["u"], st["xr"], st["xi"], *ops[:4],
                                                        ops[5], s5_d[j], f"s5_bwd{l}")
            dus = [du]
            ngrp = s5_lam_re.shape[1]
            ext = lambda t: _blockdiag_extract(t, ngrp).transpose(0, 2, 1)
            _, disc_vjp = jax.vjp(_s5_discretize, s5_lam_re[j], s5_lam_im[j], s5_log_dt[j], s5_b_re[j], s5_b_im[j])
            shp = s5_lam_re[j].shape
            dlr, dli, dldt, dbr, dbi = disc_vjp((da[0].reshape(shp), da[1].reshape(shp), ext(dwbr), ext(dwbi)))
            for n, t in (("s5_lam_re", dlr), ("s5_lam_im", dli), ("s5_log_dt", dldt), ("s5_b_re", dbr),
                         ("s5_b_im", dbi), ("s5_c_re", ext(dwcr)), ("s5_c_im", -ext(dwci)), ("s5_d", dd[0])):
                gs5[n][j] = t
        else:
            dhs = mm([(dm, w["pw2"])], "nt", BF16, name=f"cv_dhs{l}")
            gw["pw2"] = mm([(st["hs"], dm)], "tn", BF16, name=f"cv_dwpw2{l}")
            dhg, dwdw, s_cv = conv_mid_bwd(dhs, st["hc"], st["hg"], full["cv_w_dw"][j], full["cv_ln_g"][j],
                                           full["cv_ln_b"][j], f"cv_mid_bwd{l}")
            d1, d2, cs = dual_bwd(dhg, st["p1"], st["p2"], "glu", f"cv_glu_bwd{l}")
            wp = w["pw1"]
            dus = [mm([(d1, wp[:, :d]), (d2, wp[:, d:])], "nt", F32, name=f"cv_du{l}")]
            gw["pw1"] = jnp.concatenate([mm([(st["u"], t)], "tn", BF16, name=f"cv_dwpw1{l}_{i}")
                                         for i, t in enumerate((d1, d2))], axis=1)
            for n, t in (("cv_b_pw1", jnp.concatenate([cs[0], cs[1]])), ("cv_w_dw", dwdw[:cv_w_dw.shape[1]]),
                         ("cv_b_dw", s_cv[0]), ("cv_ln_g", s_cv[1]), ("cv_ln_b", s_cv[2]), ("cv_b_pw2", s_rm[2])):
                gcv[n][j] = t
        dh, s_nm = norm_mod_bwd(dus, st["h"], dh2, ng[l, 0], sc_m, f"nm_a_bwd{l}")
        dng[l] = jnp.stack([s_nm[2], s_rm[1], s_nf[2], s_rf[1]])
        dmod[l] = jnp.concatenate([s_nm[0], s_nm[1], s_rm[0], s_nf[0], s_nf[1], s_rf[0]])
        pending = (l, gw)
    keys = [key for key, _, _ in layer_pieces(pending[0])]
    store_grads(pending[0], keys, communicate("ex", grad_exchange(pending[0], pending[1], keys).operands, "rs_x_last"))

    local = {"norm_g": jnp.stack(dng), "b_mod": jnp.stack(dmod)}
    local.update({n: jnp.stack(t) for n, t in gs5.items()})
    local.update({n: jnp.stack(t) for n, t in gcv.items()})
    pieces = [local[n].reshape(-1) for n in SMALL]
    sizes = [q.shape[0] for q in pieces]
    gathered = all_gather(_pack_flat(pieces, F32), "ag_grads")
    sums = _unpack_flat(sum_slots(gathered, "sum_grads"), sizes)
    grads = {}
    for n, t in zip(SMALL, sums):
        t = t.reshape(local[n].shape)
        if n in SMALL_SHARDED:
            nsh = p[n].shape[-1]
            t = lax.dynamic_slice_in_dim(t, me * nsh, nsh, axis=t.ndim - 1)
        grads[n] = t.reshape(p[n].shape)
    dmod_all = _unpack_flat(gathered, sizes)[SMALL.index("b_mod")].reshape(N_DEV, depth, N_DEV * nmod)
    dmod_cols = jnp.moveaxis(lax.dynamic_slice_in_dim(dmod_all, me * nmod, nmod, axis=2), 0, 1)
    grads["w_mod"] = mod_bwd(c_all, dmod_cols, "mod_bwd")
    for n in gbig:
        grads[n] = jnp.stack(gbig[n])

    delta, new_m, new_v = {}, {}, {}
    for n in BIG:
        shp = p[n].shape
        two = lambda t: t.reshape(-1, shp[-1])
        delta[n], new_m[n], new_v[n] = (t.reshape(shp) for t in
                                        adamw(two(p[n]), two(grads[n]), two(p["m_" + n]), two(p["v_" + n]), f"adamw_{n}"))
    sizes = [p[n].size for n in SMALL]
    packs = [_pack_flat([t[n].reshape(-1) for n in SMALL], F32)
             for t in (p, grads, {n: p["m_" + n] for n in SMALL}, {n: p["v_" + n] for n in SMALL})]
    for res, out in zip(adamw(*packs, "adamw_small"), (delta, new_m, new_v)):
        for n, t in zip(SMALL, _unpack_flat(res, sizes)):
            out[n] = t.reshape(p[n].shape)

    return (loss, dh.reshape(x.shape), *[grads[n] for n in WEIGHTS], *[delta[n] for n in WEIGHTS],
            *[new_m[n] for n in WEIGHTS], *[new_v[n] for n in WEIGHTS])
```

```python
import functools
import math

import jax
import jax.numpy as jnp
from jax import lax
from jax.experimental import pallas as pl
from jax.experimental.pallas import tpu as pltpu

F32, BF16 = jnp.float32, jnp.bfloat16
N_DEV = 8
AXES = ("x", "y", "c")
EPS = 1e-6
HEAD_DIM = 64
LANES = 128
SUBLANES = 8
VMEM_LIMIT = 56 * 1024 * 1024
S5_GROUP = 16
ADAM_LR, ADAM_B1, ADAM_B2, ADAM_EPS, ADAM_WD, ADAM_STEP = 0.001, 0.9, 0.999, 1e-08, 0.01, 10
NEG_CUTOFF = -104.0


def _cp(sem):
    return pltpu.CompilerParams(dimension_semantics=sem, vmem_limit_bytes=VMEM_LIMIT)


def _tile(n, cap, mult=LANES):
    best = None
    for t in range(mult, min(n, cap) + 1, mult):
        if n % t == 0:
            best = t
    return best if best is not None else n


def _sigmoid(x):
    return 1.0 / (1.0 + jnp.exp(-x))


_DIMS = {"nn": (((1,), (0,)), ((), ())), "nt": (((1,), (1,)), ((), ())), "tn": (((0,), (0,)), ((), ()))}


CAPS_FULL_K = (1024, 1536, 4096)
CAPS_PAIRS = (512, 1024, 1408)
CAPS_TN = (4096, 1536, 1024)


def _host_call(body, name, grid, in_specs, out_specs, out_shape, scratch, sem, args, car):
    if car is None:
        return pl.pallas_call(body, name=name, grid=grid, in_specs=in_specs, out_specs=out_specs, out_shape=out_shape,
                              scratch_shapes=scratch, compiler_params=_cp(sem))(*args), None
    n_in, n_out = len(in_specs), len(out_specs)
    cspecs, cshapes, cscratch = car.call_args()

    def wrapped(*refs):
        host, crefs = _carry_refs(refs, n_in, n_out, car)
        after = _carry_steps(car, crefs, grid)
        body(*host)
        after()

    res = pl.pallas_call(
        wrapped, name=name, grid=grid, in_specs=in_specs + cspecs, out_specs=out_specs + cspecs,
        out_shape=out_shape + cshapes, scratch_shapes=scratch + cscratch,
        compiler_params=_cp(("arbitrary",) * len(grid)))(*args, *car.operands)
    return res[:n_out], res[n_out:]


def mm(pairs, mode, out_dtype=F32, bias=None, name="mm", caps=None, car=None):
    a0, b0 = pairs[0]
    if mode == "nn":
        (m, k), n = a0.shape, b0.shape[1]
    elif mode == "nt":
        (m, k), n = a0.shape, b0.shape[0]
    else:
        (k, m), n = a0.shape, b0.shape[1]
    if caps is None:
        caps = CAPS_TN if mode == "tn" else (CAPS_FULL_K if len(pairs) == 1 or k <= 1024 else CAPS_PAIRS)
    tm, tn, tk = _tile(m, caps[0]), _tile(n, caps[1]), _tile(k, caps[2])
    nk = k // tk
    npairs = len(pairs)
    dims = _DIMS[mode]

    def body(*refs):
        ins = refs[:2 * npairs]
        bias_ref = refs[2 * npairs] if bias is not None else None
        o_ref, acc = refs[-2], refs[-1]
        kk = pl.program_id(2)
        part = None
        for p in range(npairs):
            d = lax.dot_general(ins[2 * p][...].astype(BF16), ins[2 * p + 1][...].astype(BF16), dims,
                                preferred_element_type=F32)
            part = d if part is None else part + d

        def finish(r):
            if bias_ref is not None:
                r = r + bias_ref[...]
            o_ref[...] = r.astype(o_ref.dtype)

        if nk == 1:
            finish(part)
        else:
            @pl.when(kk == 0)
            def _():
                acc[...] = part

            @pl.when(kk > 0)
            def _():
                acc[...] += part

            @pl.when(kk == nk - 1)
            def _():
                finish(acc[...])

    if mode == "nn":
        sa, sb = pl.BlockSpec((tm, tk), lambda i, j, kk: (i, kk)), pl.BlockSpec((tk, tn), lambda i, j, kk: (kk, j))
    elif mode == "nt":
        sa, sb = pl.BlockSpec((tm, tk), lambda i, j, kk: (i, kk)), pl.BlockSpec((tn, tk), lambda i, j, kk: (j, kk))
    else:
        sa, sb = pl.BlockSpec((tk, tm), lambda i, j, kk: (kk, i)), pl.BlockSpec((tk, tn), lambda i, j, kk: (kk, j))
    in_specs, args = [], []
    for a, b in pairs:
        in_specs += [sa, sb]
        args += [a, b]
    if bias is not None:
        in_specs.append(pl.BlockSpec((1, tn), lambda i, j, kk: (0, j)))
        args.append(bias.reshape(1, n).astype(F32))
    outs, carried = _host_call(
        body, name, (m // tm, n // tn, nk), in_specs, [pl.BlockSpec((tm, tn), lambda i, j, kk: (i, j))],
        [jax.ShapeDtypeStruct((m, n), out_dtype)], [pltpu.VMEM((tm, tn) if nk > 1 else (SUBLANES, LANES), F32)],
        ("parallel", "parallel", "arbitrary"), args, car)
    return outs[0] if car is None else (outs[0], carried)


def _act_fwd(kind, p1, p2):
    if kind == "swiglu":
        return p1 * _sigmoid(p1) * p2
    return p1 * _sigmoid(p2)


def _act_bwd(kind, d, p1, p2):
    if kind == "swiglu":
        s = _sigmoid(p1)
        return d * p2 * s * (1.0 + p1 * (1.0 - s)), d * (p1 * s)
    s = _sigmoid(p2)
    return d * s, d * p1 * s * (1.0 - s)


def mm_dual(a, w, bias, kind, pre_dtype, act_dtype, name, car=None):
    m, k = a.shape
    w1, w2 = w if isinstance(w, tuple) else (w, w)
    n = w1.shape[1] if isinstance(w, tuple) else w.shape[1] // 2
    tm, tn = _tile(m, 512), _tile(n, 1536)
    nb = n // tn
    nb2 = 0 if isinstance(w, tuple) else nb

    def body(*refs):
        a_ref, w1_ref, w2_ref = refs[:3]
        p1_ref, p2_ref, act_ref = refs[-3:]
        av = a_ref[...].astype(BF16)
        p1 = jnp.dot(av, w1_ref[...].astype(BF16), preferred_element_type=F32)
        p2 = jnp.dot(av, w2_ref[...].astype(BF16), preferred_element_type=F32)
        if bias is not None:
            p1 = p1 + refs[3][...]
            p2 = p2 + refs[4][...]
        p1_ref[...] = p1.astype(p1_ref.dtype)
        p2_ref[...] = p2.astype(p2_ref.dtype)
        act_ref[...] = _act_fwd(kind, p1, p2).astype(act_ref.dtype)

    in_specs = [pl.BlockSpec((tm, k), lambda i, j: (i, 0)), pl.BlockSpec((k, tn), lambda i, j: (0, j)),
                pl.BlockSpec((k, tn), lambda i, j: (0, j + nb2))]
    args = [a, w1, w2]
    if bias is not None:
        b2 = bias.reshape(1, 2 * n).astype(F32)
        in_specs += [pl.BlockSpec((1, tn), lambda i, j: (0, j)), pl.BlockSpec((1, tn), lambda i, j: (0, j + nb))]
        args += [b2, b2]
    ospec = pl.BlockSpec((tm, tn), lambda i, j: (i, j))
    outs, carried = _host_call(
        body, name, (m // tm, nb), in_specs, [ospec, ospec, ospec],
        [jax.ShapeDtypeStruct((m, n), pre_dtype), jax.ShapeDtypeStruct((m, n), pre_dtype),
         jax.ShapeDtypeStruct((m, n), act_dtype)], [], ("parallel", "parallel"), args, car)
    return outs if car is None else (outs, carried)


def mm_act_bwd(dy, w, p1, p2, kind, name, car=None):
    m, k = dy.shape
    n = w.shape[0]
    tm, tn = _tile(m, 512), _tile(n, 1536)

    def body(dy_ref, w_ref, p1_ref, p2_ref, d1_ref, d2_ref):
        dact = lax.dot_general(dy_ref[...].astype(BF16), w_ref[...].astype(BF16), _DIMS["nt"],
                               preferred_element_type=F32)
        d1, d2 = _act_bwd(kind, dact, p1_ref[...].astype(F32), p2_ref[...].astype(F32))
        d1_ref[...] = d1.astype(BF16)
        d2_ref[...] = d2.astype(BF16)

    spec = pl.BlockSpec((tm, tn), lambda i, j: (i, j))
    sd = jax.ShapeDtypeStruct((m, n), BF16)
    outs, carried = _host_call(
        body, name, (m // tm, n // tn),
        [pl.BlockSpec((tm, k), lambda i, j: (i, 0)), pl.BlockSpec((tn, k), lambda i, j: (j, 0)), spec, spec],
        [spec, spec], [sd, sd], [], ("parallel", "parallel"), [dy, w, p1, p2], car)
    return outs if car is None else (outs, carried)


def dual_bwd(dact, p1, p2, kind, name):
    m, n = dact.shape
    tm, tn = _tile(m, 512), _tile(n, 512)

    def body(d_ref, p1_ref, p2_ref, d1_ref, d2_ref, s_ref):
        d1, d2 = _act_bwd(kind, d_ref[...].astype(F32), p1_ref[...].astype(F32), p2_ref[...].astype(F32))
        d1_ref[...] = d1.astype(BF16)
        d2_ref[...] = d2.astype(BF16)

        @pl.when(pl.program_id(1) == 0)
        def _():
            s_ref[...] = jnp.zeros_like(s_ref)

        s_ref[0:1, :] += jnp.sum(d1, axis=0, keepdims=True)
        s_ref[1:2, :] += jnp.sum(d2, axis=0, keepdims=True)

    spec = pl.BlockSpec((tm, tn), lambda j, i: (i, j))
    return pl.pallas_call(
        body, name=name, grid=(n // tn, m // tm), in_specs=[spec, spec, spec],
        out_specs=[spec, spec, pl.BlockSpec((SUBLANES, tn), lambda j, i: (0, j))],
        out_shape=[jax.ShapeDtypeStruct((m, n), BF16), jax.ShapeDtypeStruct((m, n), BF16),
                   jax.ShapeDtypeStruct((SUBLANES, n), F32)],
        compiler_params=_cp(("parallel", "arbitrary")),
    )(dact, p1, p2)


def _rms(x):
    r = lax.rsqrt(jnp.mean(x * x, axis=-1, keepdims=True) + EPS)
    return x * r, r


def _vec8(*rows):
    d = rows[0].shape[-1]
    out = jnp.zeros((SUBLANES, d), F32)
    for i, r in enumerate(rows):
        out = out.at[i].set(r.reshape(d).astype(F32))
    return out


def norm_mod_fwd(h, g, scale, shift, out_dtypes, name):
    s, d = h.shape
    ts = _tile(s, 512, SUBLANES)
    vec = _vec8(g, 1.0 + scale, shift)

    def body(h_ref, v_ref, *outs):
        hh, _ = _rms(h_ref[...])
        u = hh * v_ref[0:1, :] * v_ref[1:2, :] + v_ref[2:3, :]
        for o in outs:
            o[...] = u.astype(o.dtype)

    spec = pl.BlockSpec((ts, d), lambda i: (i, 0))
    return pl.pallas_call(
        body, name=name, grid=(s // ts,), in_specs=[spec, pl.BlockSpec((SUBLANES, d), lambda i: (0, 0))],
        out_specs=[spec] * len(out_dtypes), out_shape=[jax.ShapeDtypeStruct((s, d), t) for t in out_dtypes],
        compiler_params=_cp(("parallel",)),
    )(h, vec)


def norm_mod_bwd(dus, h, dh_in, g, scale, name):
    s, d = h.shape
    ts = _tile(s, 256, SUBLANES)
    vec = _vec8(g, 1.0 + scale)
    nd = len(dus)

    def body(*refs):
        du = refs[0][...].astype(F32)
        for r in refs[1:nd]:
            du = du + r[...].astype(F32)
        h_ref, dhi_ref, v_ref, dh_ref, s_ref = refs[nd:]
        hh, r = _rms(h_ref[...])
        gg, sc1 = v_ref[0:1, :], v_ref[1:2, :]
        duh = du * hh
        dhh = du * (sc1 * gg)
        dh = r * (dhh - hh * jnp.mean(duh * (sc1 * gg), axis=-1, keepdims=True))
        dh_ref[...] = dhi_ref[...] + dh

        @pl.when(pl.program_id(0) == 0)
        def _():
            s_ref[...] = jnp.zeros_like(s_ref)

        col = jnp.sum(duh, axis=0, keepdims=True)
        s_ref[0:1, :] += jnp.sum(du, axis=0, keepdims=True)
        s_ref[1:2, :] += col * gg
        s_ref[2:3, :] += col * sc1

    spec = pl.BlockSpec((ts, d), lambda i: (i, 0))
    vspec = pl.BlockSpec((SUBLANES, d), lambda i: (0, 0))
    return pl.pallas_call(
        body, name=name, grid=(s // ts,), in_specs=[spec] * (nd + 2) + [vspec], out_specs=[spec, vspec],
        out_shape=[jax.ShapeDtypeStruct((s, d), F32), jax.ShapeDtypeStruct((SUBLANES, d), F32)],
        compiler_params=_cp(("arbitrary",)),
    )(*dus, h, dh_in, vec)


def resid_fwd(h, m, g, gate, name):
    s, d = h.shape
    ts = _tile(s, 512, SUBLANES)
    vec = _vec8(g, gate)

    def body(h_ref, m_ref, v_ref, o_ref):
        mh, _ = _rms(m_ref[...])
        o_ref[...] = h_ref[...] + v_ref[1:2, :] * (mh * v_ref[0:1, :])

    spec = pl.BlockSpec((ts, d), lambda i: (i, 0))
    return pl.pallas_call(
        body, name=name, grid=(s // ts,), in_specs=[spec, spec, pl.BlockSpec((SUBLANES, d), lambda i: (0, 0))],
        out_specs=spec, out_shape=jax.ShapeDtypeStruct((s, d), F32), compiler_params=_cp(("parallel",)),
    )(h, m, vec)


def resid_bwd(dh2, m, g, gate, out_dtype, name):
    s, d = m.shape
    ts = _tile(s, 256, SUBLANES)
    vec = _vec8(g, gate)

    def body(d_ref, m_ref, v_ref, dm_ref, s_ref):
        dh = d_ref[...]
        mh, r = _rms(m_ref[...])
        gg, gt = v_ref[0:1, :], v_ref[1:2, :]
        dhm = dh * mh
        dm = r * (dh * (gt * gg) - mh * jnp.mean(dhm * (gt * gg), axis=-1, keepdims=True))
        dm_ref[...] = dm.astype(dm_ref.dtype)

        @pl.when(pl.program_id(0) == 0)
        def _():
            s_ref[...] = jnp.zeros_like(s_ref)

        col = jnp.sum(dhm, axis=0, keepdims=True)
        s_ref[0:1, :] += col * gg
        s_ref[1:2, :] += col * gt
        s_ref[2:3, :] += jnp.sum(dm, axis=0, keepdims=True)

    spec = pl.BlockSpec((ts, d), lambda i: (i, 0))
    vspec = pl.BlockSpec((SUBLANES, d), lambda i: (0, 0))
    return pl.pallas_call(
        body, name=name, grid=(s // ts,), in_specs=[spec, spec, vspec], out_specs=[spec, vspec],
        out_shape=[jax.ShapeDtypeStruct((s, d), out_dtype), jax.ShapeDtypeStruct((SUBLANES, d), F32)],
        compiler_params=_cp(("arbitrary",)),
    )(dh2, m, vec)


def loss_and_grad(h, target, name):
    s, d = h.shape
    ts = _tile(s, 512, SUBLANES)

    def body(h_ref, t_ref, l_ref, dy_ref):
        e = h_ref[...] - t_ref[...]
        dy_ref[...] = e * (1.0 / d)

        @pl.when(pl.program_id(0) == 0)
        def _():
            l_ref[...] = jnp.zeros_like(l_ref)

        l_ref[...] += (0.5 / d) * jnp.sum(e * e)

    spec = pl.BlockSpec((ts, d), lambda i: (i, 0))
    lspec = pl.BlockSpec((SUBLANES, LANES), lambda i: (0, 0))
    return pl.pallas_call(
        body, name=name, grid=(s // ts,), in_specs=[spec, spec], out_specs=[lspec, spec],
        out_shape=[jax.ShapeDtypeStruct((SUBLANES, LANES), F32), jax.ShapeDtypeStruct((s, d), F32)],
        compiler_params=_cp(("arbitrary",)),
    )(h, target)


TK = 128
N_PAIR_HEADS = LANES // HEAD_DIM
TQ_ATTN, SLAB_ATTN = 128, 3


def _cum_matrices():
    a = jnp.arange(TK)
    ones = jnp.ones((TK, TK), F32)
    suffix = (a[:, None] > a[None, :]).astype(F32)
    prefix = (a[:, None] < a[None, :]).astype(F32)
    mk = lambda u: jnp.tile(jnp.concatenate([u, ones], axis=1), (2, 1)).astype(BF16)
    return mk(suffix), mk(prefix)


def _split_dot(x, cum):
    hi = x.astype(BF16)
    lo = (x - hi.astype(F32)).astype(BF16)
    return jnp.dot(jnp.concatenate([hi, lo], axis=1), cum, preferred_element_type=F32)


def _sb_slab(qs, k_slab, cum, nb, r, mask):
    m = qs.shape[0]
    z = lax.dot_general(qs, k_slab, _DIMS["nt"], preferred_element_type=F32)
    lb = jnp.minimum(z, 0.0) - jnp.log(1.0 + jnp.exp(-jnp.abs(z)))
    lk = lb - z
    if mask is not None:
        lk = jnp.where(mask, lk, 0.0)
    t = _split_dot(jnp.concatenate([lk[:, b * TK:(b + 1) * TK] for b in range(nb)], axis=0), cum)
    cs = [None] * nb
    for b in reversed(range(nb)):
        tb = t[b * m:(b + 1) * m]
        cs[b] = tb[:, :TK] + r
        r = r + tb[:, TK:]
    a = jnp.exp(lb + jnp.concatenate(cs, axis=1))
    if mask is not None:
        a = jnp.where(mask, a, 0.0)
    return lb, a, r


def _stack_heads(x, heads):
    return jnp.concatenate([jnp.where(hm, x, jnp.zeros_like(x)) for hm in heads], axis=0)


def _unstack_heads(xs, heads, tq):
    return jnp.where(heads[0], xs[0:tq], xs[tq:2 * tq])


def _head_masks(rows):
    lane = lax.broadcasted_iota(jnp.int32, (rows, LANES), 1)
    return [(lane >= hh * HEAD_DIM) & (lane < (hh + 1) * HEAD_DIM) for hh in range(N_PAIR_HEADS)]


def _slab_geometry(i, tq, nb):
    m = N_PAIR_HEADS * tq
    sb = jnp.maximum((i + 1) * (tq // TK) - nb, 0)
    rowq = lax.broadcasted_iota(jnp.int32, (m, nb * TK), 0) & (tq - 1)
    col = lax.broadcasted_iota(jnp.int32, (m, nb * TK), 1)
    mask = (col - rowq) < (i * tq - sb * TK)
    return sb, mask


def attn_fwd(qkv, name):
    s, d3 = qkv.shape
    d = d3 // 3
    npair = d // LANES
    tq = min(TQ_ATTN, s)
    nb = SLAB_ATTN
    m = N_PAIR_HEADS * tq
    scale = HEAD_DIM ** -0.5
    cum_s, _ = _cum_matrices()

    def body(q_ref, k_ref, v_ref, c_ref, o_ref, w_ref, b_ref, r_ref):
        i = pl.program_id(1)
        cum = c_ref[...]
        heads = _head_masks(tq)
        qs = _stack_heads((q_ref[...].astype(F32) * scale).astype(BF16), heads)
        sb, mask = _slab_geometry(i, tq, nb)
        off = pl.multiple_of(sb * TK, TK)
        lb, w, r = _sb_slab(qs, k_ref[pl.ds(off, nb * TK), :], cum, nb, jnp.zeros((m, TK), F32), mask)
        wb = w.astype(BF16)
        w_ref[...] = wb
        b_ref[...] = lb.astype(BF16)
        r_ref[...] = r
        acc = jnp.dot(wb, v_ref[pl.ds(off, nb * TK), :], preferred_element_type=F32)

        def cond(c):
            return jnp.logical_and(c[0] >= 0, jnp.max(c[1]) > NEG_CUTOFF)

        def step(c):
            j, r, acc = c
            off = pl.multiple_of(j * TK, TK)
            _, w, r = _sb_slab(qs, k_ref[pl.ds(off, TK), :], cum, 1, r, None)
            return j - 1, r, acc + jnp.dot(w.astype(BF16), v_ref[pl.ds(off, TK), :], preferred_element_type=F32)

        _, _, acc = lax.while_loop(cond, step, (sb - 1, r, acc))
        o_ref[...] = _unstack_heads(acc, heads, tq).astype(o_ref.dtype)

    return pl.pallas_call(
        body, name=name, grid=(npair, s // tq),
        in_specs=[pl.BlockSpec((tq, LANES), lambda p, i: (i, p)),
                  pl.BlockSpec((s, LANES), lambda p, i: (0, npair + p)),
                  pl.BlockSpec((s, LANES), lambda p, i: (0, 2 * npair + p)),
                  pl.BlockSpec((2 * TK, 2 * TK), lambda p, i: (0, 0))],
        out_specs=[pl.BlockSpec((tq, LANES), lambda p, i: (i, p)),
                   pl.BlockSpec((None, None, m, nb * TK), lambda p, i: (p, i, 0, 0)),
                   pl.BlockSpec((None, None, m, nb * TK), lambda p, i: (p, i, 0, 0)),
                   pl.BlockSpec((None, None, m, TK), lambda p, i: (p, i, 0, 0))],
        out_shape=[jax.ShapeDtypeStruct((s, d), BF16),
                   jax.ShapeDtypeStruct((npair, s // tq, m, nb * TK), BF16),
                   jax.ShapeDtypeStruct((npair, s // tq, m, nb * TK), BF16),
                   jax.ShapeDtypeStruct((npair, s // tq, m, TK), F32)],
        compiler_params=_cp(("parallel", "arbitrary")),
    )(qkv, qkv, qkv, cum_s)


def attn_bwd(qkv, do, saved, name):
    s, d3 = qkv.shape
    d = d3 // 3
    npair = d // LANES
    tq = min(TQ_ATTN, s)
    nb = SLAB_ATTN
    nq = s // tq
    m = N_PAIR_HEADS * tq
    scale = HEAD_DIM ** -0.5
    cum_s, cum_p = _cum_matrices()

    def body(q_ref, k_ref, v_ref, do_ref, w_ref, b_ref, r_ref, cs_ref, cp_ref, dq_ref, dk_ref, dv_ref,
             dk_acc, dv_acc, e_scr, b_scr):
        i = pl.program_id(1)

        @pl.when(i == 0)
        def _():
            dk_acc[...] = jnp.zeros_like(dk_acc)
            dv_acc[...] = jnp.zeros_like(dv_acc)

        cum_suf = cs_ref[...]
        cum_pre = cp_ref[...]
        heads = _head_masks(tq)
        qs = _stack_heads((q_ref[...].astype(F32) * scale).astype(BF16), heads)
        dos = _stack_heads(do_ref[...], heads)

        def left(off, n, r, mask):
            rows = pl.ds(off, n * TK)
            lb, a, r = _sb_slab(qs, k_ref[rows, :], cum_suf, n, r, mask)
            da = lax.dot_general(dos, v_ref[rows, :], _DIMS["nt"], preferred_element_type=F32)
            dv_acc[rows, :] += lax.dot_general(a.astype(BF16), dos, _DIMS["tn"], preferred_element_type=F32)
            return da * a, jnp.exp(lb), r

        def right(off, n, e, beta, pe, mask):
            rows = pl.ds(off, n * TK)
            t = _split_dot(jnp.concatenate([e[:, b * TK:(b + 1) * TK] for b in range(n)], axis=0), cum_pre)
            ps = [None] * n
            for b in range(n):
                tb = t[b * m:(b + 1) * m]
                ps[b] = tb[:, :TK] + pe
                pe = pe + tb[:, TK:]
            dz = e * (1.0 - beta) - beta * jnp.concatenate(ps, axis=1)
            if mask is not None:
                dz = jnp.where(mask, dz, 0.0)
            dzb = dz.astype(BF16)
            dk_acc[rows, :] += lax.dot_general(dzb, qs, _DIMS["tn"], preferred_element_type=F32)
            return pe, jnp.dot(dzb, k_ref[rows, :], preferred_element_type=F32)

        sb, mask = _slab_geometry(i, tq, nb)
        off0 = pl.multiple_of(sb * TK, TK)
        rows0 = pl.ds(off0, nb * TK)
        wb = w_ref[...]
        da0 = lax.dot_general(dos, v_ref[rows0, :], _DIMS["nt"], preferred_element_type=F32)
        dv_acc[rows0, :] += lax.dot_general(wb, dos, _DIMS["tn"], preferred_element_type=F32)
        e0, beta0, r = da0 * wb.astype(F32), jnp.exp(b_ref[...].astype(F32)), r_ref[...]

        def cond(c):
            return jnp.logical_and(c[0] >= 0, jnp.max(c[1]) > NEG_CUTOFF)

        def tail_left(c):
            j, r = c
            e_scr[j], b_scr[j], r = left(pl.multiple_of(j * TK, TK), 1, r, None)
            return j - 1, r

        jend, _ = lax.while_loop(cond, tail_left, (sb - 1, r))

        def tail_right(j, c):
            pe, dq = c
            pe, dqj = right(pl.multiple_of(j * TK, TK), 1, e_scr[j], b_scr[j], pe, None)
            return pe, dq + dqj

        pe, dq = lax.fori_loop(jend + 1, sb, tail_right, (jnp.zeros((m, TK), F32), jnp.zeros((m, LANES), F32)))
        _, dq0 = right(off0, nb, e0, beta0, pe, mask)
        dq_ref[...] = (_unstack_heads(dq + dq0, heads, tq) * scale).astype(dq_ref.dtype)

        @pl.when(i == nq - 1)
        def _():
            dk_ref[...] = dk_acc[...].astype(dk_ref.dtype)
            dv_ref[...] = dv_acc[...].astype(dv_ref.dtype)

    qspec = pl.BlockSpec((tq, LANES), lambda p, i: (i, p))
    full = lambda base: pl.BlockSpec((s, LANES), lambda p, i: (0, base + p))
    cspec = pl.BlockSpec((2 * TK, 2 * TK), lambda p, i: (0, 0))
    sd = jax.ShapeDtypeStruct((s, d), BF16)
    slab = pl.BlockSpec((None, None, m, nb * TK), lambda p, i: (p, i, 0, 0))
    return pl.pallas_call(
        body, name=name, grid=(npair, nq),
        in_specs=[qspec, full(npair), full(2 * npair), qspec, slab, slab,
                  pl.BlockSpec((None, None, m, TK), lambda p, i: (p, i, 0, 0)), cspec, cspec],
        out_specs=[qspec, full(0), full(0)], out_shape=[sd, sd, sd],
        scratch_shapes=[pltpu.VMEM((s, LANES), F32), pltpu.VMEM((s, LANES), F32),
                        pltpu.VMEM((s // TK, m, TK), F32), pltpu.VMEM((s // TK, m, TK), F32)],
        compiler_params=_cp(("parallel", "arbitrary")),
    )(qkv, qkv, qkv, do, *saved, cum_s, cum_p)


HALO = 32
CONV_ROWS = 128


def _ln_swish(hc, g, b):
    mu = jnp.mean(hc, axis=-1, keepdims=True)
    xc = hc - mu
    rstd = lax.rsqrt(jnp.mean(xc * xc, axis=-1, keepdims=True) + EPS)
    xh = xc * rstd
    hn = xh * g + b
    return xh, rstd, hn


def conv_mid_fwd(x, w_dw, b_dw, ln_g, ln_b, name):
    s, d = x.shape
    width = w_dw.shape[0]
    ts = _tile(s, 256, CONV_ROWS)
    base = HALO - (width - 1)
    wpad = jnp.zeros((HALO, d), F32).at[:width].set(w_dw.astype(F32))
    vec = _vec8(b_dw, ln_g, ln_b)

    def body(x_ref, w_ref, v_ref, hc_ref, hs_ref, win):
        i = pl.program_id(0)

        @pl.when(i == 0)
        def _():
            win[0:HALO, :] = jnp.zeros((HALO, d), F32)

        @pl.when(i > 0)
        def _():
            win[0:HALO, :] = win[ts:ts + HALO, :]

        win[HALO:HALO + ts, :] = x_ref[...]
        for rc in range(ts // CONV_ROWS):
            for lc in range(d // LANES):
                cols = slice(lc * LANES, (lc + 1) * LANES)
                acc = jnp.broadcast_to(v_ref[0:1, cols], (CONV_ROWS, LANES))
                for k in range(width):
                    acc = acc + w_ref[k:k + 1, cols] * win[pl.ds(rc * CONV_ROWS + base + k, CONV_ROWS), cols]
                hc_ref[rc * CONV_ROWS:(rc + 1) * CONV_ROWS, cols] = acc
        _, _, hn = _ln_swish(hc_ref[...], v_ref[1:2, :], v_ref[2:3, :])
        hs_ref[...] = (hn * _sigmoid(hn)).astype(hs_ref.dtype)

    spec = pl.BlockSpec((ts, d), lambda i: (i, 0))
    return pl.pallas_call(
        body, name=name, grid=(s // ts,),
        in_specs=[spec, pl.BlockSpec((HALO, d), lambda i: (0, 0)), pl.BlockSpec((SUBLANES, d), lambda i: (0, 0))],
        out_specs=[spec, spec], out_shape=[jax.ShapeDtypeStruct((s, d), F32), jax.ShapeDtypeStruct((s, d), BF16)],
        scratch_shapes=[pltpu.VMEM((ts + HALO, d), F32)],
        compiler_params=_cp(("arbitrary",)),
    )(x, wpad, vec)


def conv_mid_bwd(dhs, hc, x, w_dw, ln_g, ln_b, name):
    s, d = x.shape
    width = w_dw.shape[0]
    ts = _tile(s, 256, CONV_ROWS)
    nt = s // ts
    base = HALO - (width - 1)
    wpad = jnp.zeros((HALO, d), F32).at[:width].set(w_dw.astype(F32))
    vec = _vec8(ln_g, ln_b)

    def body(dhs_ref, hc_ref, x_ref, xh_ref, w_ref, v_ref, dx_ref, dw_ref, s_ref, dwin, xwin):
        i = pl.program_id(0)

        @pl.when(i == 0)
        def _():
            dwin[ts:ts + HALO, :] = jnp.zeros((HALO, d), F32)
            dw_ref[...] = jnp.zeros_like(dw_ref)
            s_ref[...] = jnp.zeros_like(s_ref)

        @pl.when(i > 0)
        def _():
            dwin[ts:ts + HALO, :] = dwin[0:HALO, :]

        @pl.when(i == nt - 1)
        def _():
            xwin[0:HALO, :] = jnp.zeros((HALO, d), F32)

        @pl.when(i < nt - 1)
        def _():
            xwin[0:HALO, :] = xh_ref[...]

        xwin[HALO:HALO + ts, :] = x_ref[...]
        g = v_ref[0:1, :]
        xh, rstd, hn = _ln_swish(hc_ref[...], g, v_ref[1:2, :])
        sig = _sigmoid(hn)
        dhn = dhs_ref[...].astype(F32) * (sig * (1.0 + hn * (1.0 - sig)))
        dxh = dhn * g
        dhc = rstd * (dxh - jnp.mean(dxh, axis=-1, keepdims=True) - xh * jnp.mean(dxh * xh, axis=-1, keepdims=True))
        dwin[0:ts, :] = dhc
        s_ref[0:1, :] += jnp.sum(dhc, axis=0, keepdims=True)
        s_ref[1:2, :] += jnp.sum(dhn * xh, axis=0, keepdims=True)
        s_ref[2:3, :] += jnp.sum(dhn, axis=0, keepdims=True)
        for rc in range(ts // CONV_ROWS):
            for lc in range(d // LANES):
                cols = slice(lc * LANES, (lc + 1) * LANES)
                r0 = rc * CONV_ROWS
                dch = dwin[r0:r0 + CONV_ROWS, cols]
                acc = jnp.zeros((CONV_ROWS, LANES), F32)
                for k in range(width):
                    acc = acc + w_ref[k:k + 1, cols] * dwin[pl.ds(r0 + (width - 1) - k, CONV_ROWS), cols]
                    dw_ref[k:k + 1, cols] += jnp.sum(dch * xwin[pl.ds(r0 + base + k, CONV_ROWS), cols], axis=0,
                                                     keepdims=True)
                dx_ref[r0:r0 + CONV_ROWS, cols] = acc

    rev = pl.BlockSpec((ts, d), lambda i: (nt - 1 - i, 0))
    halo = pl.BlockSpec((HALO, d), lambda i: (jnp.maximum((nt - 1 - i) * (ts // HALO) - 1, 0), 0))
    vspec = pl.BlockSpec((SUBLANES, d), lambda i: (0, 0))
    wspec = pl.BlockSpec((HALO, d), lambda i: (0, 0))
    return pl.pallas_call(
        body, name=name, grid=(nt,), in_specs=[rev, rev, rev, halo, wspec, vspec],
        out_specs=[rev, wspec, vspec],
        out_shape=[jax.ShapeDtypeStruct((s, d), F32), jax.ShapeDtypeStruct((HALO, d), F32),
                   jax.ShapeDtypeStruct((SUBLANES, d), F32)],
        scratch_shapes=[pltpu.VMEM((ts + HALO, d), F32), pltpu.VMEM((ts + HALO, d), F32)],
        compiler_params=_cp(("arbitrary",)),
    )(dhs, hc, x, x, wpad, vec)


S5_KB = 256
SCAN_LANES = 256
SCAN_UNROLL = 4
GELU_C = math.sqrt(2.0 / math.pi)
GELU_A = 0.044715


def _gelu(x):
    return 0.5 * x * (1.0 + jnp.tanh(GELU_C * (x + GELU_A * x * x * x)))


def _gelu_grad(x):
    th = jnp.tanh(GELU_C * (x + GELU_A * x * x * x))
    return 0.5 * (1.0 + th) + 0.5 * x * (1.0 - th * th) * GELU_C * (1.0 + 3.0 * GELU_A * x * x)


def _s5_discretize(lam_re, lam_im, log_dt, b_re, b_im):
    dt = jnp.exp(log_dt)[:, None]
    mag = jnp.exp(lam_re * dt)
    ar, ai = mag * jnp.cos(lam_im * dt), mag * jnp.sin(lam_im * dt)
    den = lam_re * lam_re + lam_im * lam_im
    er = ((ar - 1) * lam_re + ai * lam_im) / den
    ei = (ai * lam_re - (ar - 1) * lam_im) / den
    bbr = er[..., None] * b_re - ei[..., None] * b_im
    bbi = er[..., None] * b_im + ei[..., None] * b_re
    return ar, ai, bbr, bbi


def _blockdiag(w, nkb):
    g, r, c = w.shape
    gpb = g // nkb
    eye = jnp.eye(gpb, dtype=w.dtype)
    return jnp.einsum("kgrc,gh->kgrhc", w.reshape(nkb, gpb, r, c), eye).reshape(nkb, gpb * r, gpb * c)


def _blockdiag_extract(m, g):
    nkb = m.shape[0]
    gpb = g // nkb
    r, c = m.shape[1] // gpb, m.shape[2] // gpb
    eye = jnp.eye(gpb, dtype=m.dtype)
    return jnp.einsum("kgrhc,gh->kgrc", m.reshape(nkb, gpb, r, gpb, c), eye).reshape(g, r, c)


def _scan_powers(ar, ai, reverse):
    ar = ar.reshape(-1)
    ai = (-ai if reverse else ai).reshape(-1)
    cmul = lambda x, y: (x[0] * y[0] - x[1] * y[1], x[0] * y[1] + x[1] * y[0])
    a1 = (ar, ai)
    a2 = cmul(a1, a1)
    a4 = cmul(a2, a2)
    r = jnp.arange(SUBLANES)[:, None]
    rows = []
    for sft, p in ((1, a1), (2, a2), (4, a4)):
        keep = (r + sft <= SUBLANES - 1) if reverse else (r >= sft)
        rows += [jnp.where(keep, p[0][None, :], 0.0), jnp.where(keep, p[1][None, :], 0.0)]
    pows = [a1]
    for _ in range(SUBLANES - 1):
        pows.append(cmul(pows[-1], a1))
    if reverse:
        pows = pows[::-1]
    rows += [jnp.stack([p[0] for p in pows]), jnp.stack([p[1] for p in pows])]
    return jnp.concatenate(rows, axis=0).astype(F32)


def _scan_tile(sr, si, pw_ref, car, nrg, reverse):
    nsb = sr.shape[1]
    ch = min(SCAN_LANES, nsb)
    nch = nsb // ch
    row = 0 if reverse else SUBLANES - 1

    unroll = SCAN_UNROLL if nrg % SCAN_UNROLL == 0 else 1

    def step(t, carry):
        carry = list(carry)
        for u in range(unroll):
            g = t * unroll + u
            rg = (nrg - 1 - g) if reverse else g
            off = pl.multiple_of(rg * SUBLANES, SUBLANES)
            for c in range(nch):
                cols = slice(c * ch, (c + 1) * ch)
                cr, ci = carry[2 * c], carry[2 * c + 1]
                br = sr[pl.ds(off, SUBLANES), cols]
                bi = si[pl.ds(off, SUBLANES), cols]
                for idx, sft in enumerate((1, 2, 4)):
                    sh = SUBLANES - sft if reverse else sft
                    tr = pltpu.roll(br, sh, axis=0)
                    ti = pltpu.roll(bi, sh, axis=0)
                    mr = pw_ref[16 * idx:16 * idx + 8, cols]
                    mi = pw_ref[16 * idx + 8:16 * idx + 16, cols]
                    br, bi = br + mr * tr - mi * ti, bi + mr * ti + mi * tr
                apr, api = pw_ref[48:56, cols], pw_ref[56:64, cols]
                xr = br + apr * cr - api * ci
                xi = bi + apr * ci + api * cr
                sr[pl.ds(off, SUBLANES), cols] = xr
                si[pl.ds(off, SUBLANES), cols] = xi
                carry[2 * c] = jnp.broadcast_to(xr[row:row + 1, :], xr.shape)
                carry[2 * c + 1] = jnp.broadcast_to(xi[row:row + 1, :], xi.shape)
        return tuple(carry)

    init = []
    for c in range(nch):
        cols = slice(c * ch, (c + 1) * ch)
        init += [car[0:SUBLANES, cols], car[SUBLANES:2 * SUBLANES, cols]]
    fin = lax.fori_loop(0, nrg // unroll, step, tuple(init))
    for c in range(nch):
        cols = slice(c * ch, (c + 1) * ch)
        car[0:SUBLANES, cols] = fin[2 * c]
        car[SUBLANES:2 * SUBLANES, cols] = fin[2 * c + 1]


def s5_fwd(u, wb_r, wb_i, wc_r, wc_i, pw, d_skip, name):
    s, d = u.shape
    nkb, kb, nsb = wb_r.shape
    ts = _tile(s, 256, SUBLANES)
    dvec = _vec8(d_skip)

    def body(u_ref, wbr, wbi, wcr, wci, pw_ref, dv_ref, xr_ref, xi_ref, yy_ref, g_ref, sr, si, car):
        @pl.when(pl.program_id(1) == 0)
        def _():
            car[...] = jnp.zeros_like(car)

        uu = u_ref[...]
        ub = uu.astype(BF16)
        sr[...] = jnp.dot(ub, wbr[...], preferred_element_type=F32)
        si[...] = jnp.dot(ub, wbi[...], preferred_element_type=F32)
        _scan_tile(sr, si, pw_ref, car, ts // SUBLANES, False)
        xr, xi = sr[...], si[...]
        xr_ref[...] = xr
        xi_ref[...] = xi
        y = (jnp.dot(xr.astype(BF16), wcr[...], preferred_element_type=F32)
             + jnp.dot(xi.astype(BF16), wci[...], preferred_element_type=F32) + dv_ref[0:1, :] * uu)
        yy_ref[...] = y
        g_ref[...] = _gelu(y).astype(g_ref.dtype)

    cspec = pl.BlockSpec((ts, kb), lambda k, i: (i, k))
    sspec = pl.BlockSpec((ts, nsb), lambda k, i: (i, k))
    wbspec = pl.BlockSpec((None, kb, nsb), lambda k, i: (k, 0, 0))
    wcspec = pl.BlockSpec((None, nsb, kb), lambda k, i: (k, 0, 0))
    ns = nkb * nsb
    return pl.pallas_call(
        body, name=name, grid=(nkb, s // ts),
        in_specs=[cspec, wbspec, wbspec, wcspec, wcspec, pl.BlockSpec((64, nsb), lambda k, i: (0, k)),
                  pl.BlockSpec((SUBLANES, kb), lambda k, i: (0, k))],
        out_specs=[sspec, sspec, cspec, cspec],
        out_shape=[jax.ShapeDtypeStruct((s, ns), F32), jax.ShapeDtypeStruct((s, ns), F32),
                   jax.ShapeDtypeStruct((s, d), F32), jax.ShapeDtypeStruct((s, d), BF16)],
        scratch_shapes=[pltpu.VMEM((ts, nsb), F32), pltpu.VMEM((ts, nsb), F32), pltpu.VMEM((2 * SUBLANES, nsb), F32)],
        compiler_params=_cp(("parallel", "arbitrary")),
    )(u, wb_r, wb_i, wc_r, wc_i, pw, dvec)


def s5_bwd(dg, yy, u, xr, xi, wb_r, wb_i, wc_r, wc_i, pwb, d_skip, name):
    s, d = u.shape
    nkb, kb, nsb = wb_r.shape
    ns = nkb * nsb
    ts = _tile(s, 256, SUBLANES)
    nt = s // ts
    dvec = _vec8(d_skip)

    def body(dg_ref, yy_ref, u_ref, xr_ref, xi_ref, xrp_ref, xip_ref, wbr, wbi, wcr, wci, pw_ref, dv_ref,
             du_ref, dwbr, dwbi, dwcr, dwci, da_ref, dd_ref, sr, si, car):
        i = pl.program_id(1)

        @pl.when(i == 0)
        def _():
            car[...] = jnp.zeros_like(car)
            for r in (dwbr, dwbi, dwcr, dwci, da_ref, dd_ref):
                r[...] = jnp.zeros_like(r)

        uu = u_ref[...]
        dyy = dg_ref[...] * _gelu_grad(yy_ref[...])
        dd_ref[0:1, :] += jnp.sum(dyy * uu, axis=0, keepdims=True)
        dyb = dyy.astype(BF16)
        sr[...] = lax.dot_general(dyb, wcr[...], _DIMS["nt"], preferred_element_type=F32)
        si[...] = lax.dot_general(dyb, wci[...], _DIMS["nt"], preferred_element_type=F32)
        _scan_tile(sr, si, pw_ref, car, ts // SUBLANES, True)
        gr, gi = sr[...], si[...]
        grb, gib = gr.astype(BF16), gi.astype(BF16)
        xrt, xit = xr_ref[...], xi_ref[...]
        dwcr[...] += lax.dot_general(xrt.astype(BF16), dyb, _DIMS["tn"], preferred_element_type=F32)
        dwci[...] += lax.dot_general(xit.astype(BF16), dyb, _DIMS["tn"], preferred_element_type=F32)
        ub = uu.astype(BF16)
        dwbr[...] += lax.dot_general(ub, grb, _DIMS["tn"], preferred_element_type=F32)
        dwbi[...] += lax.dot_general(ub, gib, _DIMS["tn"], preferred_element_type=F32)
        du_ref[...] = (lax.dot_general(grb, wbr[...], _DIMS["nt"], preferred_element_type=F32)
                       + lax.dot_general(gib, wbi[...], _DIMS["nt"], preferred_element_type=F32)
                       + dyy * dv_ref[0:1, :])
        has_prev = (i < nt - 1).astype(F32)
        rowid = lax.broadcasted_iota(jnp.int32, (ts, nsb), 0)
        pr = jnp.broadcast_to(xrp_ref[SUBLANES - 1:SUBLANES, :] * has_prev, (ts, nsb))
        pi = jnp.broadcast_to(xip_ref[SUBLANES - 1:SUBLANES, :] * has_prev, (ts, nsb))
        xpr = jnp.where(rowid == 0, pr, pltpu.roll(xrt, 1, axis=0))
        xpi = jnp.where(rowid == 0, pi, pltpu.roll(xit, 1, axis=0))
        da_ref[0:1, :] += jnp.sum(gr * xpr + gi * xpi, axis=0, keepdims=True)
        da_ref[1:2, :] += jnp.sum(gi * xpr - gr * xpi, axis=0, keepdims=True)

    cspec = pl.BlockSpec((ts, kb), lambda k, i: (nt - 1 - i, k))
    sspec = pl.BlockSpec((ts, nsb), lambda k, i: (nt - 1 - i, k))
    pspec = pl.BlockSpec((SUBLANES, nsb), lambda k, i: (jnp.maximum((nt - 1 - i) * (ts // SUBLANES) - 1, 0), k))
    wbspec = pl.BlockSpec((None, kb, nsb), lambda k, i: (k, 0, 0))
    wcspec = pl.BlockSpec((None, nsb, kb), lambda k, i: (k, 0, 0))
    v8s = pl.BlockSpec((SUBLANES, nsb), lambda k, i: (0, k))
    v8c = pl.BlockSpec((SUBLANES, kb), lambda k, i: (0, k))
    return pl.pallas_call(
        body, name=name, grid=(nkb, nt),
        in_specs=[cspec, cspec, cspec, sspec, sspec, pspec, pspec, wbspec, wbspec, wcspec, wcspec,
                  pl.BlockSpec((64, nsb), lambda k, i: (0, k)), v8c],
        out_specs=[cspec, wbspec, wbspec, wcspec, wcspec, v8s, v8c],
        out_shape=[jax.ShapeDtypeStruct((s, d), F32),
                   jax.ShapeDtypeStruct((nkb, kb, nsb), F32), jax.ShapeDtypeStruct((nkb, kb, nsb), F32),
                   jax.ShapeDtypeStruct((nkb, nsb, kb), F32), jax.ShapeDtypeStruct((nkb, nsb, kb), F32),
                   jax.ShapeDtypeStruct((SUBLANES, ns), F32), jax.ShapeDtypeStruct((SUBLANES, d), F32)],
        scratch_shapes=[pltpu.VMEM((ts, nsb), F32), pltpu.VMEM((ts, nsb), F32), pltpu.VMEM((2 * SUBLANES, nsb), F32)],
        compiler_params=_cp(("parallel", "arbitrary")),
    )(dg, yy, u, xr, xi, xr, xi, wb_r, wb_i, wc_r, wc_i, pwb, dvec)


def s5_operands(lam_re, lam_im, log_dt, b_re, b_im, c_re, c_im, d):
    ar, ai, bbr, bbi = _s5_discretize(lam_re, lam_im, log_dt, b_re, b_im)
    nkb = max(d // S5_KB, 1)
    wb_r = _blockdiag(bbr.transpose(0, 2, 1), nkb).astype(BF16)
    wb_i = _blockdiag(bbi.transpose(0, 2, 1), nkb).astype(BF16)
    wc_r = _blockdiag(c_re.transpose(0, 2, 1), nkb).astype(BF16)
    wc_i = _blockdiag(-c_im.transpose(0, 2, 1), nkb).astype(BF16)
    return wb_r, wb_i, wc_r, wc_i, _scan_powers(ar, ai, False), _scan_powers(ar, ai, True)


MESH = pl.DeviceIdType.MESH
HBM_SPEC = pl.BlockSpec(memory_space=pltpu.HBM)


def _me():
    return 4 * lax.axis_index("x") + 2 * lax.axis_index("y") + lax.axis_index("c")


N_COPIES = N_DEV - 1


class Carried:
    def __init__(self, kind, operands):
        self.kind, self.operands = kind, list(operands)
        self.n = len(self.operands)

    def call_args(self):
        shapes = [jax.ShapeDtypeStruct((N_DEV,) + tuple(a.shape[-2:]), a.dtype) for a in self.operands]
        scratch = [pltpu.SemaphoreType.DMA((N_COPIES * self.n,)), pltpu.SemaphoreType.DMA((N_COPIES * self.n,)),
                   pltpu.SemaphoreType.DMA((self.n,))]
        return [HBM_SPEC] * self.n, shapes, scratch

    def _plan(self, t, x_ref, out_ref, send, recv, loc):
        x, y, c = lax.axis_index("x"), lax.axis_index("y"), lax.axis_index("c")
        me = 4 * x + 2 * y + c

        def rdma(k, src, dst, to):
            return pltpu.make_async_remote_copy(src_ref=src, dst_ref=dst, send_sem=send.at[N_COPIES * t + k],
                                                recv_sem=recv.at[N_COPIES * t + k], device_id=to, device_id_type=MESH)

        if self.kind == "ex":
            peers = [(1 - x if k & 4 else x, 1 - y if k & 2 else y, 1 - c if k & 1 else c) for k in range(1, N_DEV)]
            sends = [rdma(k, x_ref.at[4 * px + 2 * py + pc], out_ref.at[me], (px, py, pc))
                     for k, (px, py, pc) in enumerate(peers)]
            local = pltpu.make_async_copy(x_ref.at[me], out_ref.at[me], loc.at[t])
            return dict(local=local, first=sends, relay_on=[], relays=[], arrive=sends)
        sibling = (x, y, 1 - c)
        chips = [(1 - x, y), (x, 1 - y), (1 - x, 1 - y)]
        slot = lambda px, py, pc: out_ref.at[4 * px + 2 * py + pc]
        first = [rdma(0, x_ref, slot(x, y, c), sibling)]
        first += [rdma(1 + j, x_ref, slot(x, y, c), (*chip, c)) for j, chip in enumerate(chips)]
        relay_on = [rdma(1 + j, slot(*chip, c), slot(*chip, c), (x, y, c)) for j, chip in enumerate(chips)]
        relays = [rdma(4 + j, slot(*chip, c), slot(*chip, c), sibling) for j, chip in enumerate(chips)]
        arrive = [rdma(0, slot(*sibling), slot(*sibling), (x, y, c))]
        arrive += [rdma(4 + j, slot(*chip, 1 - c), slot(*chip, 1 - c), (x, y, c)) for j, chip in enumerate(chips)]
        local = pltpu.make_async_copy(x_ref, slot(x, y, c), loc.at[t])
        return dict(local=local, first=first, relay_on=relay_on, relays=relays, arrive=arrive)

    def _plans(self, refs):
        xs, outs, (send, recv, loc) = refs[:self.n], refs[self.n:2 * self.n], refs[2 * self.n:]
        return [self._plan(t, xs[t], outs[t], send, recv, loc) for t in range(self.n)]

    def begin(self, refs):
        for p in self._plans(refs):
            p["local"].start()
            for cp in p["first"]:
                cp.start()

    def finish(self, refs):
        plans = self._plans(refs)
        for p in plans:
            for landed, relay in zip(p["relay_on"], p["relays"]):
                landed.wait_recv()
                relay.start()
        for p in plans:
            for cp in p["arrive"]:
                cp.wait_recv()
            for cp in p["first"] + p["relays"]:
                cp.wait_send()
            p["local"].wait()


def _carry_refs(refs, n_in, n_out, car):
    if car is None:
        return list(refs), None
    n = car.n
    host = list(refs[:n_in]) + list(refs[n_in + n:n_in + n + n_out]) + list(refs[n_in + 2 * n + n_out:-3])
    return host, list(refs[n_in:n_in + n]) + list(refs[n_in + n + n_out:n_in + 2 * n + n_out]) + list(refs[-3:])


def _carry_steps(car, crefs, grid):
    if car is None:
        return lambda: None
    ids = [pl.program_id(a) for a in range(len(grid))]
    first = functools.reduce(jnp.logical_and, [i == 0 for i in ids])
    last = functools.reduce(jnp.logical_and, [i == g - 1 for i, g in zip(ids, grid)])

    @pl.when(first)
    def _():
        car.begin(crefs)

    def after():
        @pl.when(last)
        def _():
            car.finish(crefs)

    return after


def communicate(kind, operands, name):
    car = Carried(kind, operands)
    specs, shapes, scratch = car.call_args()

    def body(*refs):
        car.begin(refs)
        car.finish(refs)

    return pl.pallas_call(body, name=name, in_specs=specs, out_specs=specs, out_shape=shapes,
                          scratch_shapes=scratch)(*car.operands)


def all_gather(shard, name):
    return communicate("ag", [shard], name)[0]


def sum_slots(parts, name):
    _, m, n = parts.shape
    tm = _tile(m, 256, SUBLANES)

    def body(p_ref, o_ref):
        acc = p_ref[0].astype(F32)
        for q in range(1, N_DEV):
            acc = acc + p_ref[q].astype(F32)
        o_ref[...] = acc

    return pl.pallas_call(
        body, name=name, grid=(m // tm,), in_specs=[pl.BlockSpec((N_DEV, tm, n), lambda i: (0, i, 0))],
        out_specs=pl.BlockSpec((tm, n), lambda i: (i, 0)), out_shape=jax.ShapeDtypeStruct((m, n), F32),
        compiler_params=_cp(("parallel",)),
    )(parts)


PACK_COLS = 1024
PACK_ROWS = 16


def _pack_flat(pieces, dtype):
    lead = pieces[0].shape[:-1]
    flat = jnp.concatenate([p.astype(dtype) for p in pieces], axis=-1)
    unit = PACK_COLS * PACK_ROWS
    total = -(-flat.shape[-1] // unit) * unit
    flat = jnp.pad(flat, [(0, 0)] * len(lead) + [(0, total - flat.shape[-1])])
    return flat.reshape(*lead, total // PACK_COLS, PACK_COLS)


def _unpack_flat(packed, sizes):
    lead = packed.shape[:-2]
    flat = packed.reshape(*lead, -1)
    out, off = [], 0
    for n in sizes:
        out.append(flat[..., off:off + n])
        off += n
    return out


def mod_fwd(c_all, w_mod, b_cols, name):
    nl, d, n = w_mod.shape

    def body(c_ref, w_ref, b_ref, o_ref):
        cv = c_ref[...]
        sc = (cv * _sigmoid(cv)).astype(BF16)
        o_ref[...] = jnp.dot(sc, w_ref[...].astype(BF16), preferred_element_type=F32) + b_ref[...]

    return pl.pallas_call(
        body, name=name, grid=(nl,),
        in_specs=[pl.BlockSpec((N_DEV, d), lambda l: (0, 0)), pl.BlockSpec((None, d, n), lambda l: (l, 0, 0)),
                  pl.BlockSpec((None, 1, n), lambda l: (l, 0, 0))],
        out_specs=pl.BlockSpec((None, N_DEV, n), lambda l: (l, 0, 0)),
        out_shape=jax.ShapeDtypeStruct((nl, N_DEV, n), F32), compiler_params=_cp(("parallel",)),
    )(c_all, w_mod, b_cols.reshape(nl, 1, n))


def mod_bwd(c_all, dmod_cols, name):
    nl, _, n = dmod_cols.shape
    d = c_all.shape[1]

    def body(c_ref, g_ref, o_ref):
        cv = c_ref[...]
        sc = (cv * _sigmoid(cv)).astype(BF16)
        o_ref[...] = lax.dot_general(sc, g_ref[...].astype(BF16), _DIMS["tn"], preferred_element_type=F32)

    return pl.pallas_call(
        body, name=name, grid=(nl,),
        in_specs=[pl.BlockSpec((N_DEV, d), lambda l: (0, 0)), pl.BlockSpec((None, N_DEV, n), lambda l: (l, 0, 0))],
        out_specs=pl.BlockSpec((None, d, n), lambda l: (l, 0, 0)),
        out_shape=jax.ShapeDtypeStruct((nl, d, n), F32), compiler_params=_cp(("parallel",)),
    )(c_all, dmod_cols)


def adamw(w, g, m, v, name):
    r, c = w.shape
    tr = _tile(r, 512, SUBLANES)
    c1 = 1.0 - ADAM_B1 ** ADAM_STEP
    c2 = 1.0 - ADAM_B2 ** ADAM_STEP

    def body(w_ref, g_ref, m_ref, v_ref, d_ref, nm_ref, nv_ref):
        gg = g_ref[...]
        nm = ADAM_B1 * m_ref[...] + (1.0 - ADAM_B1) * gg
        nv = ADAM_B2 * v_ref[...] + (1.0 - ADAM_B2) * (gg * gg)
        nm_ref[...] = nm
        nv_ref[...] = nv
        d_ref[...] = -ADAM_LR * ((nm / c1) / (jnp.sqrt(nv / c2) + ADAM_EPS) + ADAM_WD * w_ref[...])

    spec = pl.BlockSpec((tr, c), lambda i: (i, 0))
    sd = jax.ShapeDtypeStruct((r, c), F32)
    return pl.pallas_call(
        body, name=name, grid=(r // tr,), in_specs=[spec] * 4, out_specs=[spec] * 3, out_shape=[sd, sd, sd],
        compiler_params=_cp(("parallel",)),
    )(w, g, m, v)


WEIGHTS = ["norm_g", "w_mod", "b_mod", "sb_w_qkv", "sb_w_o", "s5_lam_re", "s5_lam_im", "s5_log_dt", "s5_b_re",
           "s5_b_im", "s5_c_re", "s5_c_im", "s5_d", "s5_w_glu", "s5_b_glu", "cv_w_pw1", "cv_b_pw1", "cv_w_dw",
           "cv_b_dw", "cv_ln_g", "cv_ln_b", "cv_w_pw2", "cv_b_pw2", "ffn_w_gate", "ffn_w_up", "ffn_w_down"]
BIG = ["w_mod", "sb_w_qkv", "sb_w_o", "s5_w_glu", "cv_w_pw1", "cv_w_pw2", "ffn_w_gate", "ffn_w_up", "ffn_w_down"]
SMALL_SHARDED = ["norm_g", "cv_b_pw1", "cv_w_dw", "cv_b_dw", "cv_ln_g", "cv_ln_b", "cv_b_pw2"]
SMALL = [n for n in WEIGHTS if n not in BIG]
FFN_KEYS = ["gate", "up"]
FFN_ALL = ["gate", "up", "down"]


def _unshard_last(part, local_shape):
    a = jnp.moveaxis(part.reshape((N_DEV,) + tuple(local_shape)), 0, -2)
    return a.reshape(tuple(local_shape[:-1]) + (N_DEV * local_shape[-1],))


def kernel(x, c, norm_g, w_mod, b_mod, sb_w_qkv, sb_w_o, s5_lam_re, s5_lam_im, s5_log_dt, s5_b_re, s5_b_im, s5_c_re, s5_c_im, s5_d, s5_w_glu, s5_b_glu, cv_w_pw1, cv_b_pw1, cv_w_dw, cv_b_dw, cv_ln_g, cv_ln_b, cv_w_pw2, cv_b_pw2, ffn_w_gate, ffn_w_up, ffn_w_down, loss_target, m_norm_g, m_w_mod, m_b_mod, m_sb_w_qkv, m_sb_w_o, m_s5_lam_re, m_s5_lam_im, m_s5_log_dt, m_s5_b_re, m_s5_b_im, m_s5_c_re, m_s5_c_im, m_s5_d, m_s5_w_glu, m_s5_b_glu, m_cv_w_pw1, m_cv_b_pw1, m_cv_w_dw, m_cv_b_dw, m_cv_ln_g, m_cv_ln_b, m_cv_w_pw2, m_cv_b_pw2, m_ffn_w_gate, m_ffn_w_up, m_ffn_w_down, v_norm_g, v_w_mod, v_b_mod, v_sb_w_qkv, v_sb_w_o, v_s5_lam_re, v_s5_lam_im, v_s5_log_dt, v_s5_b_re, v_s5_b_im, v_s5_c_re, v_s5_c_im, v_s5_d, v_s5_w_glu, v_s5_b_glu, v_cv_w_pw1, v_cv_b_pw1, v_cv_w_dw, v_cv_b_dw, v_cv_ln_g, v_cv_ln_b, v_cv_w_pw2, v_cv_b_pw2, v_ffn_w_gate, v_ffn_w_up, v_ffn_w_down):
    p = dict(locals())
    me = _me()
    s, d = x.shape[1], x.shape[2]
    depth = norm_g.shape[0]
    h = x.reshape(s, d)
    target = loss_target.reshape(s, d)

    pieces = [p[n].reshape(-1) for n in SMALL_SHARDED] + [c.reshape(-1)]
    parts = _unpack_flat(all_gather(_pack_flat(pieces, F32), "ag_small"), [q.shape[0] for q in pieces])
    full = {n: _unshard_last(part, p[n].shape) for n, part in zip(SMALL_SHARDED, parts)}
    c_all = parts[-1]

    nmod = w_mod.shape[2]
    b_cols = lax.dynamic_slice_in_dim(b_mod, me * nmod, nmod, axis=1)
    mod_cols = mod_fwd(c_all, w_mod, b_cols, "mod_fwd")
    g_mod = all_gather(mod_cols.reshape(depth * N_DEV, nmod), "ag_mod").reshape(N_DEV, depth, N_DEV, nmod)
    mod = jnp.moveaxis(lax.dynamic_index_in_dim(g_mod, me, axis=2, keepdims=False), 0, 1).reshape(depth, N_DEV * nmod)
    ng = full["norm_g"]

    def layer_pieces(l):
        kind, j = l % 3, l // 3
        if kind == 0:
            ps = [("qkv", sb_w_qkv[j], "col"), ("o", sb_w_o[j], "row")]
        elif kind == 1:
            ps = [("glu", s5_w_glu[j], "col")]
        else:
            ps = [("pw1", cv_w_pw1[j], "col"), ("pw2", cv_w_pw2[j], "row")]
        return ps + [("gate", ffn_w_gate[l], "col"), ("up", ffn_w_up[l], "col"), ("down", ffn_w_down[l], "row")]

    def weight_gather(l, ffn):
        return Carried("ag", [a.astype(BF16) for k, a, _ in layer_pieces(l) if (k in FFN_ALL) == ffn])

    def gathered_weights(l, ffn, got):
        out = {}
        for (key, a, how), blk in zip([q for q in layer_pieces(l) if (q[0] in FFN_ALL) == ffn], got):
            r, cc = a.shape
            out[key] = blk.transpose(1, 0, 2).reshape(r, N_DEV * cc) if how == "col" else blk.reshape(N_DEV * r, cc)
        return out

    def grad_exchange(l, grads, keys):
        slabs = []
        for key, a, how in layer_pieces(l):
            if key in keys:
                r, cc = a.shape
                g = grads[key]
                slabs.append(g.reshape(r, N_DEV, cc).transpose(1, 0, 2) if how == "col" else g.reshape(N_DEV, r, cc))
        return Carried("ex", slabs)

    def store_grads(l, keys, got):
        names = {"qkv": "sb_w_qkv", "o": "sb_w_o", "glu": "s5_w_glu", "pw1": "cv_w_pw1", "pw2": "cv_w_pw2",
                 "gate": "ffn_w_gate", "up": "ffn_w_up", "down": "ffn_w_down"}
        for key, parts in zip(keys, got):
            idx = l if key in ("gate", "up", "down") else l // 3
            gbig[names[key]][idx] = sum_slots(parts, f"rs_sum_{key}{l}")

    saved = []
    gbig = {n: [None] * p[n].shape[0] for n in BIG if n != "w_mod"}
    first = weight_gather(0, False).operands + weight_gather(0, True).operands
    got = communicate("ag", first, "ag_w0")
    n_mix = len(first) - len(FFN_ALL)
    w_next = {**gathered_weights(0, False, got[:n_mix]), **gathered_weights(0, True, got[n_mix:])}
    for l in range(depth):
        kind, j = l % 3, l // 3
        w = w_next
        sh_m, sc_m, g_m, sh_f, sc_f, g_f = jnp.split(mod[l], 6)
        st = {"w": w, "h": h}
        if kind == 0:
            (u,) = norm_mod_fwd(h, ng[l, 0], sc_m, sh_m, [BF16], f"nm_a{l}")
            qkv = mm([(u, w["qkv"])], "nn", BF16, name=f"qkv{l}")
            o, *kept = attn_fwd(qkv, f"attn_fwd{l}")
            st.update(kept=kept)
            m = mm([(o, w["o"])], "nn", F32, name=f"attn_o{l}")
            st.update(u=u, qkv=qkv, o=o)
        elif kind == 1:
            (u,) = norm_mod_fwd(h, ng[l, 0], sc_m, sh_m, [F32], f"nm_a{l}")
            ops = s5_operands(s5_lam_re[j], s5_lam_im[j], s5_log_dt[j], s5_b_re[j], s5_b_im[j], s5_c_re[j],
                              s5_c_im[j], d)
            xr, xi, yy, gl = s5_fwd(u, *ops[:5], s5_d[j], f"s5_fwd{l}")
            p1, p2, m = mm_dual(gl, w["glu"], s5_b_glu[j], "glu", F32, F32, f"s5_glu{l}")
            st.update(u=u, ops=ops, xr=xr, xi=xi, yy=yy, gl=gl, p1=p1, p2=p2)
        else:
            (u,) = norm_mod_fwd(h, ng[l, 0], sc_m, sh_m, [BF16], f"nm_a{l}")
            p1, p2, hg = mm_dual(u, w["pw1"], full["cv_b_pw1"][j], "glu", F32, F32, f"cv_pw1{l}")
            hc, hs = conv_mid_fwd(hg, full["cv_w_dw"][j], full["cv_b_dw"][j], full["cv_ln_g"][j],
                                  full["cv_ln_b"][j], f"cv_mid{l}")
            m = mm([(hs, w["pw2"])], "nn", F32, bias=full["cv_b_pw2"][j], name=f"cv_pw2{l}")
            st.update(u=u, p1=p1, p2=p2, hg=hg, hc=hc, hs=hs)
        h2 = resid_fwd(h, m, ng[l, 1], g_m, f"res_a{l}")
        (u2,) = norm_mod_fwd(h2, ng[l, 2], sc_f, sh_f, [BF16], f"nm_f{l}")
        w_gu = (w["gate"], w["up"])
        if l + 1 < depth:
            (f1, f2, z), got = mm_dual(u2, w_gu, None, "swiglu", BF16, BF16, f"ffn_up{l}",
                                       car=weight_gather(l + 1, True))
            f, got_mix = mm([(z, w["down"])], "nn", F32, name=f"ffn_down{l}", car=weight_gather(l + 1, False))
            w_next = {**gathered_weights(l + 1, True, got), **gathered_weights(l + 1, False, got_mix)}
        else:
            f1, f2, z = mm_dual(u2, w_gu, None, "swiglu", BF16, BF16, f"ffn_up{l}")
            f = mm([(z, w["down"])], "nn", F32, name=f"ffn_down{l}")
        h = resid_fwd(h2, f, ng[l, 3], g_f, f"res_f{l}")
        st.update(m=m, h2=h2, u2=u2, f1=f1, f2=f2, z=z, f=f)
        saved.append(st)

    loss_arr, dh = loss_and_grad(h, target, "loss")
    loss = lax.psum(loss_arr[0, 0], AXES)

    nl_sb, nl_s5, nl_cv = sb_w_qkv.shape[0], s5_w_glu.shape[0], cv_w_pw1.shape[0]
    pending = None
    dng = [None] * depth
    dmod = [None] * depth
    gs5 = {n: [None] * nl_s5 for n in SMALL if n.startswith("s5_")}
    gcv = {n: [None] * nl_cv for n in SMALL if n.startswith("cv_")}
    for l in reversed(range(depth)):
        kind, j = l % 3, l // 3
        st = saved[l]
        w = st["w"]
        sh_m, sc_m, g_m, sh_f, sc_f, g_f = jnp.split(mod[l], 6)
        gw = {}
        df, s_rf = resid_bwd(dh, st["f"], ng[l, 3], g_f, BF16, f"res_f_bwd{l}")
        rest = None if pending is None else [k for k, _, _ in layer_pieces(pending[0]) if k not in FFN_KEYS]
        riding = lambda keys: None if pending is None else grad_exchange(pending[0], pending[1], keys)

        def landed(keys, res):
            if pending is None:
                return res
            store_grads(pending[0], keys, res[1])
            return res[0]

        d1, d2 = landed(FFN_KEYS[:1], mm_act_bwd(df, w["down"], st["f1"], st["f2"], "swiglu", f"ffn_dz{l}",
                                                 car=riding(FFN_KEYS[:1])))
        gw["down"] = landed(FFN_KEYS[1:], mm([(st["z"], df)], "tn", BF16, name=f"ffn_dwd{l}", car=riding(FFN_KEYS[1:])))
        du2 = landed(rest, mm([(d1, w["gate"]), (d2, w["up"])], "nt", F32, name=f"ffn_du{l}", car=riding(rest)))
        gw["gate"] = mm([(st["u2"], d1)], "tn", BF16, name=f"ffn_dwg{l}")
        gw["up"] = mm([(st["u2"], d2)], "tn", BF16, name=f"ffn_dwu{l}")
        dh2, s_nf = norm_mod_bwd([du2], st["h2"], dh, ng[l, 2], sc_f, f"nm_f_bwd{l}")
        dm, s_rm = resid_bwd(dh2, st["m"], ng[l, 1], g_m, F32 if kind == 1 else BF16, f"res_a_bwd{l}")
        if kind == 0:
            do = mm([(dm, w["o"])], "nt", BF16, name=f"attn_do{l}")
            gw["o"] = mm([(st["o"], dm)], "tn", BF16, name=f"attn_dwo{l}")
            dq, dk, dv = attn_bwd(st["qkv"], do, st["kept"], f"attn_bwd{l}")
            wq = w["qkv"]
            dus = [mm([(dq, wq[:, :d]), (dk, wq[:, d:2 * d]), (dv, wq[:, 2 * d:])], "nt", F32, name=f"qkv_du{l}")]
            gw["qkv"] = jnp.concatenate([mm([(st["u"], t)], "tn", BF16, name=f"qkv_dw{l}_{i}")
                                         for i, t in enumerate((dq, dk, dv))], axis=1)
        elif kind == 1:
            d1, d2, cs = dual_bwd(dm, st["p1"], st["p2"], "glu", f"s5_glu_bwd{l}")
            gs5["s5_b_glu"][j] = jnp.concatenate([cs[0], cs[1]])
            wg = w["glu"]
            dgl = mm([(d1, wg[:, :d]), (d2, wg[:, d:])], "nt", F32, name=f"s5_dgl{l}")
            gw["glu"] = jnp.concatenate([mm([(st["gl"], t)], "tn", BF16, name=f"s5_dwglu{l}_{i}")
                                         for i, t in enumerate((d1, d2))], axis=1)
            ops = st["ops"]
            du, dwbr, dwbi, dwcr, dwci, da, dd = s5_bwd(dgl, st["yy"], st["u"], st["xr"], st["xi"], *ops[:4],
                                                        ops[5], s5_d[j], f"s5_bwd{l}")
            dus = [du]
            ngrp = s5_lam_re.shape[1]
            ext = lambda t: _blockdiag_extract(t, ngrp).transpose(0, 2, 1)
            _, disc_vjp = jax.vjp(_s5_discretize, s5_lam_re[j], s5_lam_im[j], s5_log_dt[j], s5_b_re[j], s5_b_im[j])
            shp = s5_lam_re[j].shape
            dlr, dli, dldt, dbr, dbi = disc_vjp((da[0].reshape(shp), da[1].reshape(shp), ext(dwbr), ext(dwbi)))
            for n, t in (("s5_lam_re", dlr), ("s5_lam_im", dli), ("s5_log_dt", dldt), ("s5_b_re", dbr),
                         ("s5_b_im", dbi), ("s5_c_re", ext(dwcr)), ("s5_c_im", -ext(dwci)), ("s5_d", dd[0])):
                gs5[n][j] = t
        else:
            dhs = mm([(dm, w["pw2"])], "nt", BF16, name=f"cv_dhs{l}")
            gw["pw2"] = mm([(st["hs"], dm)], "tn", BF16, name=f"cv_dwpw2{l}")
            dhg, dwdw, s_cv = conv_mid_bwd(dhs, st["hc"], st["hg"], full["cv_w_dw"][j], full["cv_ln_g"][j],
                                           full["cv_ln_b"][j], f"cv_mid_bwd{l}")
            d1, d2, cs = dual_bwd(dhg, st["p1"], st["p2"], "glu", f"cv_glu_bwd{l}")
            wp = w["pw1"]
            dus = [mm([(d1, wp[:, :d]), (d2, wp[:, d:])], "nt", F32, name=f"cv_du{l}")]
            gw["pw1"] = jnp.concatenate([mm([(st["u"], t)], "tn", BF16, name=f"cv_dwpw1{l}_{i}")
                                         for i, t in enumerate((d1, d2))], axis=1)
            for n, t in (("cv_b_pw1", jnp.concatenate([cs[0], cs[1]])), ("cv_w_dw", dwdw[:cv_w_dw.shape[1]]),
                         ("cv_b_dw", s_cv[0]), ("cv_ln_g", s_cv[1]), ("cv_ln_b", s_cv[2]), ("cv_b_pw2", s_rm[2])):
                gcv[n][j] = t
        dh, s_nm = norm_mod_bwd(dus, st["h"], dh2, ng[l, 0], sc_m, f"nm_a_bwd{l}")
        dng[l] = jnp.stack([s_nm[2], s_rm[1], s_nf[2], s_rf[1]])
        dmod[l] = jnp.concatenate([s_nm[0], s_nm[1], s_rm[0], s_nf[0], s_nf[1], s_rf[0]])
        pending = (l, gw)
    keys = [key for key, _, _ in layer_pieces(pending[0])]
    store_grads(pending[0], keys, communicate("ex", grad_exchange(pending[0], pending[1], keys).operands, "rs_x_last"))

    local = {"norm_g": jnp.stack(dng), "b_mod": jnp.stack(dmod)}
    local.update({n: jnp.stack(t) for n, t in gs5.items()})
    local.update({n: jnp.stack(t) for n, t in gcv.items()})
    pieces = [local[n].reshape(-1) for n in SMALL]
    sizes = [q.shape[0] for q in pieces]
    gathered = all_gather(_pack_flat(pieces, F32), "ag_grads")
    sums = _unpack_flat(sum_slots(gathered, "sum_grads"), sizes)
    grads = {}
    for n, t in zip(SMALL, sums):
        t = t.reshape(local[n].shape)
        if n in SMALL_SHARDED:
            nsh = p[n].shape[-1]
            t = lax.dynamic_slice_in_dim(t, me * nsh, nsh, axis=t.ndim - 1)
        grads[n] = t.reshape(p[n].shape)
    dmod_all = _unpack_flat(gathered, sizes)[SMALL.index("b_mod")].reshape(N_DEV, depth, N_DEV * nmod)
    dmod_cols = jnp.moveaxis(lax.dynamic_slice_in_dim(dmod_all, me * nmod, nmod, axis=2), 0, 1)
    grads["w_mod"] = mod_bwd(c_all, dmod_cols, "mod_bwd")
    for n in gbig:
        grads[n] = jnp.stack(gbig[n])

    delta, new_m, new_v = {}, {}, {}
    for n in BIG:
        shp = p[n].shape
        two = lambda t: t.reshape(-1, shp[-1])
        delta[n], new_m[n], new_v[n] = (t.reshape(shp) for t in
                                        adamw(two(p[n]), two(grads[n]), two(p["m_" + n]), two(p["v_" + n]), f"adamw_{n}"))
    sizes = [p[n].size for n in SMALL]
    packs = [_pack_flat([t[n].reshape(-1) for n in SMALL], F32)
             for t in (p, grads, {n: p["m_" + n] for n in SMALL}, {n: p["v_" + n] for n in SMALL})]
    for res, out in zip(adamw(*packs, "adamw_small"), (delta, new_m, new_v)):
        for n, t in zip(SMALL, _unpack_flat(res, sizes)):
            out[n] = t.reshape(p[n].shape)

    return (loss, dh.reshape(x.shape), *[grads[n] for n in WEIGHTS], *[delta[n] for n in WEIGHTS],
            *[new_m[n] for n in WEIGHTS], *[new_v[n] for n in WEIGHTS])
```

```python
import functools
import math

import jax
import jax.numpy as jnp
from jax import lax
from jax.experimental import pallas as pl
from jax.experimental.pallas import tpu as pltpu

F32, BF16 = jnp.float32, jnp.bfloat16
N_DEV = 8
AXES = ("x", "y", "c")
EPS = 1e-6
HEAD_DIM = 64
LANES = 128
SUBLANES = 8
VMEM_LIMIT = 56 * 1024 * 1024
S5_GROUP = 16
ADAM_LR, ADAM_B1, ADAM_B2, ADAM_EPS, ADAM_WD, ADAM_STEP = 0.001, 0.9, 0.999, 1e-08, 0.01, 10
NEG_CUTOFF = -104.0


def _cp(sem):
    return pltpu.CompilerParams(dimension_semantics=sem, vmem_limit_bytes=VMEM_LIMIT)


def _tile(n, cap, mult=LANES):
    best = None
    for t in range(mult, min(n, cap) + 1, mult):
        if n % t == 0:
            best = t
    return best if best is not None else n


def _sigmoid(x):
    return 1.0 / (1.0 + jnp.exp(-x))


_DIMS = {"nn": (((1,), (0,)), ((), ())), "nt": (((1,), (1,)), ((), ())), "tn": (((0,), (0,)), ((), ()))}


CAPS_FULL_K = (1024, 1536, 4096)
CAPS_PAIRS = (512, 1024, 1408)
CAPS_TN = (4096, 1536, 1024)


def _host_call(body, name, grid, in_specs, out_specs, out_shape, scratch, sem, args, car):
    if car is None:
        return pl.pallas_call(body, name=name, grid=grid, in_specs=in_specs, out_specs=out_specs, out_shape=out_shape,
                              scratch_shapes=scratch, compiler_params=_cp(sem))(*args), None
    n_in, n_out = len(in_specs), len(out_specs)
    cspecs, cshapes, cscratch = car.call_args()

    def wrapped(*refs):
        host, crefs = _carry_refs(refs, n_in, n_out, car)
        after = _carry_steps(car, crefs, grid)
        body(*host)
        after()

    res = pl.pallas_call(
        wrapped, name=name, grid=grid, in_specs=in_specs + cspecs, out_specs=out_specs + cspecs,
        out_shape=out_shape + cshapes, scratch_shapes=scratch + cscratch,
        compiler_params=_cp(("arbitrary",) * len(grid)))(*args, *car.operands)
    return res[:n_out], res[n_out:]


def mm(pairs, mode, out_dtype=F32, bias=None, name="mm", caps=None, car=None):
    a0, b0 = pairs[0]
    if mode == "nn":
        (m, k), n = a0.shape, b0.shape[1]
    elif mode == "nt":
        (m, k), n = a0.shape, b0.shape[0]
    else:
        (k, m), n = a0.shape, b0.shape[1]
    if caps is None:
        caps = CAPS_TN if mode == "tn" else (CAPS_FULL_K if len(pairs) == 1 or k <= 1024 else CAPS_PAIRS)
    tm, tn, tk = _tile(m, caps[0]), _tile(n, caps[1]), _tile(k, caps[2])
    nk = k // tk
    npairs = len(pairs)
    dims = _DIMS[mode]

    def body(*refs):
        ins = refs[:2 * npairs]
        bias_ref = refs[2 * npairs] if bias is not None else None
        o_ref, acc = refs[-2], refs[-1]
        kk = pl.program_id(2)
        part = None
        for p in range(npairs):
            d = lax.dot_general(ins[2 * p][...].astype(BF16), ins[2 * p + 1][...].astype(BF16), dims,
                                preferred_element_type=F32)
            part = d if part is None else part + d

        def finish(r):
            if bias_ref is not None:
                r = r + bias_ref[...]
            o_ref[...] = r.astype(o_ref.dtype)

        if nk == 1:
            finish(part)
        else:
            @pl.when(kk == 0)
            def _():
                acc[...] = part

            @pl.when(kk > 0)
            def _():
                acc[...] += part

            @pl.when(kk == nk - 1)
            def _():
                finish(acc[...])

    if mode == "nn":
        sa, sb = pl.BlockSpec((tm, tk), lambda i, j, kk: (i, kk)), pl.BlockSpec((tk, tn), lambda i, j, kk: (kk, j))
    elif mode == "nt":
        sa, sb = pl.BlockSpec((tm, tk), lambda i, j, kk: (i, kk)), pl.BlockSpec((tn, tk), lambda i, j, kk: (j, kk))
    else:
        sa, sb = pl.BlockSpec((tk, tm), lambda i, j, kk: (kk, i)), pl.BlockSpec((tk, tn), lambda i, j, kk: (kk, j))
    in_specs, args = [], []
    for a, b in pairs:
        in_specs += [sa, sb]
        args += [a, b]
    if bias is not None:
        in_specs.append(pl.BlockSpec((1, tn), lambda i, j, kk: (0, j)))
        args.append(bias.reshape(1, n).astype(F32))
    outs, carried = _host_call(
        body, name, (m // tm, n // tn, nk), in_specs, [pl.BlockSpec((tm, tn), lambda i, j, kk: (i, j))],
        [jax.ShapeDtypeStruct((m, n), out_dtype)], [pltpu.VMEM((tm, tn) if nk > 1 else (SUBLANES, LANES), F32)],
        ("parallel", "parallel", "arbitrary"), args, car)
    return outs[0] if car is None else (outs[0], carried)


def _act_fwd(kind, p1, p2):
    if kind == "swiglu":
        return p1 * _sigmoid(p1) * p2
    return p1 * _sigmoid(p2)


def _act_bwd(kind, d, p1, p2):
    if kind == "swiglu":
        s = _sigmoid(p1)
        return d * p2 * s * (1.0 + p1 * (1.0 - s)), d * (p1 * s)
    s = _sigmoid(p2)
    return d * s, d * p1 * s * (1.0 - s)


def mm_dual(a, w, bias, kind, pre_dtype, act_dtype, name, car=None):
    m, k = a.shape
    w1, w2 = w if isinstance(w, tuple) else (w, w)
    n = w1.shape[1] if isinstance(w, tuple) else w.shape[1] // 2
    tm, tn = _tile(m, 512), _tile(n, 1536)
    nb = n // tn
    nb2 = 0 if isinstance(w, tuple) else nb

    def body(*refs):
        a_ref, w1_ref, w2_ref = refs[:3]
        p1_ref, p2_ref, act_ref = refs[-3:]
        av = a_ref[...].astype(BF16)
        p1 = jnp.dot(av, w1_ref[...].astype(BF16), preferred_element_type=F32)
        p2 = jnp.dot(av, w2_ref[...].astype(BF16), preferred_element_type=F32)
        if bias is not None:
            p1 = p1 + refs[3][...]
            p2 = p2 + refs[4][...]
        p1_ref[...] = p1.astype(p1_ref.dtype)
        p2_ref[...] = p2.astype(p2_ref.dtype)
        act_ref[...] = _act_fwd(kind, p1, p2).astype(act_ref.dtype)

    in_specs = [pl.BlockSpec((tm, k), lambda i, j: (i, 0)), pl.BlockSpec((k, tn), lambda i, j: (0, j)),
                pl.BlockSpec((k, tn), lambda i, j: (0, j + nb2))]
    args = [a, w1, w2]
    if bias is not None:
        b2 = bias.reshape(1, 2 * n).astype(F32)
        in_specs += [pl.BlockSpec((1, tn), lambda i, j: (0, j)), pl.BlockSpec((1, tn), lambda i, j: (0, j + nb))]
        args += [b2, b2]
    ospec = pl.BlockSpec((tm, tn), lambda i, j: (i, j))
    outs, carried = _host_call(
        body, name, (m // tm, nb), in_specs, [ospec, ospec, ospec],
        [jax.ShapeDtypeStruct((m, n), pre_dtype), jax.ShapeDtypeStruct((m, n), pre_dtype),
         jax.ShapeDtypeStruct((m, n), act_dtype)], [], ("parallel", "parallel"), args, car)
    return outs if car is None else (outs, carried)


def mm_act_bwd(dy, w, p1, p2, kind, name, car=None):
    m, k = dy.shape
    n = w.shape[0]
    tm, tn = _tile(m, 512), _tile(n, 1536)

    def body(dy_ref, w_ref, p1_ref, p2_ref, d1_ref, d2_ref):
        dact = lax.dot_general(dy_ref[...].astype(BF16), w_ref[...].astype(BF16), _DIMS["nt"],
                               preferred_element_type=F32)
        d1, d2 = _act_bwd(kind, dact, p1_ref[...].astype(F32), p2_ref[...].astype(F32))
        d1_ref[...] = d1.astype(BF16)
        d2_ref[...] = d2.astype(BF16)

    spec = pl.BlockSpec((tm, tn), lambda i, j: (i, j))
    sd = jax.ShapeDtypeStruct((m, n), BF16)
    outs, carried = _host_call(
        body, name, (m // tm, n // tn),
        [pl.BlockSpec((tm, k), lambda i, j: (i, 0)), pl.BlockSpec((tn, k), lambda i, j: (j, 0)), spec, spec],
        [spec, spec], [sd, sd], [], ("parallel", "parallel"), [dy, w, p1, p2], car)
    return outs if car is None else (outs, carried)


def dual_bwd(dact, p1, p2, kind, name):
    m, n = dact.shape
    tm, tn = _tile(m, 512), _tile(n, 512)

    def body(d_ref, p1_ref, p2_ref, d1_ref, d2_ref, s_ref):
        d1, d2 = _act_bwd(kind, d_ref[...].astype(F32), p1_ref[...].astype(F32), p2_ref[...].astype(F32))
        d1_ref[...] = d1.astype(BF16)
        d2_ref[...] = d2.astype(BF16)

        @pl.when(pl.program_id(1) == 0)
        def _():
            s_ref[...] = jnp.zeros_like(s_ref)

        s_ref[0:1, :] += jnp.sum(d1, axis=0, keepdims=True)
        s_ref[1:2, :] += jnp.sum(d2, axis=0, keepdims=True)

    spec = pl.BlockSpec((tm, tn), lambda j, i: (i, j))
    return pl.pallas_call(
        body, name=name, grid=(n // tn, m // tm), in_specs=[spec, spec, spec],
        out_specs=[spec, spec, pl.BlockSpec((SUBLANES, tn), lambda j, i: (0, j))],
        out_shape=[jax.ShapeDtypeStruct((m, n), BF16), jax.ShapeDtypeStruct((m, n), BF16),
                   jax.ShapeDtypeStruct((SUBLANES, n), F32)],
        compiler_params=_cp(("parallel", "arbitrary")),
    )(dact, p1, p2)


def _rms(x):
    r = lax.rsqrt(jnp.mean(x * x, axis=-1, keepdims=True) + EPS)
    return x * r, r


def _vec8(*rows):
    d = rows[0].shape[-1]
    out = jnp.zeros((SUBLANES, d), F32)
    for i, r in enumerate(rows):
        out = out.at[i].set(r.reshape(d).astype(F32))
    return out


def norm_mod_fwd(h, g, scale, shift, out_dtypes, name):
    s, d = h.shape
    ts = _tile(s, 512, SUBLANES)
    vec = _vec8(g, 1.0 + scale, shift)

    def body(h_ref, v_ref, *outs):
        hh, _ = _rms(h_ref[...])
        u = hh * v_ref[0:1, :] * v_ref[1:2, :] + v_ref[2:3, :]
        for o in outs:
            o[...] = u.astype(o.dtype)

    spec = pl.BlockSpec((ts, d), lambda i: (i, 0))
    return pl.pallas_call(
        body, name=name, grid=(s // ts,), in_specs=[spec, pl.BlockSpec((SUBLANES, d), lambda i: (0, 0))],
        out_specs=[spec] * len(out_dtypes), out_shape=[jax.ShapeDtypeStruct((s, d), t) for t in out_dtypes],
        compiler_params=_cp(("parallel",)),
    )(h, vec)


def norm_mod_bwd(dus, h, dh_in, g, scale, name):
    s, d = h.shape
    ts = _tile(s, 256, SUBLANES)
    vec = _vec8(g, 1.0 + scale)
    nd = len(dus)

    def body(*refs):
        du = refs[0][...].astype(F32)
        for r in refs[1:nd]:
            du = du + r[...].astype(F32)
        h_ref, dhi_ref, v_ref, dh_ref, s_ref = refs[nd:]
        hh, r = _rms(h_ref[...])
        gg, sc1 = v_ref[0:1, :], v_ref[1:2, :]
        duh = du * hh
        dhh = du * (sc1 * gg)
        dh = r * (dhh - hh * jnp.mean(duh * (sc1 * gg), axis=-1, keepdims=True))
        dh_ref[...] = dhi_ref[...] + dh

        @pl.when(pl.program_id(0) == 0)
        def _():
            s_ref[...] = jnp.zeros_like(s_ref)

        col = jnp.sum(duh, axis=0, keepdims=True)
        s_ref[0:1, :] += jnp.sum(du, axis=0, keepdims=True)
        s_ref[1:2, :] += col * gg
        s_ref[2:3, :] += col * sc1

    spec = pl.BlockSpec((ts, d), lambda i: (i, 0))
    vspec = pl.BlockSpec((SUBLANES, d), lambda i: (0, 0))
    return pl.pallas_call(
        body, name=name, grid=(s // ts,), in_specs=[spec] * (nd + 2) + [vspec], out_specs=[spec, vspec],
        out_shape=[jax.ShapeDtypeStruct((s, d), F32), jax.ShapeDtypeStruct((SUBLANES, d), F32)],
        compiler_params=_cp(("arbitrary",)),
    )(*dus, h, dh_in, vec)


def resid_fwd(h, m, g, gate, name):
    s, d = h.shape
    ts = _tile(s, 512, SUBLANES)
    vec = _vec8(g, gate)

    def body(h_ref, m_ref, v_ref, o_ref):
        mh, _ = _rms(m_ref[...])
        o_ref[...] = h_ref[...] + v_ref[1:2, :] * (mh * v_ref[0:1, :])

    spec = pl.BlockSpec((ts, d), lambda i: (i, 0))
    return pl.pallas_call(
        body, name=name, grid=(s // ts,), in_specs=[spec, spec, pl.BlockSpec((SUBLANES, d), lambda i: (0, 0))],
        out_specs=spec, out_shape=jax.ShapeDtypeStruct((s, d), F32), compiler_params=_cp(("parallel",)),
    )(h, m, vec)


def resid_bwd(dh2, m, g, gate, out_dtype, name):
    s, d = m.shape
    ts = _tile(s, 256, SUBLANES)
    vec = _vec8(g, gate)

    def body(d_ref, m_ref, v_ref, dm_ref, s_ref):
        dh = d_ref[...]
        mh, r = _rms(m_ref[...])
        gg, gt = v_ref[0:1, :], v_ref[1:2, :]
        dhm = dh * mh
        dm = r * (dh * (gt * gg) - mh * jnp.mean(dhm * (gt * gg), axis=-1, keepdims=True))
        dm_ref[...] = dm.astype(dm_ref.dtype)

        @pl.when(pl.program_id(0) == 0)
        def _():
            s_ref[...] = jnp.zeros_like(s_ref)

        col = jnp.sum(dhm, axis=0, keepdims=True)
        s_ref[0:1, :] += col * gg
        s_ref[1:2, :] += col * gt
        s_ref[2:3, :] += jnp.sum(dm, axis=0, keepdims=True)

    spec = pl.BlockSpec((ts, d), lambda i: (i, 0))
    vspec = pl.BlockSpec((SUBLANES, d), lambda i: (0, 0))
    return pl.pallas_call(
        body, name=name, grid=(s // ts,), in_specs=[spec, spec, vspec], out_specs=[spec, vspec],
        out_shape=[jax.ShapeDtypeStruct((s, d), out_dtype), jax.ShapeDtypeStruct((SUBLANES, d), F32)],
        compiler_params=_cp(("arbitrary",)),
    )(dh2, m, vec)


def loss_and_grad(h, target, name):
    s, d = h.shape
    ts = _tile(s, 512, SUBLANES)

    def body(h_ref, t_ref, l_ref, dy_ref):
        e = h_ref[...] - t_ref[...]
        dy_ref[...] = e * (1.0 / d)

        @pl.when(pl.program_id(0) == 0)
        def _():
            l_ref[...] = jnp.zeros_like(l_ref)

        l_ref[...] += (0.5 / d) * jnp.sum(e * e)

    spec = pl.BlockSpec((ts, d), lambda i: (i, 0))
    lspec = pl.BlockSpec((SUBLANES, LANES), lambda i: (0, 0))
    return pl.pallas_call(
        body, name=name, grid=(s // ts,), in_specs=[spec, spec], out_specs=[lspec, spec],
        out_shape=[jax.ShapeDtypeStruct((SUBLANES, LANES), F32), jax.ShapeDtypeStruct((s, d), F32)],
        compiler_params=_cp(("arbitrary",)),
    )(h, target)


TK = 128
N_PAIR_HEADS = LANES // HEAD_DIM
TQ_ATTN, SLAB_ATTN = 128, 3


def _cum_matrices():
    a = jnp.arange(TK)
    ones = jnp.ones((TK, TK), F32)
    suffix = (a[:, None] > a[None, :]).astype(F32)
    prefix = (a[:, None] < a[None, :]).astype(F32)
    mk = lambda u: jnp.tile(jnp.concatenate([u, ones], axis=1), (2, 1)).astype(BF16)
    return mk(suffix), mk(prefix)


def _split_dot(x, cum):
    hi = x.astype(BF16)
    lo = (x - hi.astype(F32)).astype(BF16)
    return jnp.dot(jnp.concatenate([hi, lo], axis=1), cum, preferred_element_type=F32)


def _sb_slab(qs, k_slab, cum, nb, r, mask):
    m = qs.shape[0]
    z = lax.dot_general(qs, k_slab, _DIMS["nt"], preferred_element_type=F32)
    lb = jnp.minimum(z, 0.0) - jnp.log(1.0 + jnp.exp(-jnp.abs(z)))
    lk = lb - z
    if mask is not None:
        lk = jnp.where(mask, lk, 0.0)
    t = _split_dot(jnp.concatenate([lk[:, b * TK:(b + 1) * TK] for b in range(nb)], axis=0), cum)
    cs = [None] * nb
    for b in reversed(range(nb)):
        tb = t[b * m:(b + 1) * m]
        cs[b] = tb[:, :TK] + r
        r = r + tb[:, TK:]
    a = jnp.exp(lb + jnp.concatenate(cs, axis=1))
    if mask is not None:
        a = jnp.where(mask, a, 0.0)
    return lb, a, r


def _stack_heads(x, heads):
    return jnp.concatenate([jnp.where(hm, x, jnp.zeros_like(x)) for hm in heads], axis=0)


def _unstack_heads(xs, heads, tq):
    return jnp.where(heads[0], xs[0:tq], xs[tq:2 * tq])


def _head_masks(rows):
    lane = lax.broadcasted_iota(jnp.int32, (rows, LANES), 1)
    return [(lane >= hh * HEAD_DIM) & (lane < (hh + 1) * HEAD_DIM) for hh in range(N_PAIR_HEADS)]


def _slab_geometry(i, tq, nb):
    m = N_PAIR_HEADS * tq
    sb = jnp.maximum((i + 1) * (tq // TK) - nb, 0)
    rowq = lax.broadcasted_iota(jnp.int32, (m, nb * TK), 0) & (tq - 1)
    col = lax.broadcasted_iota(jnp.int32, (m, nb * TK), 1)
    mask = (col - rowq) < (i * tq - sb * TK)
    return sb, mask


def attn_fwd(qkv, name):
    s, d3 = qkv.shape
    d = d3 // 3
    npair = d // LANES
    tq = min(TQ_ATTN, s)
    nb = SLAB_ATTN
    m = N_PAIR_HEADS * tq
    scale = HEAD_DIM ** -0.5
    cum_s, _ = _cum_matrices()

    def body(q_ref, k_ref, v_ref, c_ref, o_ref, w_ref, b_ref, r_ref):
        i = pl.program_id(1)
        cum = c_ref[...]
        heads = _head_masks(tq)
        qs = _stack_heads((q_ref[...].astype(F32) * scale).astype(BF16), heads)
        sb, mask = _slab_geometry(i, tq, nb)
        off = pl.multiple_of(sb * TK, TK)
        lb, w, r = _sb_slab(qs, k_ref[pl.ds(off, nb * TK), :], cum, nb, jnp.zeros((m, TK), F32), mask)
        wb = w.astype(BF16)
        w_ref[...] = wb
        b_ref[...] = lb.astype(BF16)
        r_ref[...] = r
        acc = jnp.dot(wb, v_ref[pl.ds(off, nb * TK), :], preferred_element_type=F32)

        def cond(c):
            return jnp.logical_and(c[0] >= 0, jnp.max(c[1]) > NEG_CUTOFF)

        def step(c):
            j, r, acc = c
            off = pl.multiple_of(j * TK, TK)
            _, w, r = _sb_slab(qs, k_ref[pl.ds(off, TK), :], cum, 1, r, None)
            return j - 1, r, acc + jnp.dot(w.astype(BF16), v_ref[pl.ds(off, TK), :], preferred_element_type=F32)

        _, _, acc = lax.while_loop(cond, step, (sb - 1, r, acc))
        o_ref[...] = _unstack_heads(acc, heads, tq).astype(o_ref.dtype)

    return pl.pallas_call(
        body, name=name, grid=(npair, s // tq),
        in_specs=[pl.BlockSpec((tq, LANES), lambda p, i: (i, p)),
                  pl.BlockSpec((s, LANES), lambda p, i: (0, npair + p)),
                  pl.BlockSpec((s, LANES), lambda p, i: (0, 2 * npair + p)),
                  pl.BlockSpec((2 * TK, 2 * TK), lambda p, i: (0, 0))],
        out_specs=[pl.BlockSpec((tq, LANES), lambda p, i: (i, p)),
                   pl.BlockSpec((None, None, m, nb * TK), lambda p, i: (p, i, 0, 0)),
                   pl.BlockSpec((None, None, m, nb * TK), lambda p, i: (p, i, 0, 0)),
                   pl.BlockSpec((None, None, m, TK), lambda p, i: (p, i, 0, 0))],
        out_shape=[jax.ShapeDtypeStruct((s, d), BF16),
                   jax.ShapeDtypeStruct((npair, s // tq, m, nb * TK), BF16),
                   jax.ShapeDtypeStruct((npair, s // tq, m, nb * TK), BF16),
                   jax.ShapeDtypeStruct((npair, s // tq, m, TK), F32)],
        compiler_params=_cp(("parallel", "arbitrary")),
    )(qkv, qkv, qkv, cum_s)


def attn_bwd(qkv, do, saved, name):
    s, d3 = qkv.shape
    d = d3 // 3
    npair = d // LANES
    tq = min(TQ_ATTN, s)
    nb = SLAB_ATTN
    nq = s // tq
    m = N_PAIR_HEADS * tq
    scale = HEAD_DIM ** -0.5
    cum_s, cum_p = _cum_matrices()

    def body(q_ref, k_ref, v_ref, do_ref, w_ref, b_ref, r_ref, cs_ref, cp_ref, dq_ref, dk_ref, dv_ref,
             dk_acc, dv_acc, e_scr, b_scr):
        i = pl.program_id(1)

        @pl.when(i == 0)
        def _():
            dk_acc[...] = jnp.zeros_like(dk_acc)
            dv_acc[...] = jnp.zeros_like(dv_acc)

        cum_suf = cs_ref[...]
        cum_pre = cp_ref[...]
        heads = _head_masks(tq)
        qs = _stack_heads((q_ref[...].astype(F32) * scale).astype(BF16), heads)
        dos = _stack_heads(do_ref[...], heads)

        def left(off, n, r, mask):
            rows = pl.ds(off, n * TK)
            lb, a, r = _sb_slab(qs, k_ref[rows, :], cum_suf, n, r, mask)
            da = lax.dot_general(dos, v_ref[rows, :], _DIMS["nt"], preferred_element_type=F32)
            dv_acc[rows, :] += lax.dot_general(a.astype(BF16), dos, _DIMS["tn"], preferred_element_type=F32)
            return da * a, jnp.exp(lb), r

        def right(off, n, e, beta, pe, mask):
            rows = pl.ds(off, n * TK)
            t = _split_dot(jnp.concatenate([e[:, b * TK:(b + 1) * TK] for b in range(n)], axis=0), cum_pre)
            ps = [None] * n
            for b in range(n):
                tb = t[b * m:(b + 1) * m]
                ps[b] = tb[:, :TK] + pe
                pe = pe + tb[:, TK:]
            dz = e * (1.0 - beta) - beta * jnp.concatenate(ps, axis=1)
            if mask is not None:
                dz = jnp.where(mask, dz, 0.0)
            dzb = dz.astype(BF16)
            dk_acc[rows, :] += lax.dot_general(dzb, qs, _DIMS["tn"], preferred_element_type=F32)
            return pe, jnp.dot(dzb, k_ref[rows, :], preferred_element_type=F32)

        sb, mask = _slab_geometry(i, tq, nb)
        off0 = pl.multiple_of(sb * TK, TK)
        rows0 = pl.ds(off0, nb * TK)
        wb = w_ref[...]
        da0 = lax.dot_general(dos, v_ref[rows0, :], _DIMS["nt"], preferred_element_type=F32)
        dv_acc[rows0, :] += lax.dot_general(wb, dos, _DIMS["tn"], preferred_element_type=F32)
        e0, beta0, r = da0 * wb.astype(F32), jnp.exp(b_ref[...].astype(F32)), r_ref[...]

        def cond(c):
            return jnp.logical_and(c[0] >= 0, jnp.max(c[1]) > NEG_CUTOFF)

        def tail_left(c):
            j, r = c
            e_scr[j], b_scr[j], r = left(pl.multiple_of(j * TK, TK), 1, r, None)
            return j - 1, r

        jend, _ = lax.while_loop(cond, tail_left, (sb - 1, r))

        def tail_right(j, c):
            pe, dq = c
            pe, dqj = right(pl.multiple_of(j * TK, TK), 1, e_scr[j], b_scr[j], pe, None)
            return pe, dq + dqj

        pe, dq = lax.fori_loop(jend + 1, sb, tail_right, (jnp.zeros((m, TK), F32), jnp.zeros((m, LANES), F32)))
        _, dq0 = right(off0, nb, e0, beta0, pe, mask)
        dq_ref[...] = (_unstack_heads(dq + dq0, heads, tq) * scale).astype(dq_ref.dtype)

        @pl.when(i == nq - 1)
        def _():
            dk_ref[...] = dk_acc[...].astype(dk_ref.dtype)
            dv_ref[...] = dv_acc[...].astype(dv_ref.dtype)

    qspec = pl.BlockSpec((tq, LANES), lambda p, i: (i, p))
    full = lambda base: pl.BlockSpec((s, LANES), lambda p, i: (0, base + p))
    cspec = pl.BlockSpec((2 * TK, 2 * TK), lambda p, i: (0, 0))
    sd = jax.ShapeDtypeStruct((s, d), BF16)
    slab = pl.BlockSpec((None, None, m, nb * TK), lambda p, i: (p, i, 0, 0))
    return pl.pallas_call(
        body, name=name, grid=(npair, nq),
        in_specs=[qspec, full(npair), full(2 * npair), qspec, slab, slab,
                  pl.BlockSpec((None, None, m, TK), lambda p, i: (p, i, 0, 0)), cspec, cspec],
        out_specs=[qspec, full(0), full(0)], out_shape=[sd, sd, sd],
        scratch_shapes=[pltpu.VMEM((s, LANES), F32), pltpu.VMEM((s, LANES), F32),
                        pltpu.VMEM((s // TK, m, TK), F32), pltpu.VMEM((s // TK, m, TK), F32)],
        compiler_params=_cp(("parallel", "arbitrary")),
    )(qkv, qkv, qkv, do, *saved, cum_s, cum_p)


HALO = 32
CONV_ROWS = 128


def _ln_swish(hc, g, b):
    mu = jnp.mean(hc, axis=-1, keepdims=True)
    xc = hc - mu
    rstd = lax.rsqrt(jnp.mean(xc * xc, axis=-1, keepdims=True) + EPS)
    xh = xc * rstd
    hn = xh * g + b
    return xh, rstd, hn


def conv_mid_fwd(x, w_dw, b_dw, ln_g, ln_b, name):
    s, d = x.shape
    width = w_dw.shape[0]
    ts = _tile(s, 256, CONV_ROWS)
    base = HALO - (width - 1)
    wpad = jnp.zeros((HALO, d), F32).at[:width].set(w_dw.astype(F32))
    vec = _vec8(b_dw, ln_g, ln_b)

    def body(x_ref, w_ref, v_ref, hc_ref, hs_ref, win):
        i = pl.program_id(0)

        @pl.when(i == 0)
        def _():
            win[0:HALO, :] = jnp.zeros((HALO, d), F32)

        @pl.when(i > 0)
        def _():
            win[0:HALO, :] = win[ts:ts + HALO, :]

        win[HALO:HALO + ts, :] = x_ref[...]
        for rc in range(ts // CONV_ROWS):
            for lc in range(d // LANES):
                cols = slice(lc * LANES, (lc + 1) * LANES)
                acc = jnp.broadcast_to(v_ref[0:1, cols], (CONV_ROWS, LANES))
                for k in range(width):
                    acc = acc + w_ref[k:k + 1, cols] * win[pl.ds(rc * CONV_ROWS + base + k, CONV_ROWS), cols]
                hc_ref[rc * CONV_ROWS:(rc + 1) * CONV_ROWS, cols] = acc
        _, _, hn = _ln_swish(hc_ref[...], v_ref[1:2, :], v_ref[2:3, :])
        hs_ref[...] = (hn * _sigmoid(hn)).astype(hs_ref.dtype)

    spec = pl.BlockSpec((ts, d), lambda i: (i, 0))
    return pl.pallas_call(
        body, name=name, grid=(s // ts,),
        in_specs=[spec, pl.BlockSpec((HALO, d), lambda i: (0, 0)), pl.BlockSpec((SUBLANES, d), lambda i: (0, 0))],
        out_specs=[spec, spec], out_shape=[jax.ShapeDtypeStruct((s, d), F32), jax.ShapeDtypeStruct((s, d), BF16)],
        scratch_shapes=[pltpu.VMEM((ts + HALO, d), F32)],
        compiler_params=_cp(("arbitrary",)),
    )(x, wpad, vec)


def conv_mid_bwd(dhs, hc, x, w_dw, ln_g, ln_b, name):
    s, d = x.shape
    width = w_dw.shape[0]
    ts = _tile(s, 256, CONV_ROWS)
    nt = s // ts
    base = HALO - (width - 1)
    wpad = jnp.zeros((HALO, d), F32).at[:width].set(w_dw.astype(F32))
    vec = _vec8(ln_g, ln_b)

    def body(dhs_ref, hc_ref, x_ref, xh_ref, w_ref, v_ref, dx_ref, dw_ref, s_ref, dwin, xwin):
        i = pl.program_id(0)

        @pl.when(i == 0)
        def _():
            dwin[ts:ts + HALO, :] = jnp.zeros((HALO, d), F32)
            dw_ref[...] = jnp.zeros_like(dw_ref)
            s_ref[...] = jnp.zeros_like(s_ref)

        @pl.when(i > 0)
        def _():
            dwin[ts:ts + HALO, :] = dwin[0:HALO, :]

        @pl.when(i == nt - 1)
        def _():
            xwin[0:HALO, :] = jnp.zeros((HALO, d), F32)

        @pl.when(i < nt - 1)
        def _():
            xwin[0:HALO, :] = xh_ref[...]

        xwin[HALO:HALO + ts, :] = x_ref[...]
        g = v_ref[0:1, :]
        xh, rstd, hn = _ln_swish(hc_ref[...], g, v_ref[1:2, :])
        sig = _sigmoid(hn)
        dhn = dhs_ref[...].astype(F32) * (sig * (1.0 + hn * (1.0 - sig)))
        dxh = dhn * g
        dhc = rstd * (dxh - jnp.mean(dxh, axis=-1, keepdims=True) - xh * jnp.mean(dxh * xh, axis=-1, keepdims=True))
        dwin[0:ts, :] = dhc
        s_ref[0:1, :] += jnp.sum(dhc, axis=0, keepdims=True)
        s_ref[1:2, :] += jnp.sum(dhn * xh, axis=0, keepdims=True)
        s_ref[2:3, :] += jnp.sum(dhn, axis=0, keepdims=True)
        for rc in range(ts // CONV_ROWS):
            for lc in range(d // LANES):
                cols = slice(lc * LANES, (lc + 1) * LANES)
                r0 = rc * CONV_ROWS
                dch = dwin[r0:r0 + CONV_ROWS, cols]
                acc = jnp.zeros((CONV_ROWS, LANES), F32)
                for k in range(width):
                    acc = acc + w_ref[k:k + 1, cols] * dwin[pl.ds(r0 + (width - 1) - k, CONV_ROWS), cols]
                    dw_ref[k:k + 1, cols] += jnp.sum(dch * xwin[pl.ds(r0 + base + k, CONV_ROWS), cols], axis=0,
                                                     keepdims=True)
                dx_ref[r0:r0 + CONV_ROWS, cols] = acc

    rev = pl.BlockSpec((ts, d), lambda i: (nt - 1 - i, 0))
    halo = pl.BlockSpec((HALO, d), lambda i: (jnp.maximum((nt - 1 - i) * (ts // HALO) - 1, 0), 0))
    vspec = pl.BlockSpec((SUBLANES, d), lambda i: (0, 0))
    wspec = pl.BlockSpec((HALO, d), lambda i: (0, 0))
    return pl.pallas_call(
        body, name=name, grid=(nt,), in_specs=[rev, rev, rev, halo, wspec, vspec],
        out_specs=[rev, wspec, vspec],
        out_shape=[jax.ShapeDtypeStruct((s, d), F32), jax.ShapeDtypeStruct((HALO, d), F32),
                   jax.ShapeDtypeStruct((SUBLANES, d), F32)],
        scratch_shapes=[pltpu.VMEM((ts + HALO, d), F32), pltpu.VMEM((ts + HALO, d), F32)],
        compiler_params=_cp(("arbitrary",)),
    )(dhs, hc, x, x, wpad, vec)


S5_KB = 256
SCAN_LANES = 256
SCAN_UNROLL = 4
GELU_C = math.sqrt(2.0 / math.pi)
GELU_A = 0.044715


def _gelu(x):
    return 0.5 * x * (1.0 + jnp.tanh(GELU_C * (x + GELU_A * x * x * x)))


def _gelu_grad(x):
    th = jnp.tanh(GELU_C * (x + GELU_A * x * x * x))
    return 0.5 * (1.0 + th) + 0.5 * x * (1.0 - th * th) * GELU_C * (1.0 + 3.0 * GELU_A * x * x)


def _s5_discretize(lam_re, lam_im, log_dt, b_re, b_im):
    dt = jnp.exp(log_dt)[:, None]
    mag = jnp.exp(lam_re * dt)
    ar, ai = mag * jnp.cos(lam_im * dt), mag * jnp.sin(lam_im * dt)
    den = lam_re * lam_re + lam_im * lam_im
    er = ((ar - 1) * lam_re + ai * lam_im) / den
    ei = (ai * lam_re - (ar - 1) * lam_im) / den
    bbr = er[..., None] * b_re - ei[..., None] * b_im
    bbi = er[..., None] * b_im + ei[..., None] * b_re
    return ar, ai, bbr, bbi


def _blockdiag(w, nkb):
    g, r, c = w.shape
    gpb = g // nkb
    eye = jnp.eye(gpb, dtype=w.dtype)
    return jnp.einsum("kgrc,gh->kgrhc", w.reshape(nkb, gpb, r, c), eye).reshape(nkb, gpb * r, gpb * c)


def _blockdiag_extract(m, g):
    nkb = m.shape[0]
    gpb = g // nkb
    r, c = m.shape[1] // gpb, m.shape[2] // gpb
    eye = jnp.eye(gpb, dtype=m.dtype)
    return jnp.einsum("kgrhc,gh->kgrc", m.reshape(nkb, gpb, r, gpb, c), eye).reshape(g, r, c)


def _scan_powers(ar, ai, reverse):
    ar = ar.reshape(-1)
    ai = (-ai if reverse else ai).reshape(-1)
    cmul = lambda x, y: (x[0] * y[0] - x[1] * y[1], x[0] * y[1] + x[1] * y[0])
    a1 = (ar, ai)
    a2 = cmul(a1, a1)
    a4 = cmul(a2, a2)
    r = jnp.arange(SUBLANES)[:, None]
    rows = []
    for sft, p in ((1, a1), (2, a2), (4, a4)):
        keep = (r + sft <= SUBLANES - 1) if reverse else (r >= sft)
        rows += [jnp.where(keep, p[0][None, :], 0.0), jnp.where(keep, p[1][None, :], 0.0)]
    pows = [a1]
    for _ in range(SUBLANES - 1):
        pows.append(cmul(pows[-1], a1))
    if reverse:
        pows = pows[::-1]
    rows += [jnp.stack([p[0] for p in pows]), jnp.stack([p[1] for p in pows])]
    return jnp.concatenate(rows, axis=0).astype(F32)


def _scan_tile(sr, si, pw_ref, car, nrg, reverse):
    nsb = sr.shape[1]
    ch = min(SCAN_LANES, nsb)
    nch = nsb // ch
    row = 0 if reverse else SUBLANES - 1

    unroll = SCAN_UNROLL if nrg % SCAN_UNROLL == 0 else 1

    def step(t, carry):
        carry = list(carry)
        for u in range(unroll):
            g = t * unroll + u
            rg = (nrg - 1 - g) if reverse else g
            off = pl.multiple_of(rg * SUBLANES, SUBLANES)
            for c in range(nch):
                cols = slice(c * ch, (c + 1) * ch)
                cr, ci = carry[2 * c], carry[2 * c + 1]
                br = sr[pl.ds(off, SUBLANES), cols]
                bi = si[pl.ds(off, SUBLANES), cols]
                for idx, sft in enumerate((1, 2, 4)):
                    sh = SUBLANES - sft if reverse else sft
                    tr = pltpu.roll(br, sh, axis=0)
                    ti = pltpu.roll(bi, sh, axis=0)
                    mr = pw_ref[16 * idx:16 * idx + 8, cols]
                    mi = pw_ref[16 * idx + 8:16 * idx + 16, cols]
                    br, bi = br + mr * tr - mi * ti, bi + mr * ti + mi * tr
                apr, api = pw_ref[48:56, cols], pw_ref[56:64, cols]
                xr = br + apr * cr - api * ci
                xi = bi + apr * ci + api * cr
                sr[pl.ds(off, SUBLANES), cols] = xr
                si[pl.ds(off, SUBLANES), cols] = xi
                carry[2 * c] = jnp.broadcast_to(xr[row:row + 1, :], xr.shape)
                carry[2 * c + 1] = jnp.broadcast_to(xi[row:row + 1, :], xi.shape)
        return tuple(carry)

    init = []
    for c in range(nch):
        cols = slice(c * ch, (c + 1) * ch)
        init += [car[0:SUBLANES, cols], car[SUBLANES:2 * SUBLANES, cols]]
    fin = lax.fori_loop(0, nrg // unroll, step, tuple(init))
    for c in range(nch):
        cols = slice(c * ch, (c + 1) * ch)
        car[0:SUBLANES, cols] = fin[2 * c]
        car[SUBLANES:2 * SUBLANES, cols] = fin[2 * c + 1]


def s5_fwd(u, wb_r, wb_i, wc_r, wc_i, pw, d_skip, name):
    s, d = u.shape
    nkb, kb, nsb = wb_r.shape
    ts = _tile(s, 256, SUBLANES)
    dvec = _vec8(d_skip)

    def body(u_ref, wbr, wbi, wcr, wci, pw_ref, dv_ref, xr_ref, xi_ref, yy_ref, g_ref, sr, si, car):
        @pl.when(pl.program_id(1) == 0)
        def _():
            car[...] = jnp.zeros_like(car)

        uu = u_ref[...]
        ub = uu.astype(BF16)
        sr[...] = jnp.dot(ub, wbr[...], preferred_element_type=F32)
        si[...] = jnp.dot(ub, wbi[...], preferred_element_type=F32)
        _scan_tile(sr, si, pw_ref, car, ts // SUBLANES, False)
        xr, xi = sr[...], si[...]
        xr_ref[...] = xr
        xi_ref[...] = xi
        y = (jnp.dot(xr.astype(BF16), wcr[...], preferred_element_type=F32)
             + jnp.dot(xi.astype(BF16), wci[...], preferred_element_type=F32) + dv_ref[0:1, :] * uu)
        yy_ref[...] = y
        g_ref[...] = _gelu(y).astype(g_ref.dtype)

    cspec = pl.BlockSpec((ts, kb), lambda k, i: (i, k))
    sspec = pl.BlockSpec((ts, nsb), lambda k, i: (i, k))
    wbspec = pl.BlockSpec((None, kb, nsb), lambda k, i: (k, 0, 0))
    wcspec = pl.BlockSpec((None, nsb, kb), lambda k, i: (k, 0, 0))
    ns = nkb * nsb
    return pl.pallas_call(
        body, name=name, grid=(nkb, s // ts),
        in_specs=[cspec, wbspec, wbspec, wcspec, wcspec, pl.BlockSpec((64, nsb), lambda k, i: (0, k)),
                  pl.BlockSpec((SUBLANES, kb), lambda k, i: (0, k))],
        out_specs=[sspec, sspec, cspec, cspec],
        out_shape=[jax.ShapeDtypeStruct((s, ns), F32), jax.ShapeDtypeStruct((s, ns), F32),
                   jax.ShapeDtypeStruct((s, d), F32), jax.ShapeDtypeStruct((s, d), BF16)],
        scratch_shapes=[pltpu.VMEM((ts, nsb), F32), pltpu.VMEM((ts, nsb), F32), pltpu.VMEM((2 * SUBLANES, nsb), F32)],
        compiler_params=_cp(("parallel", "arbitrary")),
    )(u, wb_r, wb_i, wc_r, wc_i, pw, dvec)


def s5_bwd(dg, yy, u, xr, xi, wb_r, wb_i, wc_r, wc_i, pwb, d_skip, name):
    s, d = u.shape
    nkb, kb, nsb = wb_r.shape
    ns = nkb * nsb
    ts = _tile(s, 256, SUBLANES)
    nt = s // ts
    dvec = _vec8(d_skip)

    def body(dg_ref, yy_ref, u_ref, xr_ref, xi_ref, xrp_ref, xip_ref, wbr, wbi, wcr, wci, pw_ref, dv_ref,
             du_ref, dwbr, dwbi, dwcr, dwci, da_ref, dd_ref, sr, si, car):
        i = pl.program_id(1)

        @pl.when(i == 0)
        def _():
            car[...] = jnp.zeros_like(car)
            for r in (dwbr, dwbi, dwcr, dwci, da_ref, dd_ref):
                r[...] = jnp.zeros_like(r)

        uu = u_ref[...]
        dyy = dg_ref[...] * _gelu_grad(yy_ref[...])
        dd_ref[0:1, :] += jnp.sum(dyy * uu, axis=0, keepdims=True)
        dyb = dyy.astype(BF16)
        sr[...] = lax.dot_general(dyb, wcr[...], _DIMS["nt"], preferred_element_type=F32)
        si[...] = lax.dot_general(dyb, wci[...], _DIMS["nt"], preferred_element_type=F32)
        _scan_tile(sr, si, pw_ref, car, ts // SUBLANES, True)
        gr, gi = sr[...], si[...]
        grb, gib = gr.astype(BF16), gi.astype(BF16)
        xrt, xit = xr_ref[...], xi_ref[...]
        dwcr[...] += lax.dot_general(xrt.astype(BF16), dyb, _DIMS["tn"], preferred_element_type=F32)
        dwci[...] += lax.dot_general(xit.astype(BF16), dyb, _DIMS["tn"], preferred_element_type=F32)
        ub = uu.astype(BF16)
        dwbr[...] += lax.dot_general(ub, grb, _DIMS["tn"], preferred_element_type=F32)
        dwbi[...] += lax.dot_general(ub, gib, _DIMS["tn"], preferred_element_type=F32)
        du_ref[...] = (lax.dot_general(grb, wbr[...], _DIMS["nt"], preferred_element_type=F32)
                       + lax.dot_general(gib, wbi[...], _DIMS["nt"], preferred_element_type=F32)
                       + dyy * dv_ref[0:1, :])
        has_prev = (i < nt - 1).astype(F32)
        rowid = lax.broadcasted_iota(jnp.int32, (ts, nsb), 0)
        pr = jnp.broadcast_to(xrp_ref[SUBLANES - 1:SUBLANES, :] * has_prev, (ts, nsb))
        pi = jnp.broadcast_to(xip_ref[SUBLANES - 1:SUBLANES, :] * has_prev, (ts, nsb))
        xpr = jnp.where(rowid == 0, pr, pltpu.roll(xrt, 1, axis=0))
        xpi = jnp.where(rowid == 0, pi, pltpu.roll(xit, 1, axis=0))
        da_ref[0:1, :] += jnp.sum(gr * xpr + gi * xpi, axis=0, keepdims=True)
        da_ref[1:2, :] += jnp.sum(gi * xpr - gr * xpi, axis=0, keepdims=True)

    cspec = pl.BlockSpec((ts, kb), lambda k, i: (nt - 1 - i, k))
    sspec = pl.BlockSpec((ts, nsb), lambda k, i: (nt - 1 - i, k))
    pspec = pl.BlockSpec((SUBLANES, nsb), lambda k, i: (jnp.maximum((nt - 1 - i) * (ts // SUBLANES) - 1, 0), k))
    wbspec = pl.BlockSpec((None, kb, nsb), lambda k, i: (k, 0, 0))
    wcspec = pl.BlockSpec((None, nsb, kb), lambda k, i: (k, 0, 0))
    v8s = pl.BlockSpec((SUBLANES, nsb), lambda k, i: (0, k))
    v8c = pl.BlockSpec((SUBLANES, kb), lambda k, i: (0, k))
    return pl.pallas_call(
        body, name=name, grid=(nkb, nt),
        in_specs=[cspec, cspec, cspec, sspec, sspec, pspec, pspec, wbspec, wbspec, wcspec, wcspec,
                  pl.BlockSpec((64, nsb), lambda k, i: (0, k)), v8c],
        out_specs=[cspec, wbspec, wbspec, wcspec, wcspec, v8s, v8c],
        out_shape=[jax.ShapeDtypeStruct((s, d), F32),
                   jax.ShapeDtypeStruct((nkb, kb, nsb), F32), jax.ShapeDtypeStruct((nkb, kb, nsb), F32),
                   jax.ShapeDtypeStruct((nkb, nsb, kb), F32), jax.ShapeDtypeStruct((nkb, nsb, kb), F32),
                   jax.ShapeDtypeStruct((SUBLANES, ns), F32), jax.ShapeDtypeStruct((SUBLANES, d), F32)],
        scratch_shapes=[pltpu.VMEM((ts, nsb), F32), pltpu.VMEM((ts, nsb), F32), pltpu.VMEM((2 * SUBLANES, nsb), F32)],
        compiler_params=_cp(("parallel", "arbitrary")),
    )(dg, yy, u, xr, xi, xr, xi, wb_r, wb_i, wc_r, wc_i, pwb, dvec)


def s5_operands(lam_re, lam_im, log_dt, b_re, b_im, c_re, c_im, d):
    ar, ai, bbr, bbi = _s5_discretize(lam_re, lam_im, log_dt, b_re, b_im)
    nkb = max(d // S5_KB, 1)
    wb_r = _blockdiag(bbr.transpose(0, 2, 1), nkb).astype(BF16)
    wb_i = _blockdiag(bbi.transpose(0, 2, 1), nkb).astype(BF16)
    wc_r = _blockdiag(c_re.transpose(0, 2, 1), nkb).astype(BF16)
    wc_i = _blockdiag(-c_im.transpose(0, 2, 1), nkb).astype(BF16)
    return wb_r, wb_i, wc_r, wc_i, _scan_powers(ar, ai, False), _scan_powers(ar, ai, True)


MESH = pl.DeviceIdType.MESH
HBM_SPEC = pl.BlockSpec(memory_space=pltpu.HBM)


def _me():
    return 4 * lax.axis_index("x") + 2 * lax.axis_index("y") + lax.axis_index("c")


N_COPIES = N_DEV - 1


class Carried:
    def __init__(self, kind, operands):
        self.kind, self.operands = kind, list(operands)
        self.n = len(self.operands)

    def call_args(self):
        shapes = [jax.ShapeDtypeStruct((N_DEV,) + tuple(a.shape[-2:]), a.dtype) for a in self.operands]
        scratch = [pltpu.SemaphoreType.DMA((N_COPIES * self.n,)), pltpu.SemaphoreType.DMA((N_COPIES * self.n,)),
                   pltpu.SemaphoreType.DMA((self.n,))]
        return [HBM_SPEC] * self.n, shapes, scratch

    def _plan(self, t, x_ref, out_ref, send, recv, loc):
        x, y, c = lax.axis_index("x"), lax.axis_index("y"), lax.axis_index("c")
        me = 4 * x + 2 * y + c

        def rdma(k, src, dst, to):
            return pltpu.make_async_remote_copy(src_ref=src, dst_ref=dst, send_sem=send.at[N_COPIES * t + k],
                                                recv_sem=recv.at[N_COPIES * t + k], device_id=to, device_id_type=MESH)

        if self.kind == "ex":
            peers = [(1 - x if k & 4 else x, 1 - y if k & 2 else y, 1 - c if k & 1 else c) for k in range(1, N_DEV)]
            sends = [rdma(k, x_ref.at[4 * px + 2 * py + pc], out_ref.at[me], (px, py, pc))
                     for k, (px, py, pc) in enumerate(peers)]
            local = pltpu.make_async_copy(x_ref.at[me], out_ref.at[me], loc.at[t])
            return dict(local=local, first=sends, relay_on=[], relays=[], arrive=sends)
        sibling = (x, y, 1 - c)
        chips = [(1 - x, y), (x, 1 - y), (1 - x, 1 - y)]
        slot = lambda px, py, pc: out_ref.at[4 * px + 2 * py + pc]
        first = [rdma(0, x_ref, slot(x, y, c), sibling)]
        first += [rdma(1 + j, x_ref, slot(x, y, c), (*chip, c)) for j, chip in enumerate(chips)]
        relay_on = [rdma(1 + j, slot(*chip, c), slot(*chip, c), (x, y, c)) for j, chip in enumerate(chips)]
        relays = [rdma(4 + j, slot(*chip, c), slot(*chip, c), sibling) for j, chip in enumerate(chips)]
        arrive = [rdma(0, slot(*sibling), slot(*sibling), (x, y, c))]
        arrive += [rdma(4 + j, slot(*chip, 1 - c), slot(*chip, 1 - c), (x, y, c)) for j, chip in enumerate(chips)]
        local = pltpu.make_async_copy(x_ref, slot(x, y, c), loc.at[t])
        return dict(local=local, first=first, relay_on=relay_on, relays=relays, arrive=arrive)

    def _plans(self, refs):
        xs, outs, (send, recv, loc) = refs[:self.n], refs[self.n:2 * self.n], refs[2 * self.n:]
        return [self._plan(t, xs[t], outs[t], send, recv, loc) for t in range(self.n)]

    def begin(self, refs):
        for p in self._plans(refs):
            p["local"].start()
            for cp in p["first"]:
                cp.start()

    def finish(self, refs):
        plans = self._plans(refs)
        for p in plans:
            for landed, relay in zip(p["relay_on"], p["relays"]):
                landed.wait_recv()
                relay.start()
        for p in plans:
            for cp in p["arrive"]:
                cp.wait_recv()
            for cp in p["first"] + p["relays"]:
                cp.wait_send()
            p["local"].wait()


def _carry_refs(refs, n_in, n_out, car):
    if car is None:
        return list(refs), None
    n = car.n
    host = list(refs[:n_in]) + list(refs[n_in + n:n_in + n + n_out]) + list(refs[n_in + 2 * n + n_out:-3])
    return host, list(refs[n_in:n_in + n]) + list(refs[n_in + n + n_out:n_in + 2 * n + n_out]) + list(refs[-3:])


def _carry_steps(car, crefs, grid):
    if car is None:
        return lambda: None
    ids = [pl.program_id(a) for a in range(len(grid))]
    first = functools.reduce(jnp.logical_and, [i == 0 for i in ids])
    last = functools.reduce(jnp.logical_and, [i == g - 1 for i, g in zip(ids, grid)])

    @pl.when(first)
    def _():
        car.begin(crefs)

    def after():
        @pl.when(last)
        def _():
            car.finish(crefs)

    return after


def communicate(kind, operands, name):
    car = Carried(kind, operands)
    specs, shapes, scratch = car.call_args()

    def body(*refs):
        car.begin(refs)
        car.finish(refs)

    return pl.pallas_call(body, name=name, in_specs=specs, out_specs=specs, out_shape=shapes,
                          scratch_shapes=scratch)(*car.operands)


def all_gather(shard, name):
    return communicate("ag", [shard], name)[0]


def sum_slots(parts, name):
    _, m, n = parts.shape
    tm = _tile(m, 256, SUBLANES)

    def body(p_ref, o_ref):
        acc = p_ref[0].astype(F32)
        for q in range(1, N_DEV):
            acc = acc + p_ref[q].astype(F32)
        o_ref[...] = acc

    return pl.pallas_call(
        body, name=name, grid=(m // tm,), in_specs=[pl.BlockSpec((N_DEV, tm, n), lambda i: (0, i, 0))],
        out_specs=pl.BlockSpec((tm, n), lambda i: (i, 0)), out_shape=jax.ShapeDtypeStruct((m, n), F32),
        compiler_params=_cp(("parallel",)),
    )(parts)


PACK_COLS = 1024
PACK_ROWS = 16


def _pack_flat(pieces, dtype):
    lead = pieces[0].shape[:-1]
    flat = jnp.concatenate([p.astype(dtype) for p in pieces], axis=-1)
    unit = PACK_COLS * PACK_ROWS
    total = -(-flat.shape[-1] // unit) * unit
    flat = jnp.pad(flat, [(0, 0)] * len(lead) + [(0, total - flat.shape[-1])])
    return flat.reshape(*lead, total // PACK_COLS, PACK_COLS)


def _unpack_flat(packed, sizes):
    lead = packed.shape[:-2]
    flat = packed.reshape(*lead, -1)
    out, off = [], 0
    for n in sizes:
        out.append(flat[..., off:off + n])
        off += n
    return out


def mod_fwd(c_all, w_mod, b_cols, name):
    nl, d, n = w_mod.shape

    def body(c_ref, w_ref, b_ref, o_ref):
        cv = c_ref[...]
        sc = (cv * _sigmoid(cv)).astype(BF16)
        o_ref[...] = jnp.dot(sc, w_ref[...].astype(BF16), preferred_element_type=F32) + b_ref[...]

    return pl.pallas_call(
        body, name=name, grid=(nl,),
        in_specs=[pl.BlockSpec((N_DEV, d), lambda l: (0, 0)), pl.BlockSpec((None, d, n), lambda l: (l, 0, 0)),
                  pl.BlockSpec((None, 1, n), lambda l: (l, 0, 0))],
        out_specs=pl.BlockSpec((None, N_DEV, n), lambda l: (l, 0, 0)),
        out_shape=jax.ShapeDtypeStruct((nl, N_DEV, n), F32), compiler_params=_cp(("parallel",)),
    )(c_all, w_mod, b_cols.reshape(nl, 1, n))


def mod_bwd(c_all, dmod_cols, name):
    nl, _, n = dmod_cols.shape
    d = c_all.shape[1]

    def body(c_ref, g_ref, o_ref):
        cv = c_ref[...]
        sc = (cv * _sigmoid(cv)).astype(BF16)
        o_ref[...] = lax.dot_general(sc, g_ref[...].astype(BF16), _DIMS["tn"], preferred_element_type=F32)

    return pl.pallas_call(
        body, name=name, grid=(nl,),
        in_specs=[pl.BlockSpec((N_DEV, d), lambda l: (0, 0)), pl.BlockSpec((None, N_DEV, n), lambda l: (l, 0, 0))],
        out_specs=pl.BlockSpec((None, d, n), lambda l: (l, 0, 0)),
        out_shape=jax.ShapeDtypeStruct((nl, d, n), F32), compiler_params=_cp(("parallel",)),
    )(c_all, dmod_cols)


def adamw(w, g, m, v, name):
    r, c = w.shape
    tr = _tile(r, 512, SUBLANES)
    c1 = 1.0 - ADAM_B1 ** ADAM_STEP
    c2 = 1.0 - ADAM_B2 ** ADAM_STEP

    def body(w_ref, g_ref, m_ref, v_ref, d_ref, nm_ref, nv_ref):
        gg = g_ref[...]
        nm = ADAM_B1 * m_ref[...] + (1.0 - ADAM_B1) * gg
        nv = ADAM_B2 * v_ref[...] + (1.0 - ADAM_B2) * (gg * gg)
        nm_ref[...] = nm
        nv_ref[...] = nv
        d_ref[...] = -ADAM_LR * ((nm / c1) / (jnp.sqrt(nv / c2) + ADAM_EPS) + ADAM_WD * w_ref[...])

    spec = pl.BlockSpec((tr, c), lambda i: (i, 0))
    sd = jax.ShapeDtypeStruct((r, c), F32)
    return pl.pallas_call(
        body, name=name, grid=(r // tr,), in_specs=[spec] * 4, out_specs=[spec] * 3, out_shape=[sd, sd, sd],
        compiler_params=_cp(("parallel",)),
    )(w, g, m, v)


WEIGHTS = ["norm_g", "w_mod", "b_mod", "sb_w_qkv", "sb_w_o", "s5_lam_re", "s5_lam_im", "s5_log_dt", "s5_b_re",
           "s5_b_im", "s5_c_re", "s5_c_im", "s5_d", "s5_w_glu", "s5_b_glu", "cv_w_pw1", "cv_b_pw1", "cv_w_dw",
           "cv_b_dw", "cv_ln_g", "cv_ln_b", "cv_w_pw2", "cv_b_pw2", "ffn_w_gate", "ffn_w_up", "ffn_w_down"]
BIG = ["w_mod", "sb_w_qkv", "sb_w_o", "s5_w_glu", "cv_w_pw1", "cv_w_pw2", "ffn_w_gate", "ffn_w_up", "ffn_w_down"]
SMALL_SHARDED = ["norm_g", "cv_b_pw1", "cv_w_dw", "cv_b_dw", "cv_ln_g", "cv_ln_b", "cv_b_pw2"]
SMALL = [n for n in WEIGHTS if n not in BIG]
FFN_KEYS = ["gate", "up"]
FFN_ALL = ["gate", "up", "down"]


def _unshard_last(part, local_shape):
    a = jnp.moveaxis(part.reshape((N_DEV,) + tuple(local_shape)), 0, -2)
    return a.reshape(tuple(local_shape[:-1]) + (N_DEV * local_shape[-1],))


def kernel(x, c, norm_g, w_mod, b_mod, sb_w_qkv, sb_w_o, s5_lam_re, s5_lam_im, s5_log_dt, s5_b_re, s5_b_im, s5_c_re, s5_c_im, s5_d, s5_w_glu, s5_b_glu, cv_w_pw1, cv_b_pw1, cv_w_dw, cv_b_dw, cv_ln_g, cv_ln_b, cv_w_pw2, cv_b_pw2, ffn_w_gate, ffn_w_up, ffn_w_down, loss_target, m_norm_g, m_w_mod, m_b_mod, m_sb_w_qkv, m_sb_w_o, m_s5_lam_re, m_s5_lam_im, m_s5_log_dt, m_s5_b_re, m_s5_b_im, m_s5_c_re, m_s5_c_im, m_s5_d, m_s5_w_glu, m_s5_b_glu, m_cv_w_pw1, m_cv_b_pw1, m_cv_w_dw, m_cv_b_dw, m_cv_ln_g, m_cv_ln_b, m_cv_w_pw2, m_cv_b_pw2, m_ffn_w_gate, m_ffn_w_up, m_ffn_w_down, v_norm_g, v_w_mod, v_b_mod, v_sb_w_qkv, v_sb_w_o, v_s5_lam_re, v_s5_lam_im, v_s5_log_dt, v_s5_b_re, v_s5_b_im, v_s5_c_re, v_s5_c_im, v_s5_d, v_s5_w_glu, v_s5_b_glu, v_cv_w_pw1, v_cv_b_pw1, v_cv_w_dw, v_cv_b_dw, v_cv_ln_g, v_cv_ln_b, v_cv_w_pw2, v_cv_b_pw2, v_ffn_w_gate, v_ffn_w_up, v_ffn_w_down):
    p = dict(locals())
    me = _me()
    s, d = x.shape[1], x.shape[2]
    depth = norm_g.shape[0]
    h = x.reshape(s, d)
    target = loss_target.reshape(s, d)

    pieces = [p[n].reshape(-1) for n in SMALL_SHARDED] + [c.reshape(-1)]
    parts = _unpack_flat(all_gather(_pack_flat(pieces, F32), "ag_small"), [q.shape[0] for q in pieces])
    full = {n: _unshard_last(part, p[n].shape) for n, part in zip(SMALL_SHARDED, parts)}
    c_all = parts[-1]

    nmod = w_mod.shape[2]
    b_cols = lax.dynamic_slice_in_dim(b_mod, me * nmod, nmod, axis=1)
    mod_cols = mod_fwd(c_all, w_mod, b_cols, "mod_fwd")
    g_mod = all_gather(mod_cols.reshape(depth * N_DEV, nmod), "ag_mod").reshape(N_DEV, depth, N_DEV, nmod)
    mod = jnp.moveaxis(lax.dynamic_index_in_dim(g_mod, me, axis=2, keepdims=False), 0, 1).reshape(depth, N_DEV * nmod)
    ng = full["norm_g"]

    def layer_pieces(l):
        kind, j = l % 3, l // 3
        if kind == 0:
            ps = [("qkv", sb_w_qkv[j], "col"), ("o", sb_w_o[j], "row")]
        elif kind == 1:
            ps = [("glu", s5_w_glu[j], "col")]
        else:
            ps = [("pw1", cv_w_pw1[j], "col"), ("pw2", cv_w_pw2[j], "row")]
        return ps + [("gate", ffn_w_gate[l], "col"), ("up", ffn_w_up[l], "col"), ("down", ffn_w_down[l], "row")]

    def weight_gather(l, ffn):
        return Carried("ag", [a.astype(BF16) for k, a, _ in layer_pieces(l) if (k in FFN_ALL) == ffn])

    def gathered_weights(l, ffn, got):
        out = {}
        for (key, a, how), blk in zip([q for q in layer_pieces(l) if (q[0] in FFN_ALL) == ffn], got):
            r, cc = a.shape
            out[key] = blk.transpose(1, 0, 2).reshape(r, N_DEV * cc) if how == "col" else blk.reshape(N_DEV * r, cc)
        return out

    def grad_exchange(l, grads, keys):
        slabs = []
        for key, a, how in layer_pieces(l):
            if key in keys:
                r, cc = a.shape
                g = grads[key]
                slabs.append(g.reshape(r, N_DEV, cc).transpose(1, 0, 2) if how == "col" else g.reshape(N_DEV, r, cc))
        return Carried("ex", slabs)

    def store_grads(l, keys, got):
        names = {"qkv": "sb_w_qkv", "o": "sb_w_o", "glu": "s5_w_glu", "pw1": "cv_w_pw1", "pw2": "cv_w_pw2",
                 "gate": "ffn_w_gate", "up": "ffn_w_up", "down": "ffn_w_down"}
        for key, parts in zip(keys, got):
            idx = l if key in ("gate", "up", "down") else l // 3
            gbig[names[key]][idx] = sum_slots(parts, f"rs_sum_{key}{l}")

    saved = []
    gbig = {n: [None] * p[n].shape[0] for n in BIG if n != "w_mod"}
    w_next = gathered_weights(0, False, communicate("ag", weight_gather(0, False).operands, "ag_w0"))
    for l in range(depth):
        kind, j = l % 3, l // 3
        w = w_next
        sh_m, sc_m, g_m, sh_f, sc_f, g_f = jnp.split(mod[l], 6)
        st = {"w": w, "h": h}
        if kind == 0:
            (u,) = norm_mod_fwd(h, ng[l, 0], sc_m, sh_m, [BF16], f"nm_a{l}")
            if l == 0:
                qkv, got = mm([(u, w["qkv"])], "nn", BF16, name=f"qkv{l}", car=weight_gather(0, True))
                w.update(gathered_weights(0, True, got))
            else:
                qkv = mm([(u, w["qkv"])], "nn", BF16, name=f"qkv{l}")
            o, *kept = attn_fwd(qkv, f"attn_fwd{l}")
            st.update(kept=kept)
            m = mm([(o, w["o"])], "nn", F32, name=f"attn_o{l}")
            st.update(u=u, qkv=qkv, o=o)
        elif kind == 1:
            (u,) = norm_mod_fwd(h, ng[l, 0], sc_m, sh_m, [F32], f"nm_a{l}")
            ops = s5_operands(s5_lam_re[j], s5_lam_im[j], s5_log_dt[j], s5_b_re[j], s5_b_im[j], s5_c_re[j],
                              s5_c_im[j], d)
            xr, xi, yy, gl = s5_fwd(u, *ops[:5], s5_d[j], f"s5_fwd{l}")
            p1, p2, m = mm_dual(gl, w["glu"], s5_b_glu[j], "glu", F32, F32, f"s5_glu{l}")
            st.update(u=u, ops=ops, xr=xr, xi=xi, yy=yy, gl=gl, p1=p1, p2=p2)
        else:
            (u,) = norm_mod_fwd(h, ng[l, 0], sc_m, sh_m, [BF16], f"nm_a{l}")
            p1, p2, hg = mm_dual(u, w["pw1"], full["cv_b_pw1"][j], "glu", F32, F32, f"cv_pw1{l}")
            hc, hs = conv_mid_fwd(hg, full["cv_w_dw"][j], full["cv_b_dw"][j], full["cv_ln_g"][j],
                                  full["cv_ln_b"][j], f"cv_mid{l}")
            m = mm([(hs, w["pw2"])], "nn", F32, bias=full["cv_b_pw2"][j], name=f"cv_pw2{l}")
            st.update(u=u, p1=p1, p2=p2, hg=hg, hc=hc, hs=hs)
        h2 = resid_fwd(h, m, ng[l, 1], g_m, f"res_a{l}")
        (u2,) = norm_mod_fwd(h2, ng[l, 2], sc_f, sh_f, [BF16], f"nm_f{l}")
        w_gu = (w["gate"], w["up"])
        if l + 1 < depth:
            (f1, f2, z), got = mm_dual(u2, w_gu, None, "swiglu", BF16, BF16, f"ffn_up{l}",
                                       car=weight_gather(l + 1, True))
            f, got_mix = mm([(z, w["down"])], "nn", F32, name=f"ffn_down{l}", car=weight_gather(l + 1, False))
            w_next = {**gathered_weights(l + 1, True, got), **gathered_weights(l + 1, False, got_mix)}
        else:
            f1, f2, z = mm_dual(u2, w_gu, None, "swiglu", BF16, BF16, f"ffn_up{l}")
            f = mm([(z, w["down"])], "nn", F32, name=f"ffn_down{l}")
        h = resid_fwd(h2, f, ng[l, 3], g_f, f"res_f{l}")
        st.update(m=m, h2=h2, u2=u2, f1=f1, f2=f2, z=z, f=f)
        saved.append(st)

    loss_arr, dh = loss_and_grad(h, target, "loss")
    loss = lax.psum(loss_arr[0, 0], AXES)

    nl_sb, nl_s5, nl_cv = sb_w_qkv.shape[0], s5_w_glu.shape[0], cv_w_pw1.shape[0]
    pending = None
    dng = [None] * depth
    dmod = [None] * depth
    gs5 = {n: [None] * nl_s5 for n in SMALL if n.startswith("s5_")}
    gcv = {n: [None] * nl_cv for n in SMALL if n.startswith("cv_")}
    for l in reversed(range(depth)):
        kind, j = l % 3, l // 3
        st = saved[l]
        w = st["w"]
        sh_m, sc_m, g_m, sh_f, sc_f, g_f = jnp.split(mod[l], 6)
        gw = {}
        df, s_rf = resid_bwd(dh, st["f"], ng[l, 3], g_f, BF16, f"res_f_bwd{l}")
        rest = None if pending is None else [k for k, _, _ in layer_pieces(pending[0]) if k not in FFN_KEYS]
        riding = lambda keys: None if pending is None else grad_exchange(pending[0], pending[1], keys)

        def landed(keys, res):
            if pending is None:
                return res
            store_grads(pending[0], keys, res[1])
            return res[0]

        d1, d2 = landed(FFN_KEYS[:1], mm_act_bwd(df, w["down"], st["f1"], st["f2"], "swiglu", f"ffn_dz{l}",
                                                 car=riding(FFN_KEYS[:1])))
        gw["down"] = landed(FFN_KEYS[1:], mm([(st["z"], df)], "tn", BF16, name=f"ffn_dwd{l}", car=riding(FFN_KEYS[1:])))
        du2 = landed(rest, mm([(d1, w["gate"]), (d2, w["up"])], "nt", F32, name=f"ffn_du{l}", car=riding(rest)))
        gw["gate"] = mm([(st["u2"], d1)], "tn", BF16, name=f"ffn_dwg{l}")
        gw["up"] = mm([(st["u2"], d2)], "tn", BF16, name=f"ffn_dwu{l}")
        dh2, s_nf = norm_mod_bwd([du2], st["h2"], dh, ng[l, 2], sc_f, f"nm_f_bwd{l}")
        dm, s_rm = resid_bwd(dh2, st["m"], ng[l, 1], g_m, F32 if kind == 1 else BF16, f"res_a_bwd{l}")
        if kind == 0:
            do = mm([(dm, w["o"])], "nt", BF16, name=f"attn_do{l}")
            gw["o"] = mm([(st["o"], dm)], "tn", BF16, name=f"attn_dwo{l}")
            dq, dk, dv = attn_bwd(st["qkv"], do, st["kept"], f"attn_bwd{l}")
            wq = w["qkv"]
            pairs = [(dq, wq[:, :d]), (dk, wq[:, d:2 * d]), (dv, wq[:, 2 * d:])]
            if l == 0:
                du, got = mm(pairs, "nt", F32, name=f"qkv_du{l}", car=grad_exchange(l, gw, FFN_ALL))
                store_grads(l, FFN_ALL, got)
                dus = [du]
            else:
                dus = [mm(pairs, "nt", F32, name=f"qkv_du{l}")]
            gw["qkv"] = jnp.concatenate([mm([(st["u"], t)], "tn", BF16, name=f"qkv_dw{l}_{i}")
                                         for i, t in enumerate((dq, dk, dv))], axis=1)
        elif kind == 1:
            d1, d2, cs = dual_bwd(dm, st["p1"], st["p2"], "glu", f"s5_glu_bwd{l}")
            gs5["s5_b_glu"][j] = jnp.concatenate([cs[0], cs[1]])
            wg = w["glu"]
            dgl = mm([(d1, wg[:, :d]), (d2, wg[:, d:])], "nt", F32, name=f"s5_dgl{l}")
            gw["glu"] = jnp.concatenate([mm([(st["gl"], t)], "tn", BF16, name=f"s5_dwglu{l}_{i}")
                                         for i, t in enumerate((d1, d2))], axis=1)
            ops = st["ops"]
            du, dwbr, dwbi, dwcr, dwci, da, dd = s5_bwd(dgl, st["yy"], st["u"], st["xr"], st["xi"], *ops[:4],
                                                        ops[5], s5_d[j], f"s5_bwd{l}")
            dus = [du]
            ngrp = s5_lam_re.shape[1]
            ext = lambda t: _blockdiag_extract(t, ngrp).transpose(0, 2, 1)
            _, disc_vjp = jax.vjp(_s5_discretize, s5_lam_re[j], s5_lam_im[j], s5_log_dt[j], s5_b_re[j], s5_b_im[j])
            shp = s5_lam_re[j].shape
            dlr, dli, dldt, dbr, dbi = disc_vjp((da[0].reshape(shp), da[1].reshape(shp), ext(dwbr), ext(dwbi)))
            for n, t in (("s5_lam_re", dlr), ("s5_lam_im", dli), ("s5_log_dt", dldt), ("s5_b_re", dbr),
                         ("s5_b_im", dbi), ("s5_c_re", ext(dwcr)), ("s5_c_im", -ext(dwci)), ("s5_d", dd[0])):
                gs5[n][j] = t
        else:
            dhs = mm([(dm, w["pw2"])], "nt", BF16, name=f"cv_dhs{l}")
            gw["pw2"] = mm([(st["hs"], dm)], "tn", BF16, name=f"cv_dwpw2{l}")
            dhg, dwdw, s_cv = conv_mid_bwd(dhs, st["hc"], st["hg"], full["cv_w_dw"][j], full["cv_ln_g"][j],
                                           full["cv_ln_b"][j], f"cv_mid_bwd{l}")
            d1, d2, cs = dual_bwd(dhg, st["p1"], st["p2"], "glu", f"cv_glu_bwd{l}")
            wp = w["pw1"]
            dus = [mm([(d1, wp[:, :d]), (d2, wp[:, d:])], "nt", F32, name=f"cv_du{l}")]
            gw["pw1"] = jnp.concatenate([mm([(st["u"], t)], "tn", BF16, name=f"cv_dwpw1{l}_{i}")
                                         for i, t in enumerate((d1, d2))], axis=1)
            for n, t in (("cv_b_pw1", jnp.concatenate([cs[0], cs[1]])), ("cv_w_dw", dwdw[:cv_w_dw.shape[1]]),
                         ("cv_b_dw", s_cv[0]), ("cv_ln_g", s_cv[1]), ("cv_ln_b", s_cv[2]), ("cv_b_pw2", s_rm[2])):
                gcv[n][j] = t
        dh, s_nm = norm_mod_bwd(dus, st["h"], dh2, ng[l, 0], sc_m, f"nm_a_bwd{l}")
        dng[l] = jnp.stack([s_nm[2], s_rm[1], s_nf[2], s_rf[1]])
        dmod[l] = jnp.concatenate([s_nm[0], s_nm[1], s_rm[0], s_nf[0], s_nf[1], s_rf[0]])
        pending = (l, gw)
    keys = [key for key, _, _ in layer_pieces(pending[0]) if key not in FFN_ALL]
    store_grads(pending[0], keys, communicate("ex", grad_exchange(pending[0], pending[1], keys).operands, "rs_x_last"))

    local = {"norm_g": jnp.stack(dng), "b_mod": jnp.stack(dmod)}
    local.update({n: jnp.stack(t) for n, t in gs5.items()})
    local.update({n: jnp.stack(t) for n, t in gcv.items()})
    pieces = [local[n].reshape(-1) for n in SMALL]
    sizes = [q.shape[0] for q in pieces]
    gathered = all_gather(_pack_flat(pieces, F32), "ag_grads")
    sums = _unpack_flat(sum_slots(gathered, "sum_grads"), sizes)
    grads = {}
    for n, t in zip(SMALL, sums):
        t = t.reshape(local[n].shape)
        if n in SMALL_SHARDED:
            nsh = p[n].shape[-1]
            t = lax.dynamic_slice_in_dim(t, me * nsh, nsh, axis=t.ndim - 1)
        grads[n] = t.reshape(p[n].shape)
    dmod_all = _unpack_flat(gathered, sizes)[SMALL.index("b_mod")].reshape(N_DEV, depth, N_DEV * nmod)
    dmod_cols = jnp.moveaxis(lax.dynamic_slice_in_dim(dmod_all, me * nmod, nmod, axis=2), 0, 1)
    grads["w_mod"] = mod_bwd(c_all, dmod_cols, "mod_bwd")
    for n in gbig:
        grads[n] = jnp.stack(gbig[n])

    delta, new_m, new_v = {}, {}, {}
    for n in BIG:
        shp = p[n].shape
        two = lambda t: t.reshape(-1, shp[-1])
        delta[n], new_m[n], new_v[n] = (t.reshape(shp) for t in
                                        adamw(two(p[n]), two(grads[n]), two(p["m_" + n]), two(p["v_" + n]), f"adamw_{n}"))
    sizes = [p[n].size for n in SMALL]
    packs = [_pack_flat([t[n].reshape(-1) for n in SMALL], F32)
             for t in (p, grads, {n: p["m_" + n] for n in SMALL}, {n: p["v_" + n] for n in SMALL})]
    for res, out in zip(adamw(*packs, "adamw_small"), (delta, new_m, new_v)):
        for n, t in zip(SMALL, _unpack_flat(res, sizes)):
            out[n] = t.reshape(p[n].shape)

    return (loss, dh.reshape(x.shape), *[grads[n] for n in WEIGHTS], *[delta[n] for n in WEIGHTS],
            *[new_m[n] for n in WEIGHTS], *[new_v[n] for n in WEIGHTS])
```
